```python
import jax, jax.numpy as jnp
from jax import lax
import numpy as np

D_MODEL = 4096
BATCH = 4
SEQ = 2048
DEPTH = 4
DEC_BATCH = 8
DEC_SEQ = 64
PAST_LEN = 2048

CHUNK = 64
N_EVEN = (DEPTH + 1) // 2
N_ODD = DEPTH // 2
M_HEADS = 4
M_DQK = D_MODEL // 16
M_DV = D_MODEL // 8
S_HEADS = D_MODEL // 128
S_KV_HEADS = S_HEADS // 8
S_GROUP = S_HEADS // S_KV_HEADS
S_HEAD_DIM = 64
WINDOW = 128
N_WIN_CHUNKS = WINDOW // CHUNK
W_BUF = min(WINDOW, PAST_LEN)
G_HEADS = 8
G_DK = D_MODEL // 16
G_DV = D_MODEL // 8
G_RANK = 16
G_TAU = 16.0
D_FF = ((8 * D_MODEL // 3 + 255) // 256) * 256
M_QK_W = M_HEADS * M_DQK
M_V_W = M_HEADS * M_DV
S_Q_W = S_HEADS * S_HEAD_DIM
S_KV_W = S_KV_HEADS * S_HEAD_DIM
EVEN_IN = 2 * M_QK_W + 2 * M_V_W + 2 * M_HEADS + S_Q_W + 2 * S_KV_W
EVEN_MIX = M_V_W + S_Q_W
G_K_W = G_HEADS * G_DK
G_V_W = G_HEADS * G_DV
ODD_IN = 2 * G_K_W + 2 * G_V_W + G_RANK
ODD_MIX = G_V_W
EPS = 1e-6

kernel_name = 'hybrid_mlstm_swa_gla_stream'

F32 = jnp.float32


def rmsnorm(x, g):
    xf = x.astype(F32)
    y = xf * lax.rsqrt(jnp.mean(xf * xf, axis=-1, keepdims=True) + EPS)
    return (y * g.astype(F32)).astype(x.dtype)


def head_rmsnorm(h, g):
    hf = h.astype(F32)
    y = hf * lax.rsqrt(jnp.mean(hf * hf, axis=-1, keepdims=True) + EPS)
    return y.reshape(h.shape[:-2] + (-1,)) * g.astype(F32)


def split_cols(z, sizes):
    idx = [int(i) for i in np.cumsum(sizes)[:-1]]
    return jnp.split(z, idx, axis=-1)


def to_chunks(x):
    B, T = x.shape[:2]
    return jnp.moveaxis(x.reshape((B, T // CHUNK, CHUNK) + x.shape[2:]), 1, 0)


def from_chunks(x):
    NC, B, L = x.shape[:3]
    return jnp.moveaxis(x, 0, 1).reshape((B, NC * L) + x.shape[3:])


def swiglu(h, w_in, w_out):
    g, u = jnp.split(h @ w_in, 2, axis=-1)
    return (jax.nn.silu(g) * u) @ w_out


def mlstm_chunk(carry, inp):
    C, n, m = carry
    q, k, v, ig, lf = inp
    L = q.shape[1]
    causal = jnp.tril(jnp.ones((L, L), dtype=bool))
    b = jnp.swapaxes(jnp.cumsum(lf, axis=1), 1, 2)
    igh = jnp.swapaxes(ig, 1, 2)
    d = jnp.where(causal, b[..., :, None] - b[..., None, :] + igh[..., None, :], -jnp.inf)
    inter = b + m[..., None]
    m_t = jnp.maximum(inter, jnp.max(d, axis=-1))
    w_intra = jnp.exp(d - m_t[..., None])
    w_inter = jnp.exp(inter - m_t)
    a = jnp.einsum('blhd,bshd->bhls', q, k) * w_intra
    w_inter_l = jnp.swapaxes(w_inter, 1, 2)[..., None]
    num = jnp.einsum('bhls,bshv->blhv', a, v) + jnp.einsum('blhd,bhdv->blhv', q, C) * w_inter_l
    den = jnp.sum(a, axis=-1) + jnp.einsum('blhd,bhd->bhl', q, n) * w_inter
    den = jnp.maximum(jnp.abs(den), jnp.exp(-m_t))
    h = num / jnp.swapaxes(den, 1, 2)[..., None]
    m_new = m_t[..., -1]
    w_last = jnp.exp(b[..., -1:] - b + igh - m_new[..., None])
    decay = jnp.exp(b[..., -1] + m - m_new)
    k_w = k * jnp.swapaxes(w_last, 1, 2)[..., None]
    C_new = decay[..., None, None] * C + jnp.einsum('bshd,bshv->bhdv', k_w, v)
    n_new = decay[..., None] * n + jnp.sum(k_w, axis=1)
    return (C_new, n_new, m_new), h


def gla_chunk(S, inp):
    q, k, v, lg = inp
    L = q.shape[1]
    causal = jnp.tril(jnp.ones((L, L), dtype=bool))[None, :, :, None, None]
    bc = jnp.cumsum(lg, axis=1)
    rel = jnp.exp(jnp.where(causal, bc[:, :, None] - bc[:, None, :], -jnp.inf))
    a = jnp.einsum('bthc,bshc,btshc->bhts', q, k, rel)
    o = jnp.einsum('bhts,bshv->bthv', a, v) + jnp.einsum('bthc,bhcv->bthv', q * jnp.exp(bc), S)
    b_last = bc[:, -1]
    k_dec = k * jnp.exp(b_last[:, None] - bc)
    S_new = jnp.exp(b_last)[..., None] * S + jnp.einsum('bshc,bshv->bhcv', k_dec, v)
    return S_new, o


def swa_attend(q, k, v, qpos, kpos, sinks):
    B, N, Tq = q.shape[:3]
    qg = q.reshape(B, N, Tq, S_KV_HEADS, S_GROUP, S_HEAD_DIM)
    s = jnp.einsum('bnqgrd,bnkgd->bngrqk', qg, k).astype(F32) * (S_HEAD_DIM ** -0.5)
    slopes = jnp.exp2(-8.0 * jnp.arange(1, S_HEADS + 1, dtype=F32) / S_HEADS).reshape(S_KV_HEADS, S_GROUP)
    dist = jnp.abs(qpos[:, :, None] - kpos[:, None, :]).astype(F32)
    s = s - slopes[None, None, :, :, None, None] * dist[None, :, None, None]
    qc = qpos // CHUNK
    kc = kpos // CHUNK
    mask = (kpos[:, None, :] >= 0) & (kc[:, None, :] <= qc[:, :, None]) & (kc[:, None, :] >= qc[:, :, None] - N_WIN_CHUNKS)
    s = jnp.where(mask[None, :, None, None], s, -jnp.inf)
    sink = sinks.astype(F32).reshape(S_KV_HEADS, S_GROUP)[None, None, :, :, None, None]
    mx = jnp.maximum(jnp.max(s, axis=-1, keepdims=True), sink)
    p = jnp.exp(s - mx)
    p = p / (jnp.sum(p, axis=-1, keepdims=True) + jnp.exp(sink - mx))
    o = jnp.einsum('bngrqk,bnkgd->bnqgrd', p.astype(v.dtype), v)
    return o.reshape(B, N, Tq, S_Q_W)


def even_project(h, w_in, b_i, b_f):
    B, T = h.shape[:2]
    mq, mk, mv, mo, mi, mf, sq, sk, sv = split_cols(
        h @ w_in, [M_QK_W, M_QK_W, M_V_W, M_V_W, M_HEADS, M_HEADS, S_Q_W, S_KV_W, S_KV_W])
    mq = mq.astype(F32).reshape(B, T, M_HEADS, M_DQK)
    mk = mk.astype(F32).reshape(B, T, M_HEADS, M_DQK) * (M_DQK ** -0.5)
    mv = mv.astype(F32).reshape(B, T, M_HEADS, M_DV)
    ig = (mi.astype(F32) + b_i.astype(F32))
    lf = jax.nn.log_sigmoid(mf.astype(F32) + b_f.astype(F32))
    sq = sq.reshape(B, T, S_HEADS, S_HEAD_DIM)
    sk = sk.reshape(B, T, S_KV_HEADS, S_HEAD_DIM)
    sv = sv.reshape(B, T, S_KV_HEADS, S_HEAD_DIM)
    return mq, mk, mv, mo, ig, lf, sq, sk, sv


def even_output(hm, mo, hs, g_head, w_out):
    hm = head_rmsnorm(hm, g_head) * jax.nn.sigmoid(mo.astype(F32))
    return jnp.concatenate([hm, hs.astype(F32)], axis=-1).astype(w_out.dtype) @ w_out


def even_mixer_prompt(h, w_in, b_i, b_f, g_head, sinks, w_out):
    B, T = h.shape[:2]
    NC = T // CHUNK
    mq, mk, mv, mo, ig, lf, sq, sk, sv = even_project(h, w_in, b_i, b_f)
    carry0 = (jnp.zeros((B, M_HEADS, M_DQK, M_DV), F32), jnp.zeros((B, M_HEADS, M_DQK), F32),
              jnp.zeros((B, M_HEADS), F32))
    (C, n, m), hc = lax.scan(mlstm_chunk, carry0,
                             (to_chunks(mq), to_chunks(mk), to_chunks(mv), to_chunks(ig), to_chunks(lf)))
    hm = from_chunks(hc)
    pad = N_WIN_CHUNKS * CHUNK
    kp = jnp.pad(sk, ((0, 0), (pad, 0), (0, 0), (0, 0))).reshape(B, NC + N_WIN_CHUNKS, CHUNK, S_KV_HEADS, S_HEAD_DIM)
    vp = jnp.pad(sv, ((0, 0), (pad, 0), (0, 0), (0, 0))).reshape(B, NC + N_WIN_CHUNKS, CHUNK, S_KV_HEADS, S_HEAD_DIM)
    kb = jnp.concatenate([kp[:, j:j + NC] for j in range(N_WIN_CHUNKS + 1)], axis=2)
    vb = jnp.concatenate([vp[:, j:j + NC] for j in range(N_WIN_CHUNKS + 1)], axis=2)
    qb = sq.reshape(B, NC, CHUNK, S_HEADS, S_HEAD_DIM)
    blk = jnp.arange(NC, dtype=jnp.int32)[:, None] * CHUNK
    qpos = blk + jnp.arange(CHUNK, dtype=jnp.int32)[None]
    kpos = blk - pad + jnp.arange((N_WIN_CHUNKS + 1) * CHUNK, dtype=jnp.int32)[None]
    hs = swa_attend(qb, kb, vb, qpos, kpos, sinks).reshape(B, T, S_Q_W)
    y = even_output(hm, mo, hs, g_head, w_out)
    return y, (sk[:, -W_BUF:], sv[:, -W_BUF:], C, n, m)


def even_mixer_sample(h, ck, cv, cC, cn, cm, w_in, b_i, b_f, g_head, sinks, w_out):
    B, T = h.shape[:2]
    mq, mk, mv, mo, ig, lf, sq, sk, sv = even_project(h, w_in, b_i, b_f)
    (C, n, m), hm = mlstm_chunk((cC.astype(F32), cn.astype(F32), cm.astype(F32)), (mq, mk, mv, ig, lf))
    k_all = jnp.concatenate([ck.astype(sk.dtype), sk], axis=1)
    v_all = jnp.concatenate([cv.astype(sv.dtype), sv], axis=1)
    qpos = (PAST_LEN + jnp.arange(T, dtype=jnp.int32))[None]
    kpos = jnp.concatenate([PAST_LEN - W_BUF + jnp.arange(W_BUF, dtype=jnp.int32),
                            PAST_LEN + jnp.arange(T, dtype=jnp.int32)])[None]
    hs = swa_attend(sq[:, None], k_all[:, None], v_all[:, None], qpos, kpos, sinks).reshape(B, T, S_Q_W)
    y = even_output(hm, mo, hs, g_head, w_out)
    return y, (k_all[:, -W_BUF:], v_all[:, -W_BUF:], C, n, m)


def odd_project(h, w_in, w_g2, b_g):
    B, T = h.shape[:2]
    gq, gk, gv, gr, glr = split_cols(h @ w_in, [G_K_W, G_K_W, G_V_W, G_V_W, G_RANK])
    q = gq.astype(F32).reshape(B, T, G_HEADS, G_DK) * (G_DK ** -0.5)
    k = gk.astype(F32).reshape(B, T, G_HEADS, G_DK)
    v = gv.astype(F32).reshape(B, T, G_HEADS, G_DV)
    lg = jax.nn.log_sigmoid((glr @ w_g2).astype(F32) + b_g.astype(F32)) / G_TAU
    lg = lg.reshape(B, T, G_HEADS, G_DK)
    return q, k, v, gr, lg


def odd_output(o, r, g_head, w_out):
    y = head_rmsnorm(o, g_head) * jax.nn.silu(r.astype(F32))
    return y.astype(w_out.dtype) @ w_out


def odd_mixer_prompt(h, w_in, w_g2, b_g, g_head, w_out):
    B, T = h.shape[:2]
    q, k, v, r, lg = odd_project(h, w_in, w_g2, b_g)
    S0 = jnp.zeros((B, G_HEADS, G_DK, G_DV), F32)
    S, oc = lax.scan(gla_chunk, S0, (to_chunks(q), to_chunks(k), to_chunks(v), to_chunks(lg)))
    return odd_output(from_chunks(oc), r, g_head, w_out), S


def odd_mixer_sample(h, cS, w_in, w_g2, b_g, g_head, w_out):
    q, k, v, r, lg = odd_project(h, w_in, w_g2, b_g)
    S, o = gla_chunk(cS.astype(F32), (q, k, v, lg))
    return odd_output(o, r, g_head, w_out), S


def setup_inputs(seed: int = 0) -> dict:
    key = jax.random.key(seed)
    ks = jax.random.split(key, 24)
    nrm = jax.random.normal
    d = D_MODEL
    return {
        'x_prompt': nrm(ks[0], (BATCH, SEQ, d), F32),
        'x_sample': nrm(ks[1], (DEC_BATCH, DEC_SEQ, d), F32),
        'cache_swa_k': nrm(ks[2], (N_EVEN, DEC_BATCH, W_BUF, S_KV_HEADS, S_HEAD_DIM), F32),
        'cache_swa_v': nrm(ks[3], (N_EVEN, DEC_BATCH, W_BUF, S_KV_HEADS, S_HEAD_DIM), F32),
        'state_mlstm_C': 0.5 * nrm(ks[4], (N_EVEN, DEC_BATCH, M_HEADS, M_DQK, M_DV), F32),
        'state_mlstm_n': jnp.abs(nrm(ks[5], (N_EVEN, DEC_BATCH, M_HEADS, M_DQK), F32)),
        'state_mlstm_m': 0.5 * nrm(ks[6], (N_EVEN, DEC_BATCH, M_HEADS), F32),
        'state_gla_S': nrm(ks[7], (N_ODD, DEC_BATCH, G_HEADS, G_DK, G_DV), F32),
        'norm_mix': 1.0 + 0.05 * nrm(ks[8], (DEPTH, d), F32),
        'norm_ffn': 1.0 + 0.05 * nrm(ks[9], (DEPTH, d), F32),
        'norm_final': 1.0 + 0.05 * nrm(ks[10], (d,), F32),
        'w_even_in': nrm(ks[11], (N_EVEN, d, EVEN_IN), F32) * d ** -0.5,
        'b_mlstm_i': 0.1 * nrm(ks[12], (N_EVEN, M_HEADS), F32),
        'b_mlstm_f': 3.0 + 0.5 * nrm(ks[13], (N_EVEN, M_HEADS), F32),
        'mlstm_head_norm': 1.0 + 0.05 * nrm(ks[14], (N_EVEN, M_V_W), F32),
        'swa_sinks': nrm(ks[15], (N_EVEN, S_HEADS), F32),
        'w_even_out': nrm(ks[16], (N_EVEN, EVEN_MIX, d), F32) * EVEN_MIX ** -0.5,
        'w_odd_in': nrm(ks[17], (N_ODD, d, ODD_IN), F32) * d ** -0.5,
        'w_gla_gate': nrm(ks[18], (N_ODD, G_RANK, G_K_W), F32) * G_RANK ** -0.5,
        'b_gla_gate': 0.5 * nrm(ks[19], (N_ODD, G_K_W), F32),
        'gla_head_norm': 1.0 + 0.05 * nrm(ks[20], (N_ODD, G_V_W), F32),
        'w_odd_out': nrm(ks[21], (N_ODD, ODD_MIX, d), F32) * ODD_MIX ** -0.5,
        'w_ffn_in': nrm(ks[22], (DEPTH, d, 2 * D_FF), F32) * d ** -0.5,
        'w_ffn_out': nrm(ks[23], (DEPTH, D_FF, d), F32) * D_FF ** -0.5,
    }


def reference(x_prompt, x_sample, cache_swa_k, cache_swa_v, state_mlstm_C, state_mlstm_n, state_mlstm_m,
              state_gla_S, norm_mix, norm_ffn, norm_final, w_even_in, b_mlstm_i, b_mlstm_f, mlstm_head_norm,
              swa_sinks, w_even_out, w_odd_in, w_gla_gate, b_gla_gate, gla_head_norm, w_odd_out,
              w_ffn_in, w_ffn_out):
    xp, xs = x_prompt, x_sample
    p_k, p_v, p_C, p_n, p_m, p_S = [], [], [], [], [], []
    s_k, s_v, s_C, s_n, s_m, s_S = [], [], [], [], [], []
    for l in range(DEPTH):
        hp = rmsnorm(xp, norm_mix[l])
        hs = rmsnorm(xs, norm_mix[l])
        e = l // 2
        if l % 2 == 0:
            yp, (k1, v1, C1, n1, m1) = even_mixer_prompt(hp, w_even_in[e], b_mlstm_i[e], b_mlstm_f[e],
                                                         mlstm_head_norm[e], swa_sinks[e], w_even_out[e])
            ys, (k2, v2, C2, n2, m2) = even_mixer_sample(hs, cache_swa_k[e], cache_swa_v[e], state_mlstm_C[e],
                                                         state_mlstm_n[e], state_mlstm_m[e], w_even_in[e],
                                                         b_mlstm_i[e], b_mlstm_f[e], mlstm_head_norm[e],
                                                         swa_sinks[e], w_even_out[e])
            p_k.append(k1); p_v.append(v1); p_C.append(C1); p_n.append(n1); p_m.append(m1)
            s_k.append(k2); s_v.append(v2); s_C.append(C2); s_n.append(n2); s_m.append(m2)
        else:
            yp, S1 = odd_mixer_prompt(hp, w_odd_in[e], w_gla_gate[e], b_gla_gate[e], gla_head_norm[e], w_odd_out[e])
            ys, S2 = odd_mixer_sample(hs, state_gla_S[e], w_odd_in[e], w_gla_gate[e], b_gla_gate[e],
                                      gla_head_norm[e], w_odd_out[e])
            p_S.append(S1)
            s_S.append(S2)
        xp = xp + yp
        xs = xs + ys
        xp = xp + swiglu(rmsnorm(xp, norm_ffn[l]), w_ffn_in[l], w_ffn_out[l])
        xs = xs + swiglu(rmsnorm(xs, norm_ffn[l]), w_ffn_in[l], w_ffn_out[l])
    y_prompt = rmsnorm(xp, norm_final)
    y_sample = rmsnorm(xs, norm_final)
    return (y_prompt, y_sample,
            jnp.stack(p_k), jnp.stack(p_v), jnp.stack(p_C), jnp.stack(p_n), jnp.stack(p_m), jnp.stack(p_S),
            jnp.stack(s_k), jnp.stack(s_v), jnp.stack(s_C), jnp.stack(s_n), jnp.stack(s_m), jnp.stack(s_S))
```

```python
import functools

import jax
import jax.numpy as jnp
from jax import lax
from jax.experimental import pallas as pl
from jax.experimental.pallas import tpu as pltpu

F32 = jnp.float32
BF16 = jnp.bfloat16

CHUNK = 64
SWA_HEAD_DIM = 64
SWA_GROUP = 8
LANES = 128
GLA_SUB = 16
GLA_TAU = 16.0
EPS = 1e-6
NEG_INF = float("-inf")

V7X_VMEM_BYTES = 64 * 1024 * 1024
VMEM_CAP_BYTES = V7X_VMEM_BYTES - 6 * 1024 * 1024

NT = (((1,), (1,)), ((), ()))
TN = (((0,), (0,)), ((), ()))


def _pick(n, cands):
    for c in cands:
        if n % c == 0:
            return c
    raise ValueError(f"no tile in {cands} divides {n}")


def _params(semantics, block_bytes):
    limit = min(VMEM_CAP_BYTES, block_bytes + 16 * 1024 * 1024)
    return pltpu.CompilerParams(dimension_semantics=semantics, vmem_limit_bytes=int(limit))


def _log_sigmoid(x):
    return jnp.minimum(x, 0.0) - jnp.log(1.0 + jnp.exp(-jnp.abs(x)))


def _sigmoid(x):
    return 1.0 / (1.0 + jnp.exp(-x))


def _split3(x):
    hi = x.astype(BF16)
    r1 = x - hi.astype(F32)
    mid = r1.astype(BF16)
    lo = (r1 - mid.astype(F32)).astype(BF16)
    return hi, mid, lo


def _rmsnorm_kernel(x_ref, g_ref, o_ref):
    x = x_ref[...]
    y = x * lax.rsqrt(jnp.mean(x * x, axis=-1, keepdims=True) + EPS)
    o_ref[...] = (y * g_ref[...]).astype(o_ref.dtype)


def rmsnorm(x, g, out_dtype):
    T, D = x.shape
    tr = _pick(T, (272, 256, 160, 128, 64))
    nbytes = 2 * tr * D * (4 + jnp.dtype(out_dtype).itemsize)
    return pl.pallas_call(
        _rmsnorm_kernel,
        grid=(T // tr,),
        in_specs=[pl.BlockSpec((tr, D), lambda i: (i, 0)), pl.BlockSpec((1, D), lambda i: (0, 0))],
        out_specs=pl.BlockSpec((tr, D), lambda i: (i, 0)),
        out_shape=jax.ShapeDtypeStruct((T, D), out_dtype),
        compiler_params=_params(("parallel",), nbytes),
        name="rmsnorm",
    )(x, g.reshape(1, D))


def _mm_kernel(x_ref, w_ref, o_ref):
    o_ref[...] = jnp.dot(x_ref[...], w_ref[...], preferred_element_type=F32).astype(o_ref.dtype)


def _mm_res_kernel(x_ref, w_ref, r_ref, o_ref):
    o_ref[...] = r_ref[...] + jnp.dot(x_ref[...], w_ref[...], preferred_element_type=F32)


def _m_tile(T):
    return _pick(T, (1088, 1024, 640, 512, 320, 256, 128, 64))


def matmul(x, w, out_dtype=F32, residual=None, name="matmul"):
    T, K = x.shape
    N = w.shape[1]
    tm = _m_tile(T)
    tn = _pick(N, (512, 256, 128))
    osz = jnp.dtype(out_dtype).itemsize
    nbytes = 2 * (tm * K * 2 + K * tn * 2 + tm * tn * osz) + tm * tn * 4
    in_specs = [pl.BlockSpec((tm, K), lambda i, j: (i, 0)), pl.BlockSpec((K, tn), lambda i, j: (0, j))]
    args = [x, w]
    kern = _mm_kernel
    if residual is not None:
        in_specs.append(pl.BlockSpec((tm, tn), lambda i, j: (i, j)))
        args.append(residual)
        kern = _mm_res_kernel
        nbytes += 2 * tm * tn * 4
    return pl.pallas_call(
        kern,
        grid=(T // tm, N // tn),
        in_specs=in_specs,
        out_specs=pl.BlockSpec((tm, tn), lambda i, j: (i, j)),
        out_shape=jax.ShapeDtypeStruct((T, N), out_dtype),
        compiler_params=_params(("parallel", "arbitrary"), nbytes),
        name=name,
    )(*args)


def _ffn_in_kernel(x_ref, wg_ref, wu_ref, o_ref):
    x = x_ref[...]
    g = jnp.dot(x, wg_ref[...], preferred_element_type=F32)
    u = jnp.dot(x, wu_ref[...], preferred_element_type=F32)
    o_ref[...] = (g * _sigmoid(g) * u).astype(o_ref.dtype)


def ffn_in(x, w_in):
    T, K = x.shape
    F = w_in.shape[1] // 2
    tm = _m_tile(T)
    tn = _pick(F, (256, 128))
    nf = F // tn
    nbytes = 2 * (tm * K * 2 + 2 * K * tn * 2 + tm * tn * 2) + 3 * tm * tn * 4
    return pl.pallas_call(
        _ffn_in_kernel,
        grid=(T // tm, nf),
        in_specs=[pl.BlockSpec((tm, K), lambda i, j: (i, 0)),
                  pl.BlockSpec((K, tn), lambda i, j: (0, j)),
                  pl.BlockSpec((K, tn), lambda i, j: (0, j + nf))],
        out_specs=pl.BlockSpec((tm, tn), lambda i, j: (i, j)),
        out_shape=jax.ShapeDtypeStruct((T, F), BF16),
        compiler_params=_params(("parallel", "arbitrary"), nbytes),
        name="ffn_in",
    )(x, w_in, w_in)


def _mm_res_ksplit_kernel(x_ref, w_ref, r_ref, o_ref, acc_ref):
    k = pl.program_id(2)

    @pl.when(k == 0)
    def _():
        acc_ref[...] = r_ref[...]

    acc_ref[...] += jnp.dot(x_ref[...], w_ref[...], preferred_element_type=F32)

    @pl.when(k == pl.num_programs(2) - 1)
    def _():
        o_ref[...] = acc_ref[...]


def matmul_res_ksplit(x, w, residual):
    T, K = x.shape
    N = w.shape[1]
    tm = _m_tile(T)
    tn = _pick(N, (512, 256, 128))
    nk = 2
    tk = K // nk
    assert tk * nk == K and tk % LANES == 0
    nbytes = 2 * (tm * tk * 2 + tk * tn * 2 + 2 * tm * tn * 4) + 2 * tm * tn * 4
    return pl.pallas_call(
        _mm_res_ksplit_kernel,
        grid=(T // tm, N // tn, nk),
        in_specs=[pl.BlockSpec((tm, tk), lambda i, j, k: (i, k)),
                  pl.BlockSpec((tk, tn), lambda i, j, k: (k, j)),
                  pl.BlockSpec((tm, tn), lambda i, j, k: (i, j))],
        out_specs=pl.BlockSpec((tm, tn), lambda i, j, k: (i, j)),
        out_shape=jax.ShapeDtypeStruct((T, N), F32),
        scratch_shapes=[pltpu.VMEM((tm, tn), F32)],
        compiler_params=_params(("parallel", "arbitrary", "arbitrary"), nbytes),
        name="ffn_out",
    )(x, w, residual)


class Units:
    def __init__(self, n_prompt, chunks_per_prompt, n_sample):
        self.ncp = chunks_per_prompt
        self.up = n_prompt * chunks_per_prompt
        self.n_prompt = n_prompt
        self.n_sample = n_sample
        self.total = self.up + n_sample
        self.n_streams = n_prompt + n_sample

    def is_sample(self, u):
        return u >= self.up

    def stream(self, u):
        return jnp.where(u < self.up, u // self.ncp, self.n_prompt + u - self.up)

    def sample_index(self, u):
        return jnp.maximum(u - self.up, 0)

    def first(self, u):
        return jnp.logical_or(u >= self.up, u % self.ncp == 0)

    def last(self, u):
        return jnp.logical_or(u >= self.up, u % self.ncp == self.ncp - 1)


def _mlstm_kernel(bi_ref, bf_ref, m0_ref, q_ref, k_ref, v_ref, og_ref, gate_ref, gh_ref, c0_ref, n0_ref,
                  hm_ref, cout_ref, nout_ref, mout_ref, c_s, n_s, m_s, *, units, n_heads):
    h = pl.program_id(0)
    u = pl.program_id(1)
    is_sample = units.is_sample(u)
    L = CHUNK
    dqk = q_ref.shape[1]

    @pl.when(jnp.logical_and(units.first(u), jnp.logical_not(is_sample)))
    def _():
        c_s[...] = jnp.zeros_like(c_s)
        n_s[...] = jnp.zeros_like(n_s)
        m_s[...] = jnp.zeros_like(m_s)

    @pl.when(is_sample)
    def _():
        c_s[...] = c0_ref[...]
        n_s[...] = n0_ref[...]
        m_s[...] = jnp.full(m_s.shape, m0_ref[units.sample_index(u), h], F32)

    gates = gate_ref[...]
    lane = lax.broadcasted_iota(jnp.int32, gates.shape, 1)
    ig_col = jnp.sum(jnp.where(lane == h, gates, 0.0), axis=1, keepdims=True) + bi_ref[h]
    f_col = jnp.sum(jnp.where(lane == n_heads + h, gates, 0.0), axis=1, keepdims=True) + bf_ref[h]
    lf_col = _log_sigmoid(f_col)
    row = lax.broadcasted_iota(jnp.int32, (L, L), 0)
    col = lax.broadcasted_iota(jnp.int32, (L, L), 1)
    eye = row == col
    tril = col <= row
    ig_row = jnp.sum(jnp.where(eye, ig_col, 0.0), axis=0, keepdims=True)
    lf_row = jnp.sum(jnp.where(eye, lf_col, 0.0), axis=0, keepdims=True)
    b_col = jnp.sum(jnp.where(tril, lf_row, 0.0), axis=1, keepdims=True)
    b_row = jnp.sum(jnp.where(row <= col, lf_col, 0.0), axis=0, keepdims=True)

    d = jnp.where(tril, b_col - b_row + ig_row, NEG_INF)
    m_prev = m_s[:, :1]
    inter = b_col + m_prev
    m_t = jnp.maximum(inter, jnp.max(d, axis=1, keepdims=True))
    w_intra = jnp.exp(d - m_t)
    w_inter = jnp.exp(inter - m_t)

    q = q_ref[...]
    k = k_ref[...] * (dqk ** -0.5)
    v = v_ref[...]
    qb = q.astype(BF16)
    vb = v.astype(BF16)
    c_prev = c_s[...]
    n_prev = n_s[...]
    a = lax.dot_general(qb, k.astype(BF16), NT, preferred_element_type=F32) * w_intra
    num = (jnp.dot(a.astype(BF16), vb, preferred_element_type=F32)
           + jnp.dot(qb, c_prev.astype(BF16), preferred_element_type=F32) * w_inter)
    den = jnp.sum(a, axis=1, keepdims=True) + jnp.sum(q * n_prev, axis=1, keepdims=True) * w_inter
    den = jnp.maximum(jnp.abs(den), jnp.exp(-m_t))
    hval = num / den

    y = hval * lax.rsqrt(jnp.mean(hval * hval, axis=1, keepdims=True) + EPS)
    hm_ref[...] = (y * gh_ref[...] * _sigmoid(og_ref[...])).astype(hm_ref.dtype)

    m_new = m_t[L - 1:L, :]
    b_last = b_col[L - 1:L, :]
    w_last = jnp.exp(b_last - b_col + ig_col - m_new)
    decay = jnp.exp(b_last + m_prev - m_new)
    k_w = k * w_last
    c_new = decay * c_prev + lax.dot_general(k_w.astype(BF16), vb, TN, preferred_element_type=F32)
    n_new = decay * n_prev + jnp.sum(k_w, axis=0, keepdims=True)
    c_s[...] = c_new
    n_s[...] = n_new
    m_s[...] = jnp.broadcast_to(m_new, m_s.shape)

    @pl.when(units.last(u))
    def _():
        cout_ref[...] = c_new
        nout_ref[...] = n_new
        mout_ref[...] = jnp.broadcast_to(m_new, mout_ref.shape)


def mlstm_mixer(z, gates, b_i, b_f, g_head, c0, n0, m0, units, n_heads, dqk, dv):
    T = z.shape[0]
    H = n_heads
    ns = units.n_streams
    v_blk0 = (2 * H * dqk) // dv
    assert v_blk0 * dv == 2 * H * dqk
    sidx = units.sample_index
    smem = pl.BlockSpec(memory_space=pltpu.SMEM)
    kern = functools.partial(_mlstm_kernel, units=units, n_heads=H)
    nbytes = 2 * (2 * CHUNK * dqk * 4 + 2 * CHUNK * dv * 4 + CHUNK * LANES * 4 + CHUNK * dv * 2
                  + 2 * dqk * dv * 4) + dqk * dv * 4 * 4
    hm, c_out, n_out, m_out = pl.pallas_call(
        kern,
        grid=(H, units.total),
        in_specs=[smem, smem, smem,
                  pl.BlockSpec((CHUNK, dqk), lambda h, u: (u, h)),
                  pl.BlockSpec((CHUNK, dqk), lambda h, u: (u, H + h)),
                  pl.BlockSpec((CHUNK, dv), lambda h, u: (u, v_blk0 + h)),
                  pl.BlockSpec((CHUNK, dv), lambda h, u: (u, v_blk0 + H + h)),
                  pl.BlockSpec((CHUNK, LANES), lambda h, u: (u, 0)),
                  pl.BlockSpec((1, dv), lambda h, u: (0, h)),
                  pl.BlockSpec((None, None, dqk, dv), lambda h, u: (sidx(u), h, 0, 0)),
                  pl.BlockSpec((None, None, 1, dqk), lambda h, u: (sidx(u), h, 0, 0))],
        out_specs=[pl.BlockSpec((CHUNK, dv), lambda h, u: (u, h)),
                   pl.BlockSpec((None, None, dqk, dv), lambda h, u: (units.stream(u), h, 0, 0)),
                   pl.BlockSpec((None, None, 1, dqk), lambda h, u: (units.stream(u), h, 0, 0)),
                   pl.BlockSpec((None, None, 1, LANES), lambda h, u: (units.stream(u), h, 0, 0))],
        out_shape=[jax.ShapeDtypeStruct((T, H * dv), BF16),
                   jax.ShapeDtypeStruct((ns, H, dqk, dv), F32),
                   jax.ShapeDtypeStruct((ns, H, 1, dqk), F32),
                   jax.ShapeDtypeStruct((ns, H, 1, LANES), F32)],
        scratch_shapes=[pltpu.VMEM((dqk, dv), F32), pltpu.VMEM((1, dqk), F32), pltpu.VMEM((1, LANES), F32)],
        compiler_params=_params(("parallel", "arbitrary"), nbytes),
        name="mlstm",
    )(b_i, b_f, m0, z, z, z, z, gates, g_head.reshape(1, H * dv), c0, n0.reshape(n0.shape[0], H, 1, dqk))
    return hm, c_out, n_out[:, :, 0, :], m_out[:, :, 0, 0]


def _swa_kernel(sink_ref, q_ref, k0_ref, k1_ref, k2_ref, v0_ref, v1_ref, v2_ref, ck1_ref, ck2_ref, cv1_ref, cv2_ref,
                o_ref, *, units, n_kv):
    u = pl.program_id(0)
    is_sample = units.is_sample(u)
    L = CHUNK
    W = 2 * L
    NK = W + L
    hd = SWA_HEAD_DIM
    pairs = SWA_GROUP // 2
    R = pairs * L

    def pick(c_ref, z_ref):
        return jnp.where(is_sample, c_ref[...], z_ref[...])

    k_all = jnp.concatenate([pick(ck2_ref, k2_ref), pick(ck1_ref, k1_ref), k0_ref[...]], axis=0)
    v_all = jnp.concatenate([pick(cv2_ref, v2_ref), pick(cv1_ref, v1_ref), v0_ref[...]], axis=0)

    n = u % units.ncp
    j_min = jnp.where(is_sample, 0, jnp.maximum(W - L * n, 0))
    jj = lax.broadcasted_iota(jnp.int32, (R, NK), 1)
    tt = lax.broadcasted_iota(jnp.int32, (R, NK), 0) % L
    dist = jnp.abs(tt + W - jj).astype(F32)
    valid = jj >= j_min
    pair_of_row = lax.broadcasted_iota(jnp.int32, (R, 1), 0) // L
    lane = lax.broadcasted_iota(jnp.int32, (NK, LANES), 1)
    lo_lanes = lane < hd
    n_heads = n_kv * SWA_GROUP

    for g in range(n_kv):
        tile = (g * hd) // LANES
        kt = k_all[:, tile * LANES:(tile + 1) * LANES]
        vt = v_all[:, tile * LANES:(tile + 1) * LANES]
        kr = pltpu.roll(kt, hd, axis=1)
        vr = pltpu.roll(vt, hd, axis=1)
        if (g * hd) % LANES == 0:
            k_lo, k_hi, v_lo, v_hi = kt, kr, vt, vr
        else:
            k_lo, k_hi, v_lo, v_hi = kr, kt, vr, vt
        k_lo = jnp.where(lo_lanes, k_lo, 0.0).astype(BF16)
        k_hi = jnp.where(lo_lanes, 0.0, k_hi).astype(BF16)
        v_lo = jnp.where(lo_lanes, v_lo, 0.0).astype(BF16)
        v_hi = jnp.where(lo_lanes, 0.0, v_hi).astype(BF16)
        qg = jnp.concatenate(
            [q_ref[:, (g * pairs + p) * LANES:(g * pairs + p + 1) * LANES] for p in range(pairs)], axis=0)
        qg = (qg * (hd ** -0.5)).astype(BF16)

        def probs(k_half, odd):
            s = lax.dot_general(qg, k_half, NT, preferred_element_type=F32)
            head = g * SWA_GROUP + 2 * pair_of_row + odd
            slope = jnp.exp2(-8.0 * (head + 1).astype(F32) / n_heads)
            sink = jnp.zeros((R, 1), F32)
            for p in range(pairs):
                sink = jnp.where(pair_of_row == p, sink_ref[g * SWA_GROUP + 2 * p + odd], sink)
            s = jnp.where(valid, s - slope * dist, NEG_INF)
            mx = jnp.maximum(jnp.max(s, axis=1, keepdims=True), sink)
            e = jnp.exp(s - mx)
            return e / (jnp.sum(e, axis=1, keepdims=True) + jnp.exp(sink - mx))

        o = (jnp.dot(probs(k_lo, 0).astype(BF16), v_lo, preferred_element_type=F32)
             + jnp.dot(probs(k_hi, 1).astype(BF16), v_hi, preferred_element_type=F32))
        for p in range(pairs):
            o_ref[:, (g * pairs + p) * LANES:(g * pairs + p + 1) * LANES] = o[p * L:(p + 1) * L].astype(o_ref.dtype)


def swa_mixer(z, q_off, sinks, cache_k, cache_v, units, n_kv):
    T = z.shape[0]
    qw = n_kv * SWA_GROUP * SWA_HEAD_DIM
    kw = n_kv * SWA_HEAD_DIM
    assert kw % LANES == 0 and q_off % qw == 0 and (q_off + qw) % kw == 0
    qb = q_off // qw
    kb = (q_off + qw) // kw
    vb = kb + 1
    sidx = units.sample_index

    def zspec(blk, back):
        return pl.BlockSpec((CHUNK, kw), lambda u: (jnp.maximum(u - back, 0), blk))

    def cspec(part):
        return pl.BlockSpec((None, CHUNK, kw), lambda u: (sidx(u), part, 0))

    nbytes = 2 * (CHUNK * qw * 4 + 10 * CHUNK * kw * 4 + CHUNK * qw * 2) + 16 * 4 * CHUNK * 3 * CHUNK * 4
    return pl.pallas_call(
        functools.partial(_swa_kernel, units=units, n_kv=n_kv),
        grid=(units.total,),
        in_specs=[pl.BlockSpec(memory_space=pltpu.SMEM),
                  pl.BlockSpec((CHUNK, qw), lambda u: (u, qb)),
                  zspec(kb, 0), zspec(kb, 1), zspec(kb, 2),
                  zspec(vb, 0), zspec(vb, 1), zspec(vb, 2),
                  cspec(1), cspec(0), cspec(1), cspec(0)],
        out_specs=pl.BlockSpec((CHUNK, qw), lambda u: (u, 0)),
        out_shape=jax.ShapeDtypeStruct((T, qw), BF16),
        compiler_params=_params(("arbitrary",), nbytes),
        name="swa",
    )(sinks, z, z, z, z, z, z, z, cache_k, cache_k, cache_v, cache_v)


def _gla_kernel(q_ref, k_ref, v_ref, r_ref, glr_ref, wg_ref, bg_ref, gh_ref, s0_ref, o_ref, sout_ref, s_s, *, units):
    u = pl.program_id(1)
    is_sample = units.is_sample(u)
    L = CHUNK
    dk = q_ref.shape[1]

    @pl.when(jnp.logical_and(units.first(u), jnp.logical_not(is_sample)))
    def _():
        s_s[...] = jnp.zeros_like(s_s)

    @pl.when(is_sample)
    def _():
        s_s[...] = s0_ref[...]

    zg = jnp.dot(glr_ref[...].astype(BF16), wg_ref[...], preferred_element_type=F32) + bg_ref[...]
    lg = _log_sigmoid(zg) * (1.0 / GLA_TAU)
    row = lax.broadcasted_iota(jnp.int32, (L, L), 0)
    col = lax.broadcasted_iota(jnp.int32, (L, L), 1)
    tril = col <= row
    ones_tril = jnp.where(tril, 1.0, 0.0).astype(BF16)
    bc = sum(jnp.dot(ones_tril, part, preferred_element_type=F32) for part in _split3(lg))

    q = q_ref[...] * (dk ** -0.5)
    k = k_ref[...]
    vb = v_ref[...].astype(BF16)
    st = s_s[...]

    rr = lax.broadcasted_iota(jnp.int32, (L, dk), 0)
    n_sub = L // GLA_SUB
    refs = [bc[i * GLA_SUB:i * GLA_SUB + 1, :] for i in range(n_sub)]
    r_q = refs[n_sub - 1]
    for i in range(n_sub - 2, -1, -1):
        r_q = jnp.where(rr < (i + 1) * GLA_SUB, refs[i], r_q)
    q_t = (q * jnp.exp(bc - r_q)).astype(BF16)
    blocks = []
    for i in range(n_sub):
        e_i = jnp.where(rr < (i + 1) * GLA_SUB, refs[i] - bc, NEG_INF)
        k_i = (k * jnp.exp(e_i)).astype(BF16)
        blocks.append(lax.dot_general(q_t[i * GLA_SUB:(i + 1) * GLA_SUB], k_i, NT, preferred_element_type=F32))
    a = jnp.where(tril, jnp.concatenate(blocks, axis=0), 0.0)

    o = (jnp.dot(a.astype(BF16), vb, preferred_element_type=F32)
         + lax.dot_general((q * jnp.exp(bc)).astype(BF16), st.astype(BF16), NT, preferred_element_type=F32))

    b_last = bc[L - 1:L, :]
    k_dec = (k * jnp.exp(b_last - bc)).astype(BF16)
    st_new = jnp.exp(b_last) * st + lax.dot_general(vb, k_dec, TN, preferred_element_type=F32)
    s_s[...] = st_new

    y = o * lax.rsqrt(jnp.mean(o * o, axis=1, keepdims=True) + EPS)
    r = r_ref[...]
    o_ref[...] = (y * gh_ref[...] * (r * _sigmoid(r))).astype(o_ref.dtype)

    @pl.when(units.last(u))
    def _():
        sout_ref[...] = st_new


def gla_mixer(z, glr, w_g2, b_g, g_head, s0t, units, n_heads, dk, dv):
    T = z.shape[0]
    H = n_heads
    ns = units.n_streams
    v_blk0 = (2 * H * dk) // dv
    assert v_blk0 * dv == 2 * H * dk
    sidx = units.sample_index
    nbytes = 2 * (2 * CHUNK * dk * 4 + 2 * CHUNK * dv * 4 + CHUNK * LANES * 4 + LANES * dk * 2 + CHUNK * dv * 2
                  + 2 * dk * dv * 4) + dk * dv * 4 * 4
    out, st = pl.pallas_call(
        functools.partial(_gla_kernel, units=units),
        grid=(H, units.total),
        in_specs=[pl.BlockSpec((CHUNK, dk), lambda h, u: (u, h)),
                  pl.BlockSpec((CHUNK, dk), lambda h, u: (u, H + h)),
                  pl.BlockSpec((CHUNK, dv), lambda h, u: (u, v_blk0 + h)),
                  pl.BlockSpec((CHUNK, dv), lambda h, u: (u, v_blk0 + H + h)),
                  pl.BlockSpec((CHUNK, LANES), lambda h, u: (u, 0)),
                  pl.BlockSpec((LANES, dk), lambda h, u: (0, h)),
                  pl.BlockSpec((1, dk), lambda h, u: (0, h)),
                  pl.BlockSpec((1, dv), lambda h, u: (0, h)),
                  pl.BlockSpec((None, None, dv, dk), lambda h, u: (sidx(u), h, 0, 0))],
        out_specs=[pl.BlockSpec((CHUNK, dv), lambda h, u: (u, h)),
                   pl.BlockSpec((None, None, dv, dk), lambda h, u: (units.stream(u), h, 0, 0))],
        out_shape=[jax.ShapeDtypeStruct((T, H * dv), BF16),
                   jax.ShapeDtypeStruct((ns, H, dv, dk), F32)],
        scratch_shapes=[pltpu.VMEM((dv, dk), F32)],
        compiler_params=_params(("parallel", "arbitrary"), nbytes),
        name="gla",
    )(z, z, z, z, glr, w_g2, b_g.reshape(1, H * dk), g_head.reshape(1, H * dv), s0t)
    return out, st


def _pad_cols(w, width):
    return jnp.pad(w, ((0, 0), (0, width - w.shape[1])))


def kernel(x_prompt, x_sample, cache_swa_k, cache_swa_v, state_mlstm_C, state_mlstm_n, state_mlstm_m, state_gla_S,
           norm_mix, norm_ffn, norm_final, w_even_in, b_mlstm_i, b_mlstm_f, mlstm_head_norm, swa_sinks, w_even_out,
           w_odd_in, w_gla_gate, b_gla_gate, gla_head_norm, w_odd_out, w_ffn_in, w_ffn_out):
    B, S, D = x_prompt.shape
    DB, DS, _ = x_sample.shape
    depth = norm_mix.shape[0]
    assert DS == CHUNK and S % CHUNK == 0
    MH, dqk, dv = state_mlstm_C.shape[2:]
    n_kv = cache_swa_k.shape[3]
    w_buf = cache_swa_k.shape[2]
    assert w_buf == 2 * CHUNK and cache_swa_k.shape[4] == SWA_HEAD_DIM
    GH, gdk, gdv = state_gla_S.shape[2:]
    rank = w_gla_gate.shape[1]
    m_w = 2 * MH * dqk + 2 * MH * dv
    s_w = n_kv * (SWA_GROUP + 2) * SWA_HEAD_DIM
    g_w = 2 * GH * gdk + 2 * GH * gdv
    kvw = n_kv * SWA_HEAD_DIM
    units = Units(B, S // CHUNK, DB)
    TP = B * S

    x = jnp.concatenate([x_prompt.reshape(TP, D), x_sample.reshape(DB * DS, D)], axis=0)

    p_k, p_v, p_C, p_n, p_m, p_S = [], [], [], [], [], []
    s_k, s_v, s_C, s_n, s_m, s_S = [], [], [], [], [], []
    for l in range(depth):
        e = l // 2
        h = rmsnorm(x, norm_mix[l], BF16)
        if l % 2 == 0:
            w = w_even_in[e]
            w_main = jnp.concatenate([w[:, :m_w], w[:, m_w + 2 * MH:]], axis=1).astype(BF16)
            w_gate = _pad_cols(w[:, m_w:m_w + 2 * MH], LANES).astype(BF16)
            z = matmul(h, w_main, name="even_in")
            gates = matmul(h, w_gate, name="even_gates")
            hm, c_out, n_out, m_out = mlstm_mixer(
                z, gates, b_mlstm_i[e], b_mlstm_f[e], mlstm_head_norm[e],
                state_mlstm_C[e], state_mlstm_n[e], state_mlstm_m[e], units, MH, dqk, dv)
            ck = cache_swa_k[e].reshape(DB, w_buf, kvw)
            cv = cache_swa_v[e].reshape(DB, w_buf, kvw)
            hs = swa_mixer(z, m_w, swa_sinks[e], ck, cv, units, n_kv)
            mix = jnp.concatenate([hm, hs], axis=1)
            x = matmul(mix, w_even_out[e].astype(BF16), residual=x, name="even_out")
            k_off = m_w + n_kv * SWA_GROUP * SWA_HEAD_DIM
            zk = z[:, k_off:k_off + kvw]
            zv = z[:, k_off + kvw:k_off + 2 * kvw]
            kv_shape = (w_buf, n_kv, SWA_HEAD_DIM)
            p_k.append(zk[:TP].reshape(B, S, kvw)[:, -w_buf:].reshape((B,) + kv_shape))
            p_v.append(zv[:TP].reshape(B, S, kvw)[:, -w_buf:].reshape((B,) + kv_shape))
            s_k.append(jnp.concatenate([ck[:, CHUNK:], zk[TP:].reshape(DB, DS, kvw)], axis=1).reshape((DB,) + kv_shape))
            s_v.append(jnp.concatenate([cv[:, CHUNK:], zv[TP:].reshape(DB, DS, kvw)], axis=1).reshape((DB,) + kv_shape))
            p_C.append(c_out[:B]); p_n.append(n_out[:B]); p_m.append(m_out[:B])
            s_C.append(c_out[B:]); s_n.append(n_out[B:]); s_m.append(m_out[B:])
        else:
            w = w_odd_in[e]
            z = matmul(h, w[:, :g_w].astype(BF16), name="odd_in")
            glr = matmul(h, _pad_cols(w[:, g_w:], LANES).astype(BF16), name="odd_gate_rank")
            w_g2 = jnp.pad(w_gla_gate[e], ((0, LANES - rank), (0, 0))).astype(BF16)
            mix, st = gla_mixer(z, glr, w_g2, b_gla_gate[e], gla_head_norm[e],
                                jnp.swapaxes(state_gla_S[e], -1, -2), units, GH, gdk, gdv)
            x = matmul(mix, w_odd_out[e].astype(BF16), residual=x, name="odd_out")
            s_full = jnp.swapaxes(st, -1, -2)
            p_S.append(s_full[:B]); s_S.append(s_full[B:])
        hf = rmsnorm(x, norm_ffn[l], BF16)
        act = ffn_in(hf, w_ffn_in[l].astype(BF16))
        x = matmul_res_ksplit(act, w_ffn_out[l].astype(BF16), x)
    y = rmsnorm(x, norm_final, F32)
    return (y[:TP].reshape(B, S, D), y[TP:].reshape(DB, DS, D),
            jnp.stack(p_k), jnp.stack(p_v), jnp.stack(p_C), jnp.stack(p_n), jnp.stack(p_m), jnp.stack(p_S),
            jnp.stack(s_k), jnp.stack(s_v), jnp.stack(s_C), jnp.stack(s_n), jnp.stack(s_m), jnp.stack(s_S))
```

```python
import functools

import jax
import jax.numpy as jnp
from jax import lax
from jax.experimental import pallas as pl
from jax.experimental.pallas import tpu as pltpu

F32 = jnp.float32
BF16 = jnp.bfloat16

CHUNK = 64
SWA_HEAD_DIM = 64
SWA_GROUP = 8
LANES = 128
GLA_SUB = 16
GLA_HEADS_PER_STEP = 4
GLA_TAU = 16.0
EPS = 1e-6
NEG_INF = float("-inf")

V7X_VMEM_BYTES = 64 * 1024 * 1024
VMEM_CAP_BYTES = V7X_VMEM_BYTES - 6 * 1024 * 1024

NT = (((1,), (1,)), ((), ()))
TN = (((0,), (0,)), ((), ()))


def _pick(n, cands):
    for c in cands:
        if n % c == 0:
            return c
    raise ValueError(f"no tile in {cands} divides {n}")


def _params(semantics, block_bytes):
    limit = min(VMEM_CAP_BYTES, block_bytes + 16 * 1024 * 1024)
    return pltpu.CompilerParams(dimension_semantics=semantics, vmem_limit_bytes=int(limit))


def _log_sigmoid(x):
    return jnp.minimum(x, 0.0) - jnp.log(1.0 + jnp.exp(-jnp.abs(x)))


def _sigmoid(x):
    return 1.0 / (1.0 + jnp.exp(-x))


def _split3(x):
    hi = x.astype(BF16)
    r1 = x - hi.astype(F32)
    mid = r1.astype(BF16)
    lo = (r1 - mid.astype(F32)).astype(BF16)
    return hi, mid, lo


def _rmsnorm_kernel(x_ref, g_ref, o_ref):
    x = x_ref[...]
    y = x * lax.rsqrt(jnp.mean(x * x, axis=-1, keepdims=True) + EPS)
    o_ref[...] = (y * g_ref[...]).astype(o_ref.dtype)


def rmsnorm(x, g, out_dtype):
    T, D = x.shape
    tr = _pick(T, (272, 256, 160, 128, 64))
    nbytes = 2 * tr * D * (4 + jnp.dtype(out_dtype).itemsize)
    return pl.pallas_call(
        _rmsnorm_kernel,
        grid=(T // tr,),
        in_specs=[pl.BlockSpec((tr, D), lambda i: (i, 0)), pl.BlockSpec((1, D), lambda i: (0, 0))],
        out_specs=pl.BlockSpec((tr, D), lambda i: (i, 0)),
        out_shape=jax.ShapeDtypeStruct((T, D), out_dtype),
        compiler_params=_params(("parallel",), nbytes),
        name="rmsnorm",
    )(x, g.reshape(1, D))


def _m_tile(T):
    return _pick(T, (1088, 1024, 640, 512, 320, 256, 128, 64))


def _proj_kernel(*refs, n_x, has_res):
    x_refs = refs[:n_x]
    w_ref = refs[n_x]
    r_ref = refs[n_x + 1] if has_res else None
    o_ref = refs[n_x + 1 + has_res]
    wb_ref = refs[n_x + 2 + has_res]

    @pl.when(pl.program_id(1) == 0)
    def _():
        wb_ref[...] = w_ref[...].astype(BF16)

    k0 = 0
    acc = None
    for x_ref in x_refs:
        kx = x_ref.shape[1]
        part = jnp.dot(x_ref[...], wb_ref[k0:k0 + kx, :], preferred_element_type=F32)
        acc = part if acc is None else acc + part
        k0 += kx
    if has_res:
        acc = r_ref[...] + acc
    o_ref[...] = acc.astype(o_ref.dtype)


def project(xs, w, layer, n_cols, out_dtype=F32, residual=None, name="project"):
    T = xs[0].shape[0]
    K = w.shape[1]
    assert sum(x.shape[1] for x in xs) == K
    tm = _m_tile(T)
    tn = _pick(n_cols, (512, 256, 128))
    osz = jnp.dtype(out_dtype).itemsize
    nbytes = 2 * (tm * K * 2 + K * tn * 4 + tm * tn * osz) + K * tn * 2 + tm * tn * 4
    in_specs = [pl.BlockSpec((tm, x.shape[1]), lambda j, i: (i, 0)) for x in xs]
    in_specs.append(pl.BlockSpec((None, K, tn), lambda j, i: (layer, 0, j)))
    args = list(xs) + [w]
    if residual is not None:
        in_specs.append(pl.BlockSpec((tm, tn), lambda j, i: (i, j)))
        args.append(residual)
        nbytes += 2 * tm * tn * 4
    return pl.pallas_call(
        functools.partial(_proj_kernel, n_x=len(xs), has_res=residual is not None),
        grid=(n_cols // tn, T // tm),
        in_specs=in_specs,
        out_specs=pl.BlockSpec((tm, tn), lambda j, i: (i, j)),
        out_shape=jax.ShapeDtypeStruct((T, n_cols), out_dtype),
        scratch_shapes=[pltpu.VMEM((K, tn), BF16)],
        compiler_params=_params(("arbitrary", "arbitrary"), nbytes),
        name=name,
    )(*args)


def _ffn_in_kernel(x_ref, wg_ref, wu_ref, o_ref, wgb_ref, wub_ref):
    @pl.when(pl.program_id(1) == 0)
    def _():
        wgb_ref[...] = wg_ref[...].astype(BF16)
        wub_ref[...] = wu_ref[...].astype(BF16)

    x = x_ref[...]
    g = jnp.dot(x, wgb_ref[...], preferred_element_type=F32)
    u = jnp.dot(x, wub_ref[...], preferred_element_type=F32)
    o_ref[...] = (g * _sigmoid(g) * u).astype(o_ref.dtype)


def ffn_in(x, w_in, layer):
    T, K = x.shape
    F = w_in.shape[2] // 2
    tm = _m_tile(T)
    tn = _pick(F, (256, 128))
    nf = F // tn
    nbytes = 2 * (tm * K * 2 + 2 * K * tn * 4 + tm * tn * 2) + 2 * K * tn * 2 + 3 * tm * tn * 4
    return pl.pallas_call(
        _ffn_in_kernel,
        grid=(nf, T // tm),
        in_specs=[pl.BlockSpec((tm, K), lambda j, i: (i, 0)),
                  pl.BlockSpec((None, K, tn), lambda j, i: (layer, 0, j)),
                  pl.BlockSpec((None, K, tn), lambda j, i: (layer, 0, j + nf))],
        out_specs=pl.BlockSpec((tm, tn), lambda j, i: (i, j)),
        out_shape=jax.ShapeDtypeStruct((T, F), BF16),
        scratch_shapes=[pltpu.VMEM((K, tn), BF16), pltpu.VMEM((K, tn), BF16)],
        compiler_params=_params(("arbitrary", "arbitrary"), nbytes),
        name="ffn_in",
    )(x, w_in, w_in)


def _mm_res_ksplit_kernel(x_ref, w_ref, r_ref, o_ref, acc_ref):
    k = pl.program_id(2)

    @pl.when(k == 0)
    def _():
        acc_ref[...] = r_ref[...]

    acc_ref[...] += jnp.dot(x_ref[...], w_ref[...], preferred_element_type=F32)

    @pl.when(k == pl.num_programs(2) - 1)
    def _():
        o_ref[...] = acc_ref[...]


def matmul_res_ksplit(x, w, residual):
    T, K = x.shape
    N = w.shape[1]
    tm = _m_tile(T)
    tn = _pick(N, (512, 256, 128))
    nk = 2
    tk = K // nk
    assert tk * nk == K and tk % LANES == 0
    nbytes = 2 * (tm * tk * 2 + tk * tn * 2 + 2 * tm * tn * 4) + 2 * tm * tn * 4
    return pl.pallas_call(
        _mm_res_ksplit_kernel,
        grid=(T // tm, N // tn, nk),
        in_specs=[pl.BlockSpec((tm, tk), lambda i, j, k: (i, k)),
                  pl.BlockSpec((tk, tn), lambda i, j, k: (k, j)),
                  pl.BlockSpec((tm, tn), lambda i, j, k: (i, j))],
        out_specs=pl.BlockSpec((tm, tn), lambda i, j, k: (i, j)),
        out_shape=jax.ShapeDtypeStruct((T, N), F32),
        scratch_shapes=[pltpu.VMEM((tm, tn), F32)],
        compiler_params=_params(("parallel", "arbitrary", "arbitrary"), nbytes),
        name="ffn_out",
    )(x, w, residual)


class Units:
    def __init__(self, n_prompt, chunks_per_prompt, n_sample):
        self.ncp = chunks_per_prompt
        self.up = n_prompt * chunks_per_prompt
        self.n_prompt = n_prompt
        self.n_sample = n_sample
        self.total = self.up + n_sample
        self.n_streams = n_prompt + n_sample

    def is_sample(self, u):
        return u >= self.up

    def stream(self, u):
        return jnp.where(u < self.up, u // self.ncp, self.n_prompt + u - self.up)

    def sample_index(self, u):
        return jnp.maximum(u - self.up, 0)

    def first(self, u):
        return jnp.logical_or(u >= self.up, u % self.ncp == 0)

    def last(self, u):
        return jnp.logical_or(u >= self.up, u % self.ncp == self.ncp - 1)


def _mlstm_kernel(bi_ref, bf_ref, m0_ref, q_ref, k_ref, v_ref, og_ref, gate_ref, gh_ref, c0_ref, n0_ref,
                  hm_ref, cout_ref, nout_ref, mout_ref, c_s, n_s, m_s, *, units, n_heads):
    u = pl.program_id(0)
    is_sample = units.is_sample(u)
    L = CHUNK
    H = n_heads
    dqk = c_s.shape[1]
    dv = c_s.shape[2]

    @pl.when(jnp.logical_and(units.first(u), jnp.logical_not(is_sample)))
    def _():
        c_s[...] = jnp.zeros_like(c_s)
        n_s[...] = jnp.zeros_like(n_s)
        m_s[...] = jnp.zeros_like(m_s)

    @pl.when(is_sample)
    def _():
        c_s[...] = c0_ref[...]
        n_s[...] = n0_ref[...]
        for hh in range(H):
            m_s[hh] = jnp.full(m_s.shape[1:], m0_ref[units.sample_index(u), hh], F32)

    gates = gate_ref[...]
    lane = lax.broadcasted_iota(jnp.int32, gates.shape, 1)
    row = lax.broadcasted_iota(jnp.int32, (L, L), 0)
    col = lax.broadcasted_iota(jnp.int32, (L, L), 1)
    eye = row == col
    tril = col <= row
    triu = row <= col

    for hh in range(H):
        ig_col = jnp.sum(jnp.where(lane == hh, gates, 0.0), axis=1, keepdims=True) + bi_ref[hh]
        f_col = jnp.sum(jnp.where(lane == H + hh, gates, 0.0), axis=1, keepdims=True) + bf_ref[hh]
        lf_col = _log_sigmoid(f_col)
        ig_row = jnp.sum(jnp.where(eye, ig_col, 0.0), axis=0, keepdims=True)
        lf_row = jnp.sum(jnp.where(eye, lf_col, 0.0), axis=0, keepdims=True)
        b_col = jnp.sum(jnp.where(tril, lf_row, 0.0), axis=1, keepdims=True)
        b_row = jnp.sum(jnp.where(triu, lf_col, 0.0), axis=0, keepdims=True)

        d = jnp.where(tril, b_col - b_row + ig_row, NEG_INF)
        m_prev = m_s[hh][:, :1]
        inter = b_col + m_prev
        m_t = jnp.maximum(inter, jnp.max(d, axis=1, keepdims=True))
        w_intra = jnp.exp(d - m_t)
        w_inter = jnp.exp(inter - m_t)

        q = q_ref[:, hh * dqk:(hh + 1) * dqk]
        k = k_ref[:, hh * dqk:(hh + 1) * dqk] * (dqk ** -0.5)
        vb = v_ref[:, hh * dv:(hh + 1) * dv].astype(BF16)
        qb = q.astype(BF16)
        c_prev = c_s[hh]
        n_prev = n_s[hh]
        a = lax.dot_general(qb, k.astype(BF16), NT, preferred_element_type=F32) * w_intra
        num = (jnp.dot(a.astype(BF16), vb, preferred_element_type=F32)
               + jnp.dot(qb, c_prev.astype(BF16), preferred_element_type=F32) * w_inter)
        den = jnp.sum(a, axis=1, keepdims=True) + jnp.sum(q * n_prev, axis=1, keepdims=True) * w_inter
        den = jnp.maximum(jnp.abs(den), jnp.exp(-m_t))
        hval = num / den

        y = hval * lax.rsqrt(jnp.mean(hval * hval, axis=1, keepdims=True) + EPS)
        gate_out = _sigmoid(og_ref[:, hh * dv:(hh + 1) * dv])
        hm_ref[:, hh * dv:(hh + 1) * dv] = (y * gh_ref[:, hh * dv:(hh + 1) * dv] * gate_out).astype(hm_ref.dtype)

        m_new = m_t[L - 1:L, :]
        b_last = b_col[L - 1:L, :]
        w_last = jnp.exp(b_last - b_col + ig_col - m_new)
        decay = jnp.exp(b_last + m_prev - m_new)
        k_w = k * w_last
        c_s[hh] = decay * c_prev + lax.dot_general(k_w.astype(BF16), vb, TN, preferred_element_type=F32)
        n_s[hh] = decay * n_prev + jnp.sum(k_w, axis=0, keepdims=True)
        m_s[hh] = jnp.broadcast_to(m_new, m_s.shape[1:])

    @pl.when(units.last(u))
    def _():
        cout_ref[...] = c_s[...]
        nout_ref[...] = n_s[...]
        mout_ref[...] = m_s[...]


def mlstm_mixer(z, gates, b_i, b_f, g_head, c0, n0, m0, units, n_heads, dqk, dv):
    T = z.shape[0]
    H = n_heads
    ns = units.n_streams
    qw = H * dqk
    vw = H * dv
    assert (2 * qw) % vw == 0
    v_blk = (2 * qw) // vw
    sidx = units.sample_index
    smem = pl.BlockSpec(memory_space=pltpu.SMEM)
    nbytes = (2 * (2 * CHUNK * qw * 4 + 2 * CHUNK * vw * 4 + CHUNK * LANES * 4 + CHUNK * vw * 2 + 2 * qw * dv * 4)
              + qw * dv * 4 * 4)
    hm, c_out, n_out, m_out = pl.pallas_call(
        functools.partial(_mlstm_kernel, units=units, n_heads=H),
        grid=(units.total,),
        in_specs=[smem, smem, smem,
                  pl.BlockSpec((CHUNK, qw), lambda u: (u, 0)),
                  pl.BlockSpec((CHUNK, qw), lambda u: (u, 1)),
                  pl.BlockSpec((CHUNK, vw), lambda u: (u, v_blk)),
                  pl.BlockSpec((CHUNK, vw), lambda u: (u, v_blk + 1)),
                  pl.BlockSpec((CHUNK, LANES), lambda u: (u, 0)),
                  pl.BlockSpec((1, vw), lambda u: (0, 0)),
                  pl.BlockSpec((None, H, dqk, dv), lambda u: (sidx(u), 0, 0, 0)),
                  pl.BlockSpec((None, H, 1, dqk), lambda u: (sidx(u), 0, 0, 0))],
        out_specs=[pl.BlockSpec((CHUNK, vw), lambda u: (u, 0)),
                   pl.BlockSpec((None, H, dqk, dv), lambda u: (units.stream(u), 0, 0, 0)),
                   pl.BlockSpec((None, H, 1, dqk), lambda u: (units.stream(u), 0, 0, 0)),
                   pl.BlockSpec((None, H, 1, LANES), lambda u: (units.stream(u), 0, 0, 0))],
        out_shape=[jax.ShapeDtypeStruct((T, vw), BF16),
                   jax.ShapeDtypeStruct((ns, H, dqk, dv), F32),
                   jax.ShapeDtypeStruct((ns, H, 1, dqk), F32),
                   jax.ShapeDtypeStruct((ns, H, 1, LANES), F32)],
        scratch_shapes=[pltpu.VMEM((H, dqk, dv), F32), pltpu.VMEM((H, 1, dqk), F32), pltpu.VMEM((H, 1, LANES), F32)],
        compiler_params=_params(("arbitrary",), nbytes),
        name="mlstm",
    )(b_i, b_f, m0, z, z, z, z, gates, g_head.reshape(1, vw), c0, n0.reshape(n0.shape[0], H, 1, dqk))
    return hm, c_out, n_out[:, :, 0, :], m_out[:, :, 0, 0]


def _swa_kernel(sink_ref, q_ref, k0_ref, k1_ref, k2_ref, v0_ref, v1_ref, v2_ref, ck1_ref, ck2_ref, cv1_ref, cv2_ref,
                o_ref, *, units, n_kv):
    u = pl.program_id(0)
    is_sample = units.is_sample(u)
    L = CHUNK
    W = 2 * L
    NK = W + L
    hd = SWA_HEAD_DIM
    pairs = SWA_GROUP // 2
    R = pairs * L

    def pick(c_ref, z_ref):
        return jnp.where(is_sample, c_ref[...], z_ref[...])

    k_all = jnp.concatenate([pick(ck2_ref, k2_ref), pick(ck1_ref, k1_ref), k0_ref[...]], axis=0)
    v_all = jnp.concatenate([pick(cv2_ref, v2_ref), pick(cv1_ref, v1_ref), v0_ref[...]], axis=0)

    n = u % units.ncp
    j_min = jnp.where(is_sample, 0, jnp.maximum(W - L * n, 0))
    jj = lax.broadcasted_iota(jnp.int32, (R, NK), 1)
    tt = lax.broadcasted_iota(jnp.int32, (R, NK), 0) % L
    dist = jnp.abs(tt + W - jj).astype(F32)
    valid = jj >= j_min
    pair_of_row = lax.broadcasted_iota(jnp.int32, (R, 1), 0) // L
    lane = lax.broadcasted_iota(jnp.int32, (NK, LANES), 1)
    lo_lanes = lane < hd
    n_heads = n_kv * SWA_GROUP

    for g in range(n_kv):
        tile = (g * hd) // LANES
        kt = k_all[:, tile * LANES:(tile + 1) * LANES]
        vt = v_all[:, tile * LANES:(tile + 1) * LANES]
        kr = pltpu.roll(kt, hd, axis=1)
        vr = pltpu.roll(vt, hd, axis=1)
        if (g * hd) % LANES == 0:
            k_lo, k_hi, v_lo, v_hi = kt, kr, vt, vr
        else:
            k_lo, k_hi, v_lo, v_hi = kr, kt, vr, vt
        k_lo = jnp.where(lo_lanes, k_lo, 0.0).astype(BF16)
        k_hi = jnp.where(lo_lanes, 0.0, k_hi).astype(BF16)
        v_lo = jnp.where(lo_lanes, v_lo, 0.0).astype(BF16)
        v_hi = jnp.where(lo_lanes, 0.0, v_hi).astype(BF16)
        qg = jnp.concatenate(
            [q_ref[:, (g * pairs + p) * LANES:(g * pairs + p + 1) * LANES] for p in range(pairs)], axis=0)
        qg = (qg * (hd ** -0.5)).astype(BF16)

        def probs(k_half, odd):
            s = lax.dot_general(qg, k_half, NT, preferred_element_type=F32)
            head = g * SWA_GROUP + 2 * pair_of_row + odd
            slope = jnp.exp2(-8.0 * (head + 1).astype(F32) / n_heads)
            sink = jnp.zeros((R, 1), F32)
            for p in range(pairs):
                sink = jnp.where(pair_of_row == p, sink_ref[g * SWA_GROUP + 2 * p + odd], sink)
            s = jnp.where(valid, s - slope * dist, NEG_INF)
            mx = jnp.maximum(jnp.max(s, axis=1, keepdims=True), sink)
            e = jnp.exp(s - mx)
            return e / (jnp.sum(e, axis=1, keepdims=True) + jnp.exp(sink - mx))

        o = (jnp.dot(probs(k_lo, 0).astype(BF16), v_lo, preferred_element_type=F32)
             + jnp.dot(probs(k_hi, 1).astype(BF16), v_hi, preferred_element_type=F32))
        for p in range(pairs):
            o_ref[:, (g * pairs + p) * LANES:(g * pairs + p + 1) * LANES] = o[p * L:(p + 1) * L].astype(o_ref.dtype)


def swa_mixer(z, sinks, cache_k, cache_v, units, n_kv):
    T = z.shape[0]
    qw = n_kv * SWA_GROUP * SWA_HEAD_DIM
    kw = n_kv * SWA_HEAD_DIM
    assert kw % LANES == 0 and qw % kw == 0
    kb = qw // kw
    vb = kb + 1
    sidx = units.sample_index

    def zspec(blk, back):
        return pl.BlockSpec((CHUNK, kw), lambda u: (jnp.maximum(u - back, 0), blk))

    def cspec(part):
        return pl.BlockSpec((None, CHUNK, kw), lambda u: (sidx(u), part, 0))

    nbytes = 2 * (CHUNK * qw * 4 + 10 * CHUNK * kw * 4 + CHUNK * qw * 2) + 16 * 4 * CHUNK * 3 * CHUNK * 4
    return pl.pallas_call(
        functools.partial(_swa_kernel, units=units, n_kv=n_kv),
        grid=(units.total,),
        in_specs=[pl.BlockSpec(memory_space=pltpu.SMEM),
                  pl.BlockSpec((CHUNK, qw), lambda u: (u, 0)),
                  zspec(kb, 0), zspec(kb, 1), zspec(kb, 2),
                  zspec(vb, 0), zspec(vb, 1), zspec(vb, 2),
                  cspec(1), cspec(0), cspec(1), cspec(0)],
        out_specs=pl.BlockSpec((CHUNK, qw), lambda u: (u, 0)),
        out_shape=jax.ShapeDtypeStruct((T, qw), BF16),
        compiler_params=_params(("arbitrary",), nbytes),
        name="swa",
    )(sinks, z, z, z, z, z, z, z, cache_k, cache_k, cache_v, cache_v)


def _gla_kernel(q_ref, k_ref, v_ref, r_ref, glr_ref, wg_ref, bg_ref, gh_ref, s0_ref, o_ref, sout_ref, s_s, *, units):
    u = pl.program_id(1)
    is_sample = units.is_sample(u)
    L = CHUNK
    hb, dv, dk = s_s.shape

    @pl.when(jnp.logical_and(units.first(u), jnp.logical_not(is_sample)))
    def _():
        s_s[...] = jnp.zeros_like(s_s)

    @pl.when(is_sample)
    def _():
        s_s[...] = s0_ref[...]

    glr = glr_ref[...].astype(BF16)
    row = lax.broadcasted_iota(jnp.int32, (L, L), 0)
    col = lax.broadcasted_iota(jnp.int32, (L, L), 1)
    tril = col <= row
    ones_tril = jnp.where(tril, 1.0, 0.0).astype(BF16)
    rr = lax.broadcasted_iota(jnp.int32, (L, dk), 0)
    n_sub = L // GLA_SUB

    for hh in range(hb):
        ks = slice(hh * dk, (hh + 1) * dk)
        vs = slice(hh * dv, (hh + 1) * dv)
        zg = jnp.dot(glr, wg_ref[:, ks], preferred_element_type=F32) + bg_ref[:, ks]
        lg = _log_sigmoid(zg) * (1.0 / GLA_TAU)
        bc = sum(jnp.dot(ones_tril, part, preferred_element_type=F32) for part in _split3(lg))

        q = q_ref[:, ks] * (dk ** -0.5)
        k = k_ref[:, ks]
        vb = v_ref[:, vs].astype(BF16)
        st = s_s[hh]

        refs = [bc[i * GLA_SUB:i * GLA_SUB + 1, :] for i in range(n_sub)]
        r_q = refs[n_sub - 1]
        for i in range(n_sub - 2, -1, -1):
            r_q = jnp.where(rr < (i + 1) * GLA_SUB, refs[i], r_q)
        q_t = (q * jnp.exp(bc - r_q)).astype(BF16)
        blocks = []
        for i in range(n_sub):
            e_i = jnp.where(rr < (i + 1) * GLA_SUB, refs[i] - bc, NEG_INF)
            k_i = (k * jnp.exp(e_i)).astype(BF16)
            blocks.append(lax.dot_general(q_t[i * GLA_SUB:(i + 1) * GLA_SUB], k_i, NT, preferred_element_type=F32))
        a = jnp.where(tril, jnp.concatenate(blocks, axis=0), 0.0)

        o = (jnp.dot(a.astype(BF16), vb, preferred_element_type=F32)
             + lax.dot_general((q * jnp.exp(bc)).astype(BF16), st.astype(BF16), NT, preferred_element_type=F32))

        b_last = bc[L - 1:L, :]
        k_dec = (k * jnp.exp(b_last - bc)).astype(BF16)
        s_s[hh] = jnp.exp(b_last) * st + lax.dot_general(vb, k_dec, TN, preferred_element_type=F32)

        y = o * lax.rsqrt(jnp.mean(o * o, axis=1, keepdims=True) + EPS)
        r = r_ref[:, vs]
        o_ref[:, vs] = (y * gh_ref[:, vs] * (r * _sigmoid(r))).astype(o_ref.dtype)

    @pl.when(units.last(u))
    def _():
        sout_ref[...] = s_s[...]


def gla_mixer(z, glr, w_g2, b_g, g_head, s0t, units, n_heads, dk, dv):
    T = z.shape[0]
    H = n_heads
    hb = GLA_HEADS_PER_STEP if H % GLA_HEADS_PER_STEP == 0 else 1
    ng = H // hb
    ns = units.n_streams
    kw = hb * dk
    vw = hb * dv
    assert (2 * H * dk) % vw == 0
    v_blk0 = (2 * H * dk) // vw
    sidx = units.sample_index
    nbytes = (2 * (2 * CHUNK * kw * 4 + 2 * CHUNK * vw * 4 + CHUNK * LANES * 4 + LANES * kw * 2 + CHUNK * vw * 2
                   + 2 * kw * dv * 4) + kw * dv * 4 * 4)
    out, st = pl.pallas_call(
        functools.partial(_gla_kernel, units=units),
        grid=(ng, units.total),
        in_specs=[pl.BlockSpec((CHUNK, kw), lambda g, u: (u, g)),
                  pl.BlockSpec((CHUNK, kw), lambda g, u: (u, ng + g)),
                  pl.BlockSpec((CHUNK, vw), lambda g, u: (u, v_blk0 + g)),
                  pl.BlockSpec((CHUNK, vw), lambda g, u: (u, v_blk0 + ng + g)),
                  pl.BlockSpec((CHUNK, LANES), lambda g, u: (u, 0)),
                  pl.BlockSpec((LANES, kw), lambda g, u: (0, g)),
                  pl.BlockSpec((1, kw), lambda g, u: (0, g)),
                  pl.BlockSpec((1, vw), lambda g, u: (0, g)),
                  pl.BlockSpec((None, hb, dv, dk), lambda g, u: (sidx(u), g, 0, 0))],
        out_specs=[pl.BlockSpec((CHUNK, vw), lambda g, u: (u, g)),
                   pl.BlockSpec((None, hb, dv, dk), lambda g, u: (units.stream(u), g, 0, 0))],
        out_shape=[jax.ShapeDtypeStruct((T, H * dv), BF16),
                   jax.ShapeDtypeStruct((ns, H, dv, dk), F32)],
        scratch_shapes=[pltpu.VMEM((hb, dv, dk), F32)],
        compiler_params=_params(("parallel", "arbitrary"), nbytes),
        name="gla",
    )(z, z, z, z, glr, w_g2, b_g.reshape(1, H * dk), g_head.reshape(1, H * dv), s0t)
    return out, st


def _pad_cols(w, width):
    return jnp.pad(w, ((0, 0), (0, width - w.shape[1])))


def kernel(x_prompt, x_sample, cache_swa_k, cache_swa_v, state_mlstm_C, state_mlstm_n, state_mlstm_m, state_gla_S,
           norm_mix, norm_ffn, norm_final, w_even_in, b_mlstm_i, b_mlstm_f, mlstm_head_norm, swa_sinks, w_even_out,
           w_odd_in, w_gla_gate, b_gla_gate, gla_head_norm, w_odd_out, w_ffn_in, w_ffn_out):
    B, S, D = x_prompt.shape
    DB, DS, _ = x_sample.shape
    depth = norm_mix.shape[0]
    assert DS == CHUNK and S % CHUNK == 0
    MH, dqk, dv = state_mlstm_C.shape[2:]
    n_kv = cache_swa_k.shape[3]
    w_buf = cache_swa_k.shape[2]
    assert w_buf == 2 * CHUNK and cache_swa_k.shape[4] == SWA_HEAD_DIM
    GH, gdk, gdv = state_gla_S.shape[2:]
    rank = w_gla_gate.shape[1]
    m_w = 2 * MH * dqk + 2 * MH * dv
    sq_w = n_kv * SWA_GROUP * SWA_HEAD_DIM
    kvw = n_kv * SWA_HEAD_DIM
    s_w = sq_w + 2 * kvw
    g_w = 2 * GH * gdk + 2 * GH * gdv
    units = Units(B, S // CHUNK, DB)
    TP = B * S

    x = jnp.concatenate([x_prompt.reshape(TP, D), x_sample.reshape(DB * DS, D)], axis=0)

    p_k, p_v, p_C, p_n, p_m, p_S = [], [], [], [], [], []
    s_k, s_v, s_C, s_n, s_m, s_S = [], [], [], [], [], []
    for l in range(depth):
        e = l // 2
        h = rmsnorm(x, norm_mix[l], BF16)
        if l % 2 == 0:
            w_gate = _pad_cols(w_even_in[e, :, m_w:m_w + 2 * MH], LANES)[None]
            w_attn = w_even_in[e, :, m_w + 2 * MH:][None]
            zm = project([h], w_even_in, e, m_w, name="even_in_mlstm")
            zs = project([h], w_attn, 0, s_w, name="even_in_attn")
            gates = project([h], w_gate, 0, LANES, name="even_gates")
            hm, c_out, n_out, m_out = mlstm_mixer(
                zm, gates, b_mlstm_i[e], b_mlstm_f[e], mlstm_head_norm[e],
                state_mlstm_C[e], state_mlstm_n[e], state_mlstm_m[e], units, MH, dqk, dv)
            ck = cache_swa_k[e].reshape(DB, w_buf, kvw)
            cv = cache_swa_v[e].reshape(DB, w_buf, kvw)
            hs = swa_mixer(zs, swa_sinks[e], ck, cv, units, n_kv)
            x = project([hm, hs], w_even_out, e, D, residual=x, name="even_out")
            zk = zs[:, sq_w:sq_w + kvw]
            zv = zs[:, sq_w + kvw:]
            kv_shape = (w_buf, n_kv, SWA_HEAD_DIM)
            p_k.append(zk[:TP].reshape(B, S, kvw)[:, -w_buf:].reshape((B,) + kv_shape))
            p_v.append(zv[:TP].reshape(B, S, kvw)[:, -w_buf:].reshape((B,) + kv_shape))
            s_k.append(jnp.concatenate([ck[:, CHUNK:], zk[TP:].reshape(DB, DS, kvw)], axis=1).reshape((DB,) + kv_shape))
            s_v.append(jnp.concatenate([cv[:, CHUNK:], zv[TP:].reshape(DB, DS, kvw)], axis=1).reshape((DB,) + kv_shape))
            p_C.append(c_out[:B]); p_n.append(n_out[:B]); p_m.append(m_out[:B])
            s_C.append(c_out[B:]); s_n.append(n_out[B:]); s_m.append(m_out[B:])
        else:
            z = project([h], w_odd_in, e, g_w, name="odd_in")
            w_rank = _pad_cols(w_odd_in[e, :, g_w:], LANES)[None]
            glr = project([h], w_rank, 0, LANES, name="odd_gate_rank")
            w_g2 = jnp.pad(w_gla_gate[e], ((0, LANES - rank), (0, 0))).astype(BF16)
            mix, st = gla_mixer(z, glr, w_g2, b_gla_gate[e], gla_head_norm[e],
                                jnp.swapaxes(state_gla_S[e], -1, -2), units, GH, gdk, gdv)
            x = project([mix], w_odd_out, e, D, residual=x, name="odd_out")
            s_full = jnp.swapaxes(st, -1, -2)
            p_S.append(s_full[:B]); s_S.append(s_full[B:])
        hf = rmsnorm(x, norm_ffn[l], BF16)
        act = ffn_in(hf, w_ffn_in, l)
        x = matmul_res_ksplit(act, w_ffn_out[l].astype(BF16), x)
    y = rmsnorm(x, norm_final, F32)
    return (y[:TP].reshape(B, S, D), y[TP:].reshape(DB, DS, D),
            jnp.stack(p_k), jnp.stack(p_v), jnp.stack(p_C), jnp.stack(p_n), jnp.stack(p_m), jnp.stack(p_S),
            jnp.stack(s_k), jnp.stack(s_v), jnp.stack(s_C), jnp.stack(s_n), jnp.stack(s_m), jnp.stack(s_S))
```

```python
import functools

import jax
import jax.numpy as jnp
from jax import lax
from jax.experimental import pallas as pl
from jax.experimental.pallas import tpu as pltpu

F32 = jnp.float32
BF16 = jnp.bfloat16

CHUNK = 64
SWA_HEAD_DIM = 64
SWA_GROUP = 8
LANES = 128
GLA_SUB = 16
GLA_HEADS_PER_STEP = 8
GLA_TAU = 16.0
EPS = 1e-6
NEG_INF = float("-inf")

V7X_VMEM_BYTES = 64 * 1024 * 1024
VMEM_CAP_BYTES = V7X_VMEM_BYTES - 6 * 1024 * 1024

NT = (((1,), (1,)), ((), ()))
TN = (((0,), (0,)), ((), ()))


def _pick(n, cands):
    for c in cands:
        if n % c == 0:
            return c
    raise ValueError(f"no tile in {cands} divides {n}")


def _params(semantics, block_bytes):
    limit = min(VMEM_CAP_BYTES, block_bytes + 16 * 1024 * 1024)
    return pltpu.CompilerParams(dimension_semantics=semantics, vmem_limit_bytes=int(limit))


def _log_sigmoid(x):
    return jnp.minimum(x, 0.0) - jnp.log(1.0 + jnp.exp(-jnp.abs(x)))


def _sigmoid(x):
    return 1.0 / (1.0 + jnp.exp(-x))


def _split3(x):
    hi = x.astype(BF16)
    r1 = x - hi.astype(F32)
    mid = r1.astype(BF16)
    lo = (r1 - mid.astype(F32)).astype(BF16)
    return hi, mid, lo


def _rmsnorm_kernel(x_ref, g_ref, o_ref):
    x = x_ref[...]
    y = x * lax.rsqrt(jnp.mean(x * x, axis=-1, keepdims=True) + EPS)
    o_ref[...] = (y * g_ref[...]).astype(o_ref.dtype)


def rmsnorm(x, g, out_dtype, row0=0, rows=None):
    D = x.shape[1]
    T = x.shape[0] if rows is None else rows
    tr = _pick(T, (272, 256, 160, 128, 64))
    assert row0 % tr == 0
    blk0 = row0 // tr
    nbytes = 2 * tr * D * (4 + jnp.dtype(out_dtype).itemsize)
    return pl.pallas_call(
        _rmsnorm_kernel,
        grid=(T // tr,),
        in_specs=[pl.BlockSpec((tr, D), lambda i: (blk0 + i, 0)), pl.BlockSpec((1, D), lambda i: (0, 0))],
        out_specs=pl.BlockSpec((tr, D), lambda i: (i, 0)),
        out_shape=jax.ShapeDtypeStruct((T, D), out_dtype),
        compiler_params=_params(("parallel",), nbytes),
        name="rmsnorm",
    )(x, g.reshape(1, D))


def _m_tile(T):
    return _pick(T, (1088, 1024, 640, 512, 320, 256, 128, 64))


def _proj_kernel(*refs, n_x, has_res, w_nk):
    x_refs = refs[:n_x]
    w_ref = refs[n_x]
    r_ref = refs[n_x + 1] if has_res else None
    o_ref = refs[n_x + 1 + has_res]
    wb_ref = refs[n_x + 2 + has_res]

    @pl.when(pl.program_id(1) == 0)
    def _():
        wb_ref[...] = w_ref[...].astype(BF16)

    k0 = 0
    acc = None
    for x_ref in x_refs:
        kx = x_ref.shape[1]
        if w_nk:
            part = lax.dot_general(x_ref[...], wb_ref[:, k0:k0 + kx], NT, preferred_element_type=F32)
        else:
            part = jnp.dot(x_ref[...], wb_ref[k0:k0 + kx, :], preferred_element_type=F32)
        acc = part if acc is None else acc + part
        k0 += kx
    if has_res:
        acc = r_ref[...] + acc
    o_ref[...] = acc.astype(o_ref.dtype)


def project(xs, w, layer, n_cols, out_dtype=F32, residual=None, w_nk=False, name="project"):
    T = xs[0].shape[0]
    K = w.shape[2] if w_nk else w.shape[1]
    assert sum(x.shape[1] for x in xs) == K
    tm = _m_tile(T)
    tn = _pick(n_cols, (512, 256, 128))
    osz = jnp.dtype(out_dtype).itemsize
    nbytes = 2 * (tm * K * 2 + K * tn * 4 + tm * tn * osz) + K * tn * 2 + tm * tn * 4
    in_specs = [pl.BlockSpec((tm, x.shape[1]), lambda j, i: (i, 0)) for x in xs]
    if w_nk:
        in_specs.append(pl.BlockSpec((None, tn, K), lambda j, i: (layer, j, 0)))
    else:
        in_specs.append(pl.BlockSpec((None, K, tn), lambda j, i: (layer, 0, j)))
    args = list(xs) + [w]
    if residual is not None:
        in_specs.append(pl.BlockSpec((tm, tn), lambda j, i: (i, j)))
        args.append(residual)
        nbytes += 2 * tm * tn * 4
    return pl.pallas_call(
        functools.partial(_proj_kernel, n_x=len(xs), has_res=residual is not None, w_nk=w_nk),
        grid=(n_cols // tn, T // tm),
        in_specs=in_specs,
        out_specs=pl.BlockSpec((tm, tn), lambda j, i: (i, j)),
        out_shape=jax.ShapeDtypeStruct((T, n_cols), out_dtype),
        scratch_shapes=[pltpu.VMEM((tn, K) if w_nk else (K, tn), BF16)],
        compiler_params=_params(("arbitrary", "arbitrary"), nbytes),
        name=name,
    )(*args)


def _ffn_in_kernel(x_ref, wg_ref, wu_ref, o_ref, wgb_ref, wub_ref):
    @pl.when(pl.program_id(1) == 0)
    def _():
        wgb_ref[...] = wg_ref[...].astype(BF16)
        wub_ref[...] = wu_ref[...].astype(BF16)

    x = x_ref[...]
    g = jnp.dot(x, wgb_ref[...], preferred_element_type=F32)
    u = jnp.dot(x, wub_ref[...], preferred_element_type=F32)
    o_ref[...] = (g * _sigmoid(g) * u).astype(o_ref.dtype)


def ffn_in(x, w_in, layer):
    T, K = x.shape
    F = w_in.shape[2] // 2
    tm = _m_tile(T)
    tn = _pick(F, (256, 128))
    nf = F // tn
    nbytes = 2 * (tm * K * 2 + 2 * K * tn * 4 + tm * tn * 2) + 2 * K * tn * 2 + 3 * tm * tn * 4
    return pl.pallas_call(
        _ffn_in_kernel,
        grid=(nf, T // tm),
        in_specs=[pl.BlockSpec((tm, K), lambda j, i: (i, 0)),
                  pl.BlockSpec((None, K, tn), lambda j, i: (layer, 0, j)),
                  pl.BlockSpec((None, K, tn), lambda j, i: (layer, 0, j + nf))],
        out_specs=pl.BlockSpec((tm, tn), lambda j, i: (i, j)),
        out_shape=jax.ShapeDtypeStruct((T, F), BF16),
        scratch_shapes=[pltpu.VMEM((K, tn), BF16), pltpu.VMEM((K, tn), BF16)],
        compiler_params=_params(("arbitrary", "arbitrary"), nbytes),
        name="ffn_in",
    )(x, w_in, w_in)


def _mm_res_ksplit_kernel(x_ref, w_ref, r_ref, o_ref, acc_ref):
    k = pl.program_id(2)

    @pl.when(k == 0)
    def _():
        acc_ref[...] = r_ref[...]

    acc_ref[...] += jnp.dot(x_ref[...], w_ref[...], preferred_element_type=F32)

    @pl.when(k == pl.num_programs(2) - 1)
    def _():
        o_ref[...] = acc_ref[...]


def matmul_res_ksplit(x, w, layer, residual):
    T, K = x.shape
    N = w.shape[2]
    tm = _m_tile(T)
    tn = _pick(N, (512, 256, 128))
    nk = 2
    tk = K // nk
    assert tk * nk == K and tk % LANES == 0
    nbytes = 2 * (tm * tk * 2 + tk * tn * 2 + 2 * tm * tn * 4) + 2 * tm * tn * 4
    return pl.pallas_call(
        _mm_res_ksplit_kernel,
        grid=(T // tm, N // tn, nk),
        in_specs=[pl.BlockSpec((tm, tk), lambda i, j, k: (i, k)),
                  pl.BlockSpec((None, tk, tn), lambda i, j, k: (layer, k, j)),
                  pl.BlockSpec((tm, tn), lambda i, j, k: (i, j))],
        out_specs=pl.BlockSpec((tm, tn), lambda i, j, k: (i, j)),
        out_shape=jax.ShapeDtypeStruct((T, N), F32),
        scratch_shapes=[pltpu.VMEM((tm, tn), F32)],
        compiler_params=_params(("parallel", "arbitrary", "arbitrary"), nbytes),
        name="ffn_out",
    )(x, w, residual)


class Units:
    def __init__(self, n_prompt, chunks_per_prompt, n_sample):
        self.ncp = chunks_per_prompt
        self.up = n_prompt * chunks_per_prompt
        self.n_prompt = n_prompt
        self.n_sample = n_sample
        self.total = self.up + n_sample
        self.n_streams = n_prompt + n_sample

    def is_sample(self, u):
        return u >= self.up

    def stream(self, u):
        return jnp.where(u < self.up, u // self.ncp, self.n_prompt + u - self.up)

    def sample_index(self, u):
        return jnp.maximum(u - self.up, 0)

    def first(self, u):
        return jnp.logical_or(u >= self.up, u % self.ncp == 0)

    def last(self, u):
        return jnp.logical_or(u >= self.up, u % self.ncp == self.ncp - 1)


def _mlstm_kernel(bi_ref, bf_ref, m0_ref, q_ref, k_ref, v_ref, og_ref, gate_ref, gh_ref, c0_ref, n0_ref,
                  hm_ref, cout_ref, nout_ref, mout_ref, c_s, n_s, m_s, *, units, n_heads):
    u = pl.program_id(0)
    is_sample = units.is_sample(u)
    L = CHUNK
    H = n_heads
    dqk = c_s.shape[1]
    dv = c_s.shape[2]

    @pl.when(jnp.logical_and(units.first(u), jnp.logical_not(is_sample)))
    def _():
        c_s[...] = jnp.zeros_like(c_s)
        n_s[...] = jnp.zeros_like(n_s)
        m_s[...] = jnp.zeros_like(m_s)

    @pl.when(is_sample)
    def _():
        c_s[...] = c0_ref[...]
        n_s[...] = n0_ref[...]
        for hh in range(H):
            m_s[hh] = jnp.full(m_s.shape[1:], m0_ref[units.sample_index(u), hh], F32)

    gates = gate_ref[...]
    lane = lax.broadcasted_iota(jnp.int32, gates.shape, 1)
    row = lax.broadcasted_iota(jnp.int32, (L, L), 0)
    col = lax.broadcasted_iota(jnp.int32, (L, L), 1)
    eye = row == col
    tril = col <= row
    triu = row <= col

    for hh in range(H):
        ig_col = jnp.sum(jnp.where(lane == hh, gates, 0.0), axis=1, keepdims=True) + bi_ref[hh]
        f_col = jnp.sum(jnp.where(lane == H + hh, gates, 0.0), axis=1, keepdims=True) + bf_ref[hh]
        lf_col = _log_sigmoid(f_col)
        ig_row = jnp.sum(jnp.where(eye, ig_col, 0.0), axis=0, keepdims=True)
        lf_row = jnp.sum(jnp.where(eye, lf_col, 0.0), axis=0, keepdims=True)
        b_col = jnp.sum(jnp.where(tril, lf_row, 0.0), axis=1, keepdims=True)
        b_row = jnp.sum(jnp.where(triu, lf_col, 0.0), axis=0, keepdims=True)

        d = jnp.where(tril, b_col - b_row + ig_row, NEG_INF)
        m_prev = m_s[hh][:, :1]
        inter = b_col + m_prev
        m_t = jnp.maximum(inter, jnp.max(d, axis=1, keepdims=True))
        w_intra = jnp.exp(d - m_t)
        w_inter = jnp.exp(inter - m_t)

        q = q_ref[:, hh * dqk:(hh + 1) * dqk]
        k = k_ref[:, hh * dqk:(hh + 1) * dqk] * (dqk ** -0.5)
        vb = v_ref[:, hh * dv:(hh + 1) * dv].astype(BF16)
        qb = q.astype(BF16)
        c_prev = c_s[hh]
        n_prev = n_s[hh]
        a = lax.dot_general(qb, k.astype(BF16), NT, preferred_element_type=F32) * w_intra
        num = (jnp.dot(a.astype(BF16), vb, preferred_element_type=F32)
               + jnp.dot(qb, c_prev.astype(BF16), preferred_element_type=F32) * w_inter)
        den = jnp.sum(a, axis=1, keepdims=True) + jnp.sum(q * n_prev, axis=1, keepdims=True) * w_inter
        den = jnp.maximum(jnp.abs(den), jnp.exp(-m_t))
        hval = num / den

        y = hval * lax.rsqrt(jnp.mean(hval * hval, axis=1, keepdims=True) + EPS)
        gate_out = _sigmoid(og_ref[:, hh * dv:(hh + 1) * dv])
        hm_ref[:, hh * dv:(hh + 1) * dv] = (y * gh_ref[:, hh * dv:(hh + 1) * dv] * gate_out).astype(hm_ref.dtype)

        m_new = m_t[L - 1:L, :]
        b_last = b_col[L - 1:L, :]
        w_last = jnp.exp(b_last - b_col + ig_col - m_new)
        decay = jnp.exp(b_last + m_prev - m_new)
        k_w = k * w_last
        c_s[hh] = decay * c_prev + lax.dot_general(k_w.astype(BF16), vb, TN, preferred_element_type=F32)
        n_s[hh] = decay * n_prev + jnp.sum(k_w, axis=0, keepdims=True)
        m_s[hh] = jnp.broadcast_to(m_new, m_s.shape[1:])

    @pl.when(units.last(u))
    def _():
        cout_ref[...] = c_s[...]
        nout_ref[...] = n_s[...]
        mout_ref[...] = m_s[...]


def mlstm_mixer(z, gates, b_i, b_f, g_head, c0, n0, m0, units, n_heads, dqk, dv):
    T = z.shape[0]
    H = n_heads
    ns = units.n_streams
    qw = H * dqk
    vw = H * dv
    assert (2 * qw) % vw == 0
    v_blk = (2 * qw) // vw
    sidx = units.sample_index
    smem = pl.BlockSpec(memory_space=pltpu.SMEM)
    nbytes = (2 * (2 * CHUNK * qw * 4 + 2 * CHUNK * vw * 4 + CHUNK * LANES * 4 + CHUNK * vw * 2 + 2 * qw * dv * 4)
              + qw * dv * 4 * 4)
    hm, c_out, n_out, m_out = pl.pallas_call(
        functools.partial(_mlstm_kernel, units=units, n_heads=H),
        grid=(units.total,),
        in_specs=[smem, smem, smem,
                  pl.BlockSpec((CHUNK, qw), lambda u: (u, 0)),
                  pl.BlockSpec((CHUNK, qw), lambda u: (u, 1)),
                  pl.BlockSpec((CHUNK, vw), lambda u: (u, v_blk)),
                  pl.BlockSpec((CHUNK, vw), lambda u: (u, v_blk + 1)),
                  pl.BlockSpec((CHUNK, LANES), lambda u: (u, 0)),
                  pl.BlockSpec((1, vw), lambda u: (0, 0)),
                  pl.BlockSpec((None, H, dqk, dv), lambda u: (sidx(u), 0, 0, 0)),
                  pl.BlockSpec((None, H, 1, dqk), lambda u: (sidx(u), 0, 0, 0))],
        out_specs=[pl.BlockSpec((CHUNK, vw), lambda u: (u, 0)),
                   pl.BlockSpec((None, H, dqk, dv), lambda u: (units.stream(u), 0, 0, 0)),
                   pl.BlockSpec((None, H, 1, dqk), lambda u: (units.stream(u), 0, 0, 0)),
                   pl.BlockSpec((None, H, 1, LANES), lambda u: (units.stream(u), 0, 0, 0))],
        out_shape=[jax.ShapeDtypeStruct((T, vw), BF16),
                   jax.ShapeDtypeStruct((ns, H, dqk, dv), F32),
                   jax.ShapeDtypeStruct((ns, H, 1, dqk), F32),
                   jax.ShapeDtypeStruct((ns, H, 1, LANES), F32)],
        scratch_shapes=[pltpu.VMEM((H, dqk, dv), F32), pltpu.VMEM((H, 1, dqk), F32), pltpu.VMEM((H, 1, LANES), F32)],
        compiler_params=_params(("arbitrary",), nbytes),
        name="mlstm",
    )(b_i, b_f, m0, z, z, z, z, gates, g_head.reshape(1, vw), c0, n0.reshape(n0.shape[0], H, 1, dqk))
    return hm, c_out, n_out[:, :, 0, :], m_out[:, :, 0, 0]


def _swa_kernel(sink_ref, q_ref, k0_ref, k1_ref, k2_ref, v0_ref, v1_ref, v2_ref, ck1_ref, ck2_ref, cv1_ref, cv2_ref,
                o_ref, *, units, n_kv):
    u = pl.program_id(0)
    is_sample = units.is_sample(u)
    L = CHUNK
    W = 2 * L
    NK = W + L
    hd = SWA_HEAD_DIM
    pairs = SWA_GROUP // 2
    R = pairs * L

    def pick(c_ref, z_ref):
        return jnp.where(is_sample, c_ref[...], z_ref[...])

    k_all = jnp.concatenate([pick(ck2_ref, k2_ref), pick(ck1_ref, k1_ref), k0_ref[...]], axis=0)
    v_all = jnp.concatenate([pick(cv2_ref, v2_ref), pick(cv1_ref, v1_ref), v0_ref[...]], axis=0)

    n = u % units.ncp
    j_min = jnp.where(is_sample, 0, jnp.maximum(W - L * n, 0))
    jj = lax.broadcasted_iota(jnp.int32, (R, NK), 1)
    tt = lax.broadcasted_iota(jnp.int32, (R, NK), 0) % L
    dist = jnp.abs(tt + W - jj).astype(F32)
    valid = jj >= j_min
    pair_of_row = lax.broadcasted_iota(jnp.int32, (R, 1), 0) // L
    lane = lax.broadcasted_iota(jnp.int32, (NK, LANES), 1)
    lo_lanes = lane < hd
    n_heads = n_kv * SWA_GROUP

    for g in range(n_kv):
        tile = (g * hd) // LANES
        kt = k_all[:, tile * LANES:(tile + 1) * LANES]
        vt = v_all[:, tile * LANES:(tile + 1) * LANES]
        kr = pltpu.roll(kt, hd, axis=1)
        vr = pltpu.roll(vt, hd, axis=1)
        if (g * hd) % LANES == 0:
            k_lo, k_hi, v_lo, v_hi = kt, kr, vt, vr
        else:
            k_lo, k_hi, v_lo, v_hi = kr, kt, vr, vt
        k_lo = jnp.where(lo_lanes, k_lo, 0.0).astype(BF16)
        k_hi = jnp.where(lo_lanes, 0.0, k_hi).astype(BF16)
        v_lo = jnp.where(lo_lanes, v_lo, 0.0).astype(BF16)
        v_hi = jnp.where(lo_lanes, 0.0, v_hi).astype(BF16)
        qg = jnp.concatenate(
            [q_ref[:, (g * pairs + p) * LANES:(g * pairs + p + 1) * LANES] for p in range(pairs)], axis=0)
        qg = (qg * (hd ** -0.5)).astype(BF16)

        def probs(k_half, odd):
            s = lax.dot_general(qg, k_half, NT, preferred_element_type=F32)
            head = g * SWA_GROUP + 2 * pair_of_row + odd
            slope = jnp.exp2(-8.0 * (head + 1).astype(F32) / n_heads)
            sink = jnp.zeros((R, 1), F32)
            for p in range(pairs):
                sink = jnp.where(pair_of_row == p, sink_ref[g * SWA_GROUP + 2 * p + odd], sink)
            s = jnp.where(valid, s - slope * dist, NEG_INF)
            mx = jnp.maximum(jnp.max(s, axis=1, keepdims=True), sink)
            e = jnp.exp(s - mx)
            return e / (jnp.sum(e, axis=1, keepdims=True) + jnp.exp(sink - mx))

        o = (jnp.dot(probs(k_lo, 0).astype(BF16), v_lo, preferred_element_type=F32)
             + jnp.dot(probs(k_hi, 1).astype(BF16), v_hi, preferred_element_type=F32))
        for p in range(pairs):
            o_ref[:, (g * pairs + p) * LANES:(g * pairs + p + 1) * LANES] = o[p * L:(p + 1) * L].astype(o_ref.dtype)


def swa_mixer(z, sinks, cache_k, cache_v, units, n_kv):
    T = z.shape[0]
    qw = n_kv * SWA_GROUP * SWA_HEAD_DIM
    kw = n_kv * SWA_HEAD_DIM
    assert kw % LANES == 0 and qw % kw == 0
    kb = qw // kw
    vb = kb + 1
    sidx = units.sample_index

    def zspec(blk, back):
        return pl.BlockSpec((CHUNK, kw), lambda u: (jnp.maximum(u - back, 0), blk))

    def cspec(part):
        return pl.BlockSpec((None, CHUNK, kw), lambda u: (sidx(u), part, 0))

    nbytes = 2 * (CHUNK * qw * 4 + 10 * CHUNK * kw * 4 + CHUNK * qw * 2) + 16 * 4 * CHUNK * 3 * CHUNK * 4
    return pl.pallas_call(
        functools.partial(_swa_kernel, units=units, n_kv=n_kv),
        grid=(units.total,),
        in_specs=[pl.BlockSpec(memory_space=pltpu.SMEM),
                  pl.BlockSpec((CHUNK, qw), lambda u: (u, 0)),
                  zspec(kb, 0), zspec(kb, 1), zspec(kb, 2),
                  zspec(vb, 0), zspec(vb, 1), zspec(vb, 2),
                  cspec(1), cspec(0), cspec(1), cspec(0)],
        out_specs=pl.BlockSpec((CHUNK, qw), lambda u: (u, 0)),
        out_shape=jax.ShapeDtypeStruct((T, qw), BF16),
        compiler_params=_params(("arbitrary",), nbytes),
        name="swa",
    )(sinks, z, z, z, z, z, z, z, cache_k, cache_k, cache_v, cache_v)


def _gla_kernel(q_ref, k_ref, v_ref, r_ref, glr_ref, wg_ref, bg_ref, gh_ref, s0_ref, o_ref, sout_ref, s_s, *, units):
    u = pl.program_id(1)
    is_sample = units.is_sample(u)
    L = CHUNK
    hb, dv, dk = s_s.shape

    @pl.when(jnp.logical_and(units.first(u), jnp.logical_not(is_sample)))
    def _():
        s_s[...] = jnp.zeros_like(s_s)

    @pl.when(is_sample)
    def _():
        s_s[...] = s0_ref[...]

    glr = glr_ref[...].astype(BF16)
    row = lax.broadcasted_iota(jnp.int32, (L, L), 0)
    col = lax.broadcasted_iota(jnp.int32, (L, L), 1)
    tril = col <= row
    ones_tril = jnp.where(tril, 1.0, 0.0).astype(BF16)
    rr = lax.broadcasted_iota(jnp.int32, (L, dk), 0)
    n_sub = L // GLA_SUB

    for hh in range(hb):
        ks = slice(hh * dk, (hh + 1) * dk)
        vs = slice(hh * dv, (hh + 1) * dv)
        zg = jnp.dot(glr, wg_ref[:, ks], preferred_element_type=F32) + bg_ref[:, ks]
        lg = _log_sigmoid(zg) * (1.0 / GLA_TAU)
        bc = sum(jnp.dot(ones_tril, part, preferred_element_type=F32) for part in _split3(lg))

        q = q_ref[:, ks] * (dk ** -0.5)
        k = k_ref[:, ks]
        vb = v_ref[:, vs].astype(BF16)
        st = s_s[hh]

        refs = [bc[i * GLA_SUB:i * GLA_SUB + 1, :] for i in range(n_sub)]
        r_q = refs[n_sub - 1]
        for i in range(n_sub - 2, -1, -1):
            r_q = jnp.where(rr < (i + 1) * GLA_SUB, refs[i], r_q)
        q_t = (q * jnp.exp(bc - r_q)).astype(BF16)
        blocks = []
        for i in range(n_sub):
            e_i = jnp.where(rr < (i + 1) * GLA_SUB, refs[i] - bc, NEG_INF)
            k_i = (k * jnp.exp(e_i)).astype(BF16)
            blocks.append(lax.dot_general(q_t[i * GLA_SUB:(i + 1) * GLA_SUB], k_i, NT, preferred_element_type=F32))
        a = jnp.where(tril, jnp.concatenate(blocks, axis=0), 0.0)

        o = (jnp.dot(a.astype(BF16), vb, preferred_element_type=F32)
             + lax.dot_general((q * jnp.exp(bc)).astype(BF16), st.astype(BF16), NT, preferred_element_type=F32))

        b_last = bc[L - 1:L, :]
        k_dec = (k * jnp.exp(b_last - bc)).astype(BF16)
        s_s[hh] = jnp.exp(b_last) * st + lax.dot_general(vb, k_dec, TN, preferred_element_type=F32)

        y = o * lax.rsqrt(jnp.mean(o * o, axis=1, keepdims=True) + EPS)
        r = r_ref[:, vs]
        o_ref[:, vs] = (y * gh_ref[:, vs] * (r * _sigmoid(r))).astype(o_ref.dtype)

    @pl.when(units.last(u))
    def _():
        sout_ref[...] = s_s[...]


def gla_mixer(z, glr, w_g2, b_g, g_head, s0t, units, n_heads, dk, dv):
    T = z.shape[0]
    H = n_heads
    hb = GLA_HEADS_PER_STEP if H % GLA_HEADS_PER_STEP == 0 else 1
    ng = H // hb
    ns = units.n_streams
    kw = hb * dk
    vw = hb * dv
    assert (2 * H * dk) % vw == 0
    v_blk0 = (2 * H * dk) // vw
    sidx = units.sample_index
    nbytes = (2 * (2 * CHUNK * kw * 4 + 2 * CHUNK * vw * 4 + CHUNK * LANES * 4 + LANES * kw * 2 + CHUNK * vw * 2
                   + 2 * kw * dv * 4) + kw * dv * 4 * 4)
    out, st = pl.pallas_call(
        functools.partial(_gla_kernel, units=units),
        grid=(ng, units.total),
        in_specs=[pl.BlockSpec((CHUNK, kw), lambda g, u: (u, g)),
                  pl.BlockSpec((CHUNK, kw), lambda g, u: (u, ng + g)),
                  pl.BlockSpec((CHUNK, vw), lambda g, u: (u, v_blk0 + g)),
                  pl.BlockSpec((CHUNK, vw), lambda g, u: (u, v_blk0 + ng + g)),
                  pl.BlockSpec((CHUNK, LANES), lambda g, u: (u, 0)),
                  pl.BlockSpec((LANES, kw), lambda g, u: (0, g)),
                  pl.BlockSpec((1, kw), lambda g, u: (0, g)),
                  pl.BlockSpec((1, vw), lambda g, u: (0, g)),
                  pl.BlockSpec((None, hb, dv, dk), lambda g, u: (sidx(u), g, 0, 0))],
        out_specs=[pl.BlockSpec((CHUNK, vw), lambda g, u: (u, g)),
                   pl.BlockSpec((None, hb, dv, dk), lambda g, u: (units.stream(u), g, 0, 0))],
        out_shape=[jax.ShapeDtypeStruct((T, H * dv), BF16),
                   jax.ShapeDtypeStruct((ns, H, dv, dk), F32)],
        scratch_shapes=[pltpu.VMEM((hb, dv, dk), F32)],
        compiler_params=_params(("parallel", "arbitrary"), nbytes),
        name="gla",
    )(z, z, z, z, glr, w_g2, b_g.reshape(1, H * dk), g_head.reshape(1, H * dv), s0t)
    return out, st


def _pad_rows(w, height):
    return jnp.pad(w, ((0, height - w.shape[0]), (0, 0)))


def kernel(x_prompt, x_sample, cache_swa_k, cache_swa_v, state_mlstm_C, state_mlstm_n, state_mlstm_m, state_gla_S,
           norm_mix, norm_ffn, norm_final, w_even_in, b_mlstm_i, b_mlstm_f, mlstm_head_norm, swa_sinks, w_even_out,
           w_odd_in, w_gla_gate, b_gla_gate, gla_head_norm, w_odd_out, w_ffn_in, w_ffn_out):
    B, S, D = x_prompt.shape
    DB, DS, _ = x_sample.shape
    depth = norm_mix.shape[0]
    assert DS == CHUNK and S % CHUNK == 0
    MH, dqk, dv = state_mlstm_C.shape[2:]
    n_kv = cache_swa_k.shape[3]
    w_buf = cache_swa_k.shape[2]
    assert w_buf == 2 * CHUNK and cache_swa_k.shape[4] == SWA_HEAD_DIM
    GH, gdk, gdv = state_gla_S.shape[2:]
    rank = w_gla_gate.shape[1]
    m_w = 2 * MH * dqk + 2 * MH * dv
    sq_w = n_kv * SWA_GROUP * SWA_HEAD_DIM
    kvw = n_kv * SWA_HEAD_DIM
    s_w = sq_w + 2 * kvw
    g_w = 2 * GH * gdk + 2 * GH * gdv
    units = Units(B, S // CHUNK, DB)
    TP = B * S

    x = jnp.concatenate([x_prompt.reshape(TP, D), x_sample.reshape(DB * DS, D)], axis=0)
    w_even_in_t = jnp.swapaxes(w_even_in, 1, 2)
    w_odd_in_t = jnp.swapaxes(w_odd_in, 1, 2)
    w_ffn_out_b = w_ffn_out.astype(BF16)

    p_k, p_v, p_C, p_n, p_m, p_S = [], [], [], [], [], []
    s_k, s_v, s_C, s_n, s_m, s_S = [], [], [], [], [], []
    for l in range(depth):
        e = l // 2
        h = rmsnorm(x, norm_mix[l], BF16)
        if l % 2 == 0:
            w_gate = _pad_rows(w_even_in_t[e, m_w:m_w + 2 * MH], LANES)[None]
            w_attn = w_even_in_t[e, m_w + 2 * MH:][None]
            zm = project([h], w_even_in_t, e, m_w, w_nk=True, name="even_in_mlstm")
            zs = project([h], w_attn, 0, s_w, w_nk=True, name="even_in_attn")
            gates = project([h], w_gate, 0, LANES, w_nk=True, name="even_gates")
            hm, c_out, n_out, m_out = mlstm_mixer(
                zm, gates, b_mlstm_i[e], b_mlstm_f[e], mlstm_head_norm[e],
                state_mlstm_C[e], state_mlstm_n[e], state_mlstm_m[e], units, MH, dqk, dv)
            ck = cache_swa_k[e].reshape(DB, w_buf, kvw)
            cv = cache_swa_v[e].reshape(DB, w_buf, kvw)
            hs = swa_mixer(zs, swa_sinks[e], ck, cv, units, n_kv)
            x = project([hm, hs], w_even_out, e, D, residual=x, name="even_out")
            zk = zs[:, sq_w:sq_w + kvw]
            zv = zs[:, sq_w + kvw:]
            kv_shape = (w_buf, n_kv, SWA_HEAD_DIM)
            p_k.append(zk[:TP].reshape(B, S, kvw)[:, -w_buf:].reshape((B,) + kv_shape))
            p_v.append(zv[:TP].reshape(B, S, kvw)[:, -w_buf:].reshape((B,) + kv_shape))
            s_k.append(jnp.concatenate([ck[:, CHUNK:], zk[TP:].reshape(DB, DS, kvw)], axis=1).reshape((DB,) + kv_shape))
            s_v.append(jnp.concatenate([cv[:, CHUNK:], zv[TP:].reshape(DB, DS, kvw)], axis=1).reshape((DB,) + kv_shape))
            p_C.append(c_out[:B]); p_n.append(n_out[:B]); p_m.append(m_out[:B])
            s_C.append(c_out[B:]); s_n.append(n_out[B:]); s_m.append(m_out[B:])
        else:
            z = project([h], w_odd_in_t, e, g_w, w_nk=True, name="odd_in")
            w_rank = _pad_rows(w_odd_in_t[e, g_w:], LANES)[None]
            glr = project([h], w_rank, 0, LANES, w_nk=True, name="odd_gate_rank")
            w_g2 = jnp.pad(w_gla_gate[e], ((0, LANES - rank), (0, 0))).astype(BF16)
            mix, st = gla_mixer(z, glr, w_g2, b_gla_gate[e], gla_head_norm[e],
                                jnp.swapaxes(state_gla_S[e], -1, -2), units, GH, gdk, gdv)
            x = project([mix], w_odd_out, e, D, residual=x, name="odd_out")
            s_full = jnp.swapaxes(st, -1, -2)
            p_S.append(s_full[:B]); s_S.append(s_full[B:])
        hf = rmsnorm(x, norm_ffn[l], BF16)
        act = ffn_in(hf, w_ffn_in, l)
        x = matmul_res_ksplit(act, w_ffn_out_b, l, x)
    y_prompt = rmsnorm(x, norm_final, F32, 0, TP)
    y_sample = rmsnorm(x, norm_final, F32, TP, DB * DS)
    return (y_prompt.reshape(B, S, D), y_sample.reshape(DB, DS, D),
            jnp.stack(p_k), jnp.stack(p_v), jnp.stack(p_C), jnp.stack(p_n), jnp.stack(p_m), jnp.stack(p_S),
            jnp.stack(s_k), jnp.stack(s_v), jnp.stack(s_C), jnp.stack(s_n), jnp.stack(s_m), jnp.stack(s_S))
```

```python
import functools

import jax
import jax.numpy as jnp
from jax import lax
from jax.experimental import pallas as pl
from jax.experimental.pallas import tpu as pltpu

F32 = jnp.float32
BF16 = jnp.bfloat16

CHUNK = 64
SWA_HEAD_DIM = 64
SWA_GROUP = 8
LANES = 128
GLA_SUB = 16
GLA_HEADS_PER_STEP = 8
GLA_TAU = 16.0
EPS = 1e-6
NEG_INF = float("-inf")

V7X_VMEM_BYTES = 64 * 1024 * 1024
VMEM_CAP_BYTES = V7X_VMEM_BYTES - 6 * 1024 * 1024

NT = (((1,), (1,)), ((), ()))
TN = (((0,), (0,)), ((), ()))


def _pick(n, cands):
    for c in cands:
        if n % c == 0:
            return c
    raise ValueError(f"no tile in {cands} divides {n}")


def _params(semantics, block_bytes):
    limit = min(VMEM_CAP_BYTES, block_bytes + 16 * 1024 * 1024)
    return pltpu.CompilerParams(dimension_semantics=semantics, vmem_limit_bytes=int(limit))


def _log_sigmoid(x):
    return jnp.minimum(x, 0.0) - jnp.log(1.0 + jnp.exp(-jnp.abs(x)))


def _sigmoid(x):
    return 1.0 / (1.0 + jnp.exp(-x))


def _split3(x):
    hi = x.astype(BF16)
    r1 = x - hi.astype(F32)
    mid = r1.astype(BF16)
    lo = (r1 - mid.astype(F32)).astype(BF16)
    return hi, mid, lo


def _rmsnorm_kernel(x_ref, g_ref, o_ref):
    x = x_ref[...]
    y = x * lax.rsqrt(jnp.mean(x * x, axis=-1, keepdims=True) + EPS)
    o_ref[...] = (y * g_ref[...]).astype(o_ref.dtype)


def rmsnorm(x, g, out_dtype, row0=0, rows=None):
    D = x.shape[1]
    T = x.shape[0] if rows is None else rows
    tr = _pick(T, (272, 256, 160, 128, 64))
    assert row0 % tr == 0
    blk0 = row0 // tr
    nbytes = 2 * tr * D * (4 + jnp.dtype(out_dtype).itemsize)
    return pl.pallas_call(
        _rmsnorm_kernel,
        grid=(T // tr,),
        in_specs=[pl.BlockSpec((tr, D), lambda i: (blk0 + i, 0)), pl.BlockSpec((1, D), lambda i: (0, 0))],
        out_specs=pl.BlockSpec((tr, D), lambda i: (i, 0)),
        out_shape=jax.ShapeDtypeStruct((T, D), out_dtype),
        compiler_params=_params(("parallel",), nbytes),
        name="rmsnorm",
    )(x, g.reshape(1, D))


def _m_tile(T):
    return _pick(T, (1088, 1024, 640, 512, 320, 256, 128, 64))


def _proj_kernel(*refs, n_x, has_res, w_nk):
    x_refs = refs[:n_x]
    w_ref = refs[n_x]
    r_ref = refs[n_x + 1] if has_res else None
    o_ref = refs[n_x + 1 + has_res]
    k0 = 0
    acc = None
    for x_ref in x_refs:
        kx = x_ref.shape[1]
        if w_nk:
            part = lax.dot_general(x_ref[...], w_ref[:, k0:k0 + kx].astype(BF16), NT, preferred_element_type=F32)
        else:
            part = jnp.dot(x_ref[...], w_ref[k0:k0 + kx, :].astype(BF16), preferred_element_type=F32)
        acc = part if acc is None else acc + part
        k0 += kx
    if has_res:
        acc = r_ref[...] + acc
    o_ref[...] = acc.astype(o_ref.dtype)


def project(xs, w, layer, n_cols, out_dtype=F32, residual=None, w_nk=False, name="project"):
    T = xs[0].shape[0]
    K = w.shape[2] if w_nk else w.shape[1]
    assert sum(x.shape[1] for x in xs) == K
    tm = _m_tile(T)
    tn = _pick(n_cols, (512, 256, 128))
    osz = jnp.dtype(out_dtype).itemsize
    nbytes = 2 * (tm * K * 2 + K * tn * 4 + tm * tn * osz) + K * tn * 2 + tm * tn * 4
    in_specs = [pl.BlockSpec((tm, x.shape[1]), lambda i, j: (i, 0)) for x in xs]
    if w_nk:
        in_specs.append(pl.BlockSpec((None, tn, K), lambda i, j: (layer, j, 0)))
    else:
        in_specs.append(pl.BlockSpec((None, K, tn), lambda i, j: (layer, 0, j)))
    args = list(xs) + [w]
    if residual is not None:
        in_specs.append(pl.BlockSpec((tm, tn), lambda i, j: (i, j)))
        args.append(residual)
        nbytes += 2 * tm * tn * 4
    return pl.pallas_call(
        functools.partial(_proj_kernel, n_x=len(xs), has_res=residual is not None, w_nk=w_nk),
        grid=(T // tm, n_cols // tn),
        in_specs=in_specs,
        out_specs=pl.BlockSpec((tm, tn), lambda i, j: (i, j)),
        out_shape=jax.ShapeDtypeStruct((T, n_cols), out_dtype),
        compiler_params=_params(("parallel", "arbitrary"), nbytes),
        name=name,
    )(*args)


def _ffn_in_kernel(x_ref, wg_ref, wu_ref, o_ref):
    x = x_ref[...]
    g = jnp.dot(x, wg_ref[...].astype(BF16), preferred_element_type=F32)
    u = jnp.dot(x, wu_ref[...].astype(BF16), preferred_element_type=F32)
    o_ref[...] = (g * _sigmoid(g) * u).astype(o_ref.dtype)


def ffn_in(x, w_in, layer):
    T, K = x.shape
    F = w_in.shape[2] // 2
    tm = _pick(T, (2176, 1088, 640, 512, 256, 128, 64))
    tn = _pick(F, (256, 128))
    nf = F // tn
    nbytes = 2 * (tm * K * 2 + 2 * K * tn * 4 + tm * tn * 2) + 2 * K * tn * 2 + 3 * tm * tn * 4
    return pl.pallas_call(
        _ffn_in_kernel,
        grid=(T // tm, nf),
        in_specs=[pl.BlockSpec((tm, K), lambda i, j: (i, 0)),
                  pl.BlockSpec((None, K, tn), lambda i, j: (layer, 0, j)),
                  pl.BlockSpec((None, K, tn), lambda i, j: (layer, 0, j + nf))],
        out_specs=pl.BlockSpec((tm, tn), lambda i, j: (i, j)),
        out_shape=jax.ShapeDtypeStruct((T, F), BF16),
        compiler_params=_params(("parallel", "arbitrary"), nbytes),
        name="ffn_in",
    )(x, w_in, w_in)


def _mm_res_ksplit_kernel(x_ref, w_ref, r_ref, o_ref, acc_ref):
    k = pl.program_id(2)
    part = jnp.dot(x_ref[...], w_ref[...], preferred_element_type=F32)

    @pl.when(k == 0)
    def _():
        acc_ref[...] = r_ref[...] + part

    @pl.when(k == 1)
    def _():
        o_ref[...] = acc_ref[...] + part


def matmul_res_ksplit(x, w, layer, residual):
    T, K = x.shape
    N = w.shape[2]
    tm = _m_tile(T)
    tn = _pick(N, (512, 256, 128))
    nk = 2
    tk = K // nk
    assert tk * nk == K and tk % LANES == 0
    nbytes = 2 * (tm * tk * 2 + tk * tn * 2 + 2 * tm * tn * 4) + 2 * tm * tn * 4
    return pl.pallas_call(
        _mm_res_ksplit_kernel,
        grid=(T // tm, N // tn, nk),
        in_specs=[pl.BlockSpec((tm, tk), lambda i, j, k: (i, k)),
                  pl.BlockSpec((None, tk, tn), lambda i, j, k: (layer, k, j)),
                  pl.BlockSpec((tm, tn), lambda i, j, k: (i, j))],
        out_specs=pl.BlockSpec((tm, tn), lambda i, j, k: (i, j)),
        out_shape=jax.ShapeDtypeStruct((T, N), F32),
        scratch_shapes=[pltpu.VMEM((tm, tn), F32)],
        compiler_params=_params(("parallel", "arbitrary", "arbitrary"), nbytes),
        name="ffn_out",
    )(x, w, residual)


class Units:
    def __init__(self, n_prompt, chunks_per_prompt, n_sample):
        self.ncp = chunks_per_prompt
        self.up = n_prompt * chunks_per_prompt
        self.n_prompt = n_prompt
        self.n_sample = n_sample
        self.total = self.up + n_sample
        self.n_streams = n_prompt + n_sample

    def is_sample(self, u):
        return u >= self.up

    def stream(self, u):
        return jnp.where(u < self.up, u // self.ncp, self.n_prompt + u - self.up)

    def sample_index(self, u):
        return jnp.maximum(u - self.up, 0)

    def first(self, u):
        return jnp.logical_or(u >= self.up, u % self.ncp == 0)

    def last(self, u):
        return jnp.logical_or(u >= self.up, u % self.ncp == self.ncp - 1)


def _mlstm_kernel(bi_ref, bf_ref, m0_ref, q_ref, k_ref, v_ref, og_ref, gate_ref, gh_ref, c0_ref, n0_ref,
                  hm_ref, cout_ref, nout_ref, mout_ref, c_s, n_s, m_s, *, units, n_heads):
    u = pl.program_id(0)
    is_sample = units.is_sample(u)
    L = CHUNK
    H = n_heads
    dqk = c_s.shape[1]
    dv = c_s.shape[2]

    @pl.when(jnp.logical_and(units.first(u), jnp.logical_not(is_sample)))
    def _():
        c_s[...] = jnp.zeros_like(c_s)
        n_s[...] = jnp.zeros_like(n_s)
        m_s[...] = jnp.zeros_like(m_s)

    @pl.when(is_sample)
    def _():
        c_s[...] = c0_ref[...]
        n_s[...] = n0_ref[...]
        for hh in range(H):
            m_s[hh] = jnp.full(m_s.shape[1:], m0_ref[units.sample_index(u), hh], F32)

    gates = gate_ref[...]
    lane = lax.broadcasted_iota(jnp.int32, gates.shape, 1)
    row = lax.broadcasted_iota(jnp.int32, (L, L), 0)
    col = lax.broadcasted_iota(jnp.int32, (L, L), 1)
    eye = row == col
    tril = col <= row
    triu = row <= col

    for hh in range(H):
        ig_col = jnp.sum(jnp.where(lane == hh, gates, 0.0), axis=1, keepdims=True) + bi_ref[hh]
        f_col = jnp.sum(jnp.where(lane == H + hh, gates, 0.0), axis=1, keepdims=True) + bf_ref[hh]
        lf_col = _log_sigmoid(f_col)
        ig_row = jnp.sum(jnp.where(eye, ig_col, 0.0), axis=0, keepdims=True)
        lf_row = jnp.sum(jnp.where(eye, lf_col, 0.0), axis=0, keepdims=True)
        b_col = jnp.sum(jnp.where(tril, lf_row, 0.0), axis=1, keepdims=True)
        b_row = jnp.sum(jnp.where(triu, lf_col, 0.0), axis=0, keepdims=True)

        d = jnp.where(tril, b_col - b_row + ig_row, NEG_INF)
        m_prev = m_s[hh][:, :1]
        inter = b_col + m_prev
        m_t = jnp.maximum(inter, jnp.max(d, axis=1, keepdims=True))
        w_intra = jnp.exp(d - m_t)
        w_inter = jnp.exp(inter - m_t)

        q = q_ref[:, hh * dqk:(hh + 1) * dqk].astype(F32)
        k = k_ref[:, hh * dqk:(hh + 1) * dqk].astype(F32) * (dqk ** -0.5)
        vb = v_ref[:, hh * dv:(hh + 1) * dv].astype(BF16)
        qb = q_ref[:, hh * dqk:(hh + 1) * dqk].astype(BF16)
        c_prev = c_s[hh]
        n_prev = n_s[hh]
        a = lax.dot_general(qb, k.astype(BF16), NT, preferred_element_type=F32) * w_intra
        num = (jnp.dot(a.astype(BF16), vb, preferred_element_type=F32)
               + jnp.dot(qb, c_prev.astype(BF16), preferred_element_type=F32) * w_inter)
        den = jnp.sum(a, axis=1, keepdims=True) + jnp.sum(q * n_prev, axis=1, keepdims=True) * w_inter
        den = jnp.maximum(jnp.abs(den), jnp.exp(-m_t))
        hval = num / den

        y = hval * lax.rsqrt(jnp.mean(hval * hval, axis=1, keepdims=True) + EPS)
        gate_out = _sigmoid(og_ref[:, hh * dv:(hh + 1) * dv].astype(F32))
        hm_ref[:, hh * dv:(hh + 1) * dv] = (y * gh_ref[:, hh * dv:(hh + 1) * dv] * gate_out).astype(hm_ref.dtype)

        m_new = m_t[L - 1:L, :]
        b_last = b_col[L - 1:L, :]
        w_last = jnp.exp(b_last - b_col + ig_col - m_new)
        decay = jnp.exp(b_last + m_prev - m_new)
        k_w = k * w_last
        c_s[hh] = decay * c_prev + lax.dot_general(k_w.astype(BF16), vb, TN, preferred_element_type=F32)
        n_s[hh] = decay * n_prev + jnp.sum(k_w, axis=0, keepdims=True)
        m_s[hh] = jnp.broadcast_to(m_new, m_s.shape[1:])

    @pl.when(units.last(u))
    def _():
        cout_ref[...] = c_s[...]
        nout_ref[...] = n_s[...]
        mout_ref[...] = m_s[...]


def mlstm_mixer(z, gates, b_i, b_f, g_head, c0, n0, m0, units, n_heads, dqk, dv):
    T = z.shape[0]
    H = n_heads
    ns = units.n_streams
    qw = H * dqk
    vw = H * dv
    assert (2 * qw) % vw == 0
    v_blk = (2 * qw) // vw
    sidx = units.sample_index
    smem = pl.BlockSpec(memory_space=pltpu.SMEM)
    nbytes = (2 * (2 * CHUNK * qw * 4 + 2 * CHUNK * vw * 4 + CHUNK * LANES * 4 + CHUNK * vw * 2 + 2 * qw * dv * 4)
              + qw * dv * 4 * 4)
    hm, c_out, n_out, m_out = pl.pallas_call(
        functools.partial(_mlstm_kernel, units=units, n_heads=H),
        grid=(units.total,),
        in_specs=[smem, smem, smem,
                  pl.BlockSpec((CHUNK, qw), lambda u: (u, 0)),
                  pl.BlockSpec((CHUNK, qw), lambda u: (u, 1)),
                  pl.BlockSpec((CHUNK, vw), lambda u: (u, v_blk)),
                  pl.BlockSpec((CHUNK, vw), lambda u: (u, v_blk + 1)),
                  pl.BlockSpec((CHUNK, LANES), lambda u: (u, 0)),
                  pl.BlockSpec((1, vw), lambda u: (0, 0)),
                  pl.BlockSpec((None, H, dqk, dv), lambda u: (sidx(u), 0, 0, 0)),
                  pl.BlockSpec((None, H, 1, dqk), lambda u: (sidx(u), 0, 0, 0))],
        out_specs=[pl.BlockSpec((CHUNK, vw), lambda u: (u, 0)),
                   pl.BlockSpec((None, H, dqk, dv), lambda u: (units.stream(u), 0, 0, 0)),
                   pl.BlockSpec((None, H, 1, dqk), lambda u: (units.stream(u), 0, 0, 0)),
                   pl.BlockSpec((None, H, 1, LANES), lambda u: (units.stream(u), 0, 0, 0))],
        out_shape=[jax.ShapeDtypeStruct((T, vw), BF16),
                   jax.ShapeDtypeStruct((ns, H, dqk, dv), F32),
                   jax.ShapeDtypeStruct((ns, H, 1, dqk), F32),
                   jax.ShapeDtypeStruct((ns, H, 1, LANES), F32)],
        scratch_shapes=[pltpu.VMEM((H, dqk, dv), F32), pltpu.VMEM((H, 1, dqk), F32), pltpu.VMEM((H, 1, LANES), F32)],
        compiler_params=_params(("arbitrary",), nbytes),
        name="mlstm",
    )(b_i, b_f, m0, z, z, z, z, gates, g_head.reshape(1, vw), c0, n0.reshape(n0.shape[0], H, 1, dqk))
    return hm, c_out, n_out[:, :, 0, :], m_out[:, :, 0, 0]


def _swa_kernel(sink_ref, q_ref, k0_ref, k1_ref, k2_ref, v0_ref, v1_ref, v2_ref, ck1_ref, ck2_ref, cv1_ref, cv2_ref,
                o_ref, *, units, n_kv):
    u = pl.program_id(0)
    is_sample = units.is_sample(u)
    L = CHUNK
    W = 2 * L
    NK = W + L
    hd = SWA_HEAD_DIM
    pairs = SWA_GROUP // 2
    R = pairs * L

    def pick(c_ref, z_ref):
        return jnp.where(is_sample, c_ref[...], z_ref[...])

    k_all = jnp.concatenate([pick(ck2_ref, k2_ref), pick(ck1_ref, k1_ref), k0_ref[...]], axis=0)
    v_all = jnp.concatenate([pick(cv2_ref, v2_ref), pick(cv1_ref, v1_ref), v0_ref[...]], axis=0)

    n = u % units.ncp
    j_min = jnp.where(is_sample, 0, jnp.maximum(W - L * n, 0))
    jj = lax.broadcasted_iota(jnp.int32, (R, NK), 1)
    tt = lax.broadcasted_iota(jnp.int32, (R, NK), 0) % L
    dist = jnp.abs(tt + W - jj).astype(F32)
    valid = jj >= j_min
    pair_of_row = lax.broadcasted_iota(jnp.int32, (R, 1), 0) // L
    lane = lax.broadcasted_iota(jnp.int32, (NK, LANES), 1)
    lo_lanes = lane < hd
    n_heads = n_kv * SWA_GROUP

    for g in range(n_kv):
        tile = (g * hd) // LANES
        kt = k_all[:, tile * LANES:(tile + 1) * LANES]
        vt = v_all[:, tile * LANES:(tile + 1) * LANES]
        kr = pltpu.roll(kt, hd, axis=1)
        vr = pltpu.roll(vt, hd, axis=1)
        if (g * hd) % LANES == 0:
            k_lo, k_hi, v_lo, v_hi = kt, kr, vt, vr
        else:
            k_lo, k_hi, v_lo, v_hi = kr, kt, vr, vt
        k_lo = jnp.where(lo_lanes, k_lo, 0.0).astype(BF16)
        k_hi = jnp.where(lo_lanes, 0.0, k_hi).astype(BF16)
        v_lo = jnp.where(lo_lanes, v_lo, 0.0).astype(BF16)
        v_hi = jnp.where(lo_lanes, 0.0, v_hi).astype(BF16)
        qg = jnp.concatenate(
            [q_ref[:, (g * pairs + p) * LANES:(g * pairs + p + 1) * LANES] for p in range(pairs)], axis=0)
        qg = (qg * (hd ** -0.5)).astype(BF16)

        def probs(k_half, odd):
            s = lax.dot_general(qg, k_half, NT, preferred_element_type=F32)
            head = g * SWA_GROUP + 2 * pair_of_row + odd
            slope = jnp.exp2(-8.0 * (head + 1).astype(F32) / n_heads)
            sink = jnp.zeros((R, 1), F32)
            for p in range(pairs):
                sink = jnp.where(pair_of_row == p, sink_ref[g * SWA_GROUP + 2 * p + odd], sink)
            s = jnp.where(valid, s - slope * dist, NEG_INF)
            mx = jnp.maximum(jnp.max(s, axis=1, keepdims=True), sink)
            e = jnp.exp(s - mx)
            return e / (jnp.sum(e, axis=1, keepdims=True) + jnp.exp(sink - mx))

        o = (jnp.dot(probs(k_lo, 0).astype(BF16), v_lo, preferred_element_type=F32)
             + jnp.dot(probs(k_hi, 1).astype(BF16), v_hi, preferred_element_type=F32))
        for p in range(pairs):
            o_ref[:, (g * pairs + p) * LANES:(g * pairs + p + 1) * LANES] = o[p * L:(p + 1) * L].astype(o_ref.dtype)


def swa_mixer(z, sinks, cache_k, cache_v, units, n_kv):
    T = z.shape[0]
    qw = n_kv * SWA_GROUP * SWA_HEAD_DIM
    kw = n_kv * SWA_HEAD_DIM
    assert kw % LANES == 0 and qw % kw == 0
    kb = qw // kw
    vb = kb + 1
    sidx = units.sample_index

    def zspec(blk, back):
        return pl.BlockSpec((CHUNK, kw), lambda u: (jnp.maximum(u - back, 0), blk))

    def cspec(part):
        return pl.BlockSpec((None, CHUNK, kw), lambda u: (sidx(u), part, 0))

    nbytes = 2 * (CHUNK * qw * 4 + 10 * CHUNK * kw * 4 + CHUNK * qw * 2) + 16 * 4 * CHUNK * 3 * CHUNK * 4
    return pl.pallas_call(
        functools.partial(_swa_kernel, units=units, n_kv=n_kv),
        grid=(units.total,),
        in_specs=[pl.BlockSpec(memory_space=pltpu.SMEM),
                  pl.BlockSpec((CHUNK, qw), lambda u: (u, 0)),
                  zspec(kb, 0), zspec(kb, 1), zspec(kb, 2),
                  zspec(vb, 0), zspec(vb, 1), zspec(vb, 2),
                  cspec(1), cspec(0), cspec(1), cspec(0)],
        out_specs=pl.BlockSpec((CHUNK, qw), lambda u: (u, 0)),
        out_shape=jax.ShapeDtypeStruct((T, qw), BF16),
        compiler_params=_params(("arbitrary",), nbytes),
        name="swa",
    )(sinks, z, z, z, z, z, z, z, cache_k, cache_k, cache_v, cache_v)


def _gla_kernel(q_ref, k_ref, v_ref, r_ref, glr_ref, wg_ref, bg_ref, gh_ref, s0_ref, o_ref, sout_ref, s_s, *, units):
    u = pl.program_id(1)
    is_sample = units.is_sample(u)
    L = CHUNK
    hb, dv, dk = s_s.shape

    @pl.when(jnp.logical_and(units.first(u), jnp.logical_not(is_sample)))
    def _():
        s_s[...] = jnp.zeros_like(s_s)

    @pl.when(is_sample)
    def _():
        s_s[...] = s0_ref[...]

    glr = glr_ref[...].astype(BF16)
    row = lax.broadcasted_iota(jnp.int32, (L, L), 0)
    col = lax.broadcasted_iota(jnp.int32, (L, L), 1)
    tril = col <= row
    ones_tril = jnp.where(tril, 1.0, 0.0).astype(BF16)
    rr = lax.broadcasted_iota(jnp.int32, (L, dk), 0)
    n_sub = L // GLA_SUB

    for hh in range(hb):
        ks = slice(hh * dk, (hh + 1) * dk)
        vs = slice(hh * dv, (hh + 1) * dv)
        zg = jnp.dot(glr, wg_ref[:, ks], preferred_element_type=F32) + bg_ref[:, ks]
        lg = _log_sigmoid(zg) * (1.0 / GLA_TAU)
        bc = sum(jnp.dot(ones_tril, part, preferred_element_type=F32) for part in _split3(lg))

        q = q_ref[:, ks].astype(F32) * (dk ** -0.5)
        k = k_ref[:, ks].astype(F32)
        vb = v_ref[:, vs].astype(BF16)
        st = s_s[hh]

        refs = [bc[i * GLA_SUB:i * GLA_SUB + 1, :] for i in range(n_sub)]
        r_q = refs[n_sub - 1]
        for i in range(n_sub - 2, -1, -1):
            r_q = jnp.where(rr < (i + 1) * GLA_SUB, refs[i], r_q)
        q_t = (q * jnp.exp(bc - r_q)).astype(BF16)
        blocks = []
        for i in range(n_sub):
            e_i = jnp.where(rr < (i + 1) * GLA_SUB, refs[i] - bc, NEG_INF)
            k_i = (k * jnp.exp(e_i)).astype(BF16)
            blocks.append(lax.dot_general(q_t[i * GLA_SUB:(i + 1) * GLA_SUB], k_i, NT, preferred_element_type=F32))
        a = jnp.where(tril, jnp.concatenate(blocks, axis=0), 0.0)

        o = (jnp.dot(a.astype(BF16), vb, preferred_element_type=F32)
             + lax.dot_general((q * jnp.exp(bc)).astype(BF16), st.astype(BF16), NT, preferred_element_type=F32))

        b_last = bc[L - 1:L, :]
        k_dec = (k * jnp.exp(b_last - bc)).astype(BF16)
        s_s[hh] = jnp.exp(b_last) * st + lax.dot_general(vb, k_dec, TN, preferred_element_type=F32)

        y = o * lax.rsqrt(jnp.mean(o * o, axis=1, keepdims=True) + EPS)
        r = r_ref[:, vs].astype(F32)
        o_ref[:, vs] = (y * gh_ref[:, vs] * (r * _sigmoid(r))).astype(o_ref.dtype)

    @pl.when(units.last(u))
    def _():
        sout_ref[...] = s_s[...]


def gla_mixer(z, glr, w_g2, b_g, g_head, s0t, units, n_heads, dk, dv):
    T = z.shape[0]
    H = n_heads
    hb = GLA_HEADS_PER_STEP if H % GLA_HEADS_PER_STEP == 0 else 1
    ng = H // hb
    ns = units.n_streams
    kw = hb * dk
    vw = hb * dv
    assert (2 * H * dk) % vw == 0
    v_blk0 = (2 * H * dk) // vw
    sidx = units.sample_index
    nbytes = (2 * (2 * CHUNK * kw * 4 + 2 * CHUNK * vw * 4 + CHUNK * LANES * 4 + LANES * kw * 2 + CHUNK * vw * 2
                   + 2 * kw * dv * 4) + kw * dv * 4 * 4)
    out, st = pl.pallas_call(
        functools.partial(_gla_kernel, units=units),
        grid=(ng, units.total),
        in_specs=[pl.BlockSpec((CHUNK, kw), lambda g, u: (u, g)),
                  pl.BlockSpec((CHUNK, kw), lambda g, u: (u, ng + g)),
                  pl.BlockSpec((CHUNK, vw), lambda g, u: (u, v_blk0 + g)),
                  pl.BlockSpec((CHUNK, vw), lambda g, u: (u, v_blk0 + ng + g)),
                  pl.BlockSpec((CHUNK, LANES), lambda g, u: (u, 0)),
                  pl.BlockSpec((LANES, kw), lambda g, u: (0, g)),
                  pl.BlockSpec((1, kw), lambda g, u: (0, g)),
                  pl.BlockSpec((1, vw), lambda g, u: (0, g)),
                  pl.BlockSpec((None, hb, dv, dk), lambda g, u: (sidx(u), g, 0, 0))],
        out_specs=[pl.BlockSpec((CHUNK, vw), lambda g, u: (u, g)),
                   pl.BlockSpec((None, hb, dv, dk), lambda g, u: (units.stream(u), g, 0, 0))],
        out_shape=[jax.ShapeDtypeStruct((T, H * dv), BF16),
                   jax.ShapeDtypeStruct((ns, H, dv, dk), F32)],
        scratch_shapes=[pltpu.VMEM((hb, dv, dk), F32)],
        compiler_params=_params(("parallel", "arbitrary"), nbytes),
        name="gla",
    )(z, z, z, z, glr, w_g2, b_g.reshape(1, H * dk), g_head.reshape(1, H * dv), s0t)
    return out, st


def _pad_rows(w, height):
    return jnp.pad(w, ((0, height - w.shape[0]), (0, 0)))


def kernel(x_prompt, x_sample, cache_swa_k, cache_swa_v, state_mlstm_C, state_mlstm_n, state_mlstm_m, state_gla_S,
           norm_mix, norm_ffn, norm_final, w_even_in, b_mlstm_i, b_mlstm_f, mlstm_head_norm, swa_sinks, w_even_out,
           w_odd_in, w_gla_gate, b_gla_gate, gla_head_norm, w_odd_out, w_ffn_in, w_ffn_out):
    B, S, D = x_prompt.shape
    DB, DS, _ = x_sample.shape
    depth = norm_mix.shape[0]
    assert DS == CHUNK and S % CHUNK == 0
    MH, dqk, dv = state_mlstm_C.shape[2:]
    n_kv = cache_swa_k.shape[3]
    w_buf = cache_swa_k.shape[2]
    assert w_buf == 2 * CHUNK and cache_swa_k.shape[4] == SWA_HEAD_DIM
    GH, gdk, gdv = state_gla_S.shape[2:]
    rank = w_gla_gate.shape[1]
    m_w = 2 * MH * dqk + 2 * MH * dv
    sq_w = n_kv * SWA_GROUP * SWA_HEAD_DIM
    kvw = n_kv * SWA_HEAD_DIM
    s_w = sq_w + 2 * kvw
    g_w = 2 * GH * gdk + 2 * GH * gdv
    units = Units(B, S // CHUNK, DB)
    TP = B * S

    x = jnp.concatenate([x_prompt.reshape(TP, D), x_sample.reshape(DB * DS, D)], axis=0)
    w_even_in_t = jnp.swapaxes(w_even_in, 1, 2)
    w_odd_in_t = jnp.swapaxes(w_odd_in, 1, 2)
    w_ffn_out_b = w_ffn_out.astype(BF16)

    p_k, p_v, p_C, p_n, p_m, p_S = [], [], [], [], [], []
    s_k, s_v, s_C, s_n, s_m, s_S = [], [], [], [], [], []
    for l in range(depth):
        e = l // 2
        h = rmsnorm(x, norm_mix[l], BF16)
        if l % 2 == 0:
            w_gate = _pad_rows(w_even_in_t[e, m_w:m_w + 2 * MH], LANES)[None]
            w_attn = w_even_in_t[e, m_w + 2 * MH:][None]
            zm = project([h], w_even_in_t, e, m_w, BF16, w_nk=True, name="even_in_mlstm")
            zs = project([h], w_attn, 0, s_w, w_nk=True, name="even_in_attn")
            gates = project([h], w_gate, 0, LANES, w_nk=True, name="even_gates")
            hm, c_out, n_out, m_out = mlstm_mixer(
                zm, gates, b_mlstm_i[e], b_mlstm_f[e], mlstm_head_norm[e],
                state_mlstm_C[e], state_mlstm_n[e], state_mlstm_m[e], units, MH, dqk, dv)
            ck = cache_swa_k[e].reshape(DB, w_buf, kvw)
            cv = cache_swa_v[e].reshape(DB, w_buf, kvw)
            hs = swa_mixer(zs, swa_sinks[e], ck, cv, units, n_kv)
            x = project([hm, hs], w_even_out, e, D, residual=x, name="even_out")
            zk = zs[:, sq_w:sq_w + kvw]
            zv = zs[:, sq_w + kvw:]
            kv_shape = (w_buf, n_kv, SWA_HEAD_DIM)
            p_k.append(zk[:TP].reshape(B, S, kvw)[:, -w_buf:].reshape((B,) + kv_shape))
            p_v.append(zv[:TP].reshape(B, S, kvw)[:, -w_buf:].reshape((B,) + kv_shape))
            s_k.append(jnp.concatenate([ck[:, CHUNK:], zk[TP:].reshape(DB, DS, kvw)], axis=1).reshape((DB,) + kv_shape))
            s_v.append(jnp.concatenate([cv[:, CHUNK:], zv[TP:].reshape(DB, DS, kvw)], axis=1).reshape((DB,) + kv_shape))
            p_C.append(c_out[:B]); p_n.append(n_out[:B]); p_m.append(m_out[:B])
            s_C.append(c_out[B:]); s_n.append(n_out[B:]); s_m.append(m_out[B:])
        else:
            z = project([h], w_odd_in_t, e, g_w, BF16, w_nk=True, name="odd_in")
            w_rank = _pad_rows(w_odd_in_t[e, g_w:], LANES)[None]
            glr = project([h], w_rank, 0, LANES, w_nk=True, name="odd_gate_rank")
            w_g2 = jnp.pad(w_gla_gate[e], ((0, LANES - rank), (0, 0))).astype(BF16)
            mix, st = gla_mixer(z, glr, w_g2, b_gla_gate[e], gla_head_norm[e],
                                jnp.swapaxes(state_gla_S[e], -1, -2), units, GH, gdk, gdv)
            x = project([mix], w_odd_out, e, D, residual=x, name="odd_out")
            s_full = jnp.swapaxes(st, -1, -2)
            p_S.append(s_full[:B]); s_S.append(s_full[B:])
        hf = rmsnorm(x, norm_ffn[l], BF16)
        act = ffn_in(hf, w_ffn_in, l)
        x = matmul_res_ksplit(act, w_ffn_out_b, l, x)
    y_prompt = rmsnorm(x, norm_final, F32, 0, TP)
    y_sample = rmsnorm(x, norm_final, F32, TP, DB * DS)
    return (y_prompt.reshape(B, S, D), y_sample.reshape(DB, DS, D),
            jnp.stack(p_k), jnp.stack(p_v), jnp.stack(p_C), jnp.stack(p_n), jnp.stack(p_m), jnp.stack(p_S),
            jnp.stack(s_k), jnp.stack(s_v), jnp.stack(s_C), jnp.stack(s_n), jnp.stack(s_m), jnp.stack(s_S))
```

```python
import functools

import jax
import jax.numpy as jnp
from jax import lax
from jax.experimental import pallas as pl
from jax.experimental.pallas import tpu as pltpu

F32 = jnp.float32
BF16 = jnp.bfloat16

CHUNK = 64
SWA_HEAD_DIM = 64
SWA_GROUP = 8
LANES = 128
GLA_SUB = 16
GLA_HEADS_PER_STEP = 8
GLA_TAU = 16.0
LOG2E = 1.4426950408889634
EPS = 1e-6
NEG_INF = float("-inf")

V7X_VMEM_BYTES = 64 * 1024 * 1024
VMEM_CAP_BYTES = V7X_VMEM_BYTES - 6 * 1024 * 1024

NT = (((1,), (1,)), ((), ()))
TN = (((0,), (0,)), ((), ()))


def _pick(n, cands):
    for c in cands:
        if n % c == 0:
            return c
    raise ValueError(f"no tile in {cands} divides {n}")


def _params(semantics, block_bytes):
    limit = min(VMEM_CAP_BYTES, block_bytes + 16 * 1024 * 1024)
    return pltpu.CompilerParams(dimension_semantics=semantics, vmem_limit_bytes=int(limit))


def _log_sigmoid(x):
    return jnp.minimum(x, 0.0) - jnp.log(1.0 + jnp.exp(-jnp.abs(x)))


def _sigmoid(x):
    return 1.0 / (1.0 + jnp.exp(-x))


def _split3(x):
    hi = x.astype(BF16)
    r1 = x - hi.astype(F32)
    mid = r1.astype(BF16)
    lo = (r1 - mid.astype(F32)).astype(BF16)
    return hi, mid, lo


def _rmsnorm_kernel(x_ref, g_ref, o_ref):
    x = x_ref[...]
    y = x * lax.rsqrt(jnp.mean(x * x, axis=-1, keepdims=True) + EPS)
    o_ref[...] = (y * g_ref[...]).astype(o_ref.dtype)


def rmsnorm(x, g, layer, out_dtype, row0=0, rows=None):
    D = x.shape[1]
    T = x.shape[0] if rows is None else rows
    tr = _pick(T, (272, 256, 160, 128, 64))
    assert row0 % tr == 0
    blk0 = row0 // tr
    nbytes = 2 * tr * D * (4 + jnp.dtype(out_dtype).itemsize)
    return pl.pallas_call(
        _rmsnorm_kernel,
        grid=(T // tr,),
        in_specs=[pl.BlockSpec((tr, D), lambda i: (blk0 + i, 0)), pl.BlockSpec((None, 1, D), lambda i: (layer, 0, 0))],
        out_specs=pl.BlockSpec((tr, D), lambda i: (i, 0)),
        out_shape=jax.ShapeDtypeStruct((T, D), out_dtype),
        compiler_params=_params(("parallel",), nbytes),
        name="rmsnorm",
    )(x, g[:, None, :])


def _m_tile(T):
    return _pick(T, (1088, 1024, 640, 512, 320, 256, 128, 64))


def _proj_kernel(*refs, n_x, has_res, w_nk):
    x_refs = refs[:n_x]
    w_ref = refs[n_x]
    r_ref = refs[n_x + 1] if has_res else None
    o_ref = refs[n_x + 1 + has_res]
    k0 = 0
    acc = None
    for x_ref in x_refs:
        kx = x_ref.shape[1]
        if w_nk:
            part = lax.dot_general(x_ref[...], w_ref[:, k0:k0 + kx].astype(BF16), NT, preferred_element_type=F32)
        else:
            part = jnp.dot(x_ref[...], w_ref[k0:k0 + kx, :].astype(BF16), preferred_element_type=F32)
        acc = part if acc is None else acc + part
        k0 += kx
    if has_res:
        acc = r_ref[...] + acc
    o_ref[...] = acc.astype(o_ref.dtype)


def project(xs, w, layer, n_cols, out_dtype=F32, residual=None, w_nk=False, wide_k=False, name="project"):
    T = xs[0].shape[0]
    K = w.shape[2] if w_nk else w.shape[1]
    assert sum(x.shape[1] for x in xs) == K
    tm = _m_tile(T)
    tn = _pick(n_cols, (256, 128) if wide_k else (512, 256, 128))
    osz = jnp.dtype(out_dtype).itemsize
    x_bufs = 1 if wide_k else 2
    nbytes = x_bufs * tm * K * 2 + 2 * (K * tn * 4 + tm * tn * osz) + K * tn * 2 + tm * tn * 4
    x_mode = dict(pipeline_mode=pl.Buffered(1)) if wide_k else {}
    in_specs = [pl.BlockSpec((tm, x.shape[1]), lambda i, j: (i, 0), **x_mode) for x in xs]
    if w_nk:
        in_specs.append(pl.BlockSpec((None, tn, K), lambda i, j: (layer, j, 0)))
    else:
        in_specs.append(pl.BlockSpec((None, K, tn), lambda i, j: (layer, 0, j)))
    args = list(xs) + [w]
    if residual is not None:
        in_specs.append(pl.BlockSpec((tm, tn), lambda i, j: (i, j)))
        args.append(residual)
        nbytes += 2 * tm * tn * 4
    return pl.pallas_call(
        functools.partial(_proj_kernel, n_x=len(xs), has_res=residual is not None, w_nk=w_nk),
        grid=(T // tm, n_cols // tn),
        in_specs=in_specs,
        out_specs=pl.BlockSpec((tm, tn), lambda i, j: (i, j)),
        out_shape=jax.ShapeDtypeStruct((T, n_cols), out_dtype),
        compiler_params=_params(("parallel", "arbitrary"), nbytes),
        name=name,
    )(*args)


def _ffn_in_kernel(x_ref, wg_ref, wu_ref, o_ref):
    x = x_ref[...]
    g = jnp.dot(x, wg_ref[...].astype(BF16), preferred_element_type=F32)
    u = jnp.dot(x, wu_ref[...].astype(BF16), preferred_element_type=F32)
    o_ref[...] = (g * _sigmoid(g) * u).astype(o_ref.dtype)


def ffn_in(x, w_in, layer):
    T, K = x.shape
    F = w_in.shape[2] // 2
    tm = _pick(T, (2176, 1088, 640, 512, 256, 128, 64))
    tn = _pick(F, (256, 128))
    nf = F // tn
    nbytes = 2 * (tm * K * 2 + 2 * K * tn * 4 + tm * tn * 2) + 2 * K * tn * 2 + 3 * tm * tn * 4
    return pl.pallas_call(
        _ffn_in_kernel,
        grid=(T // tm, nf),
        in_specs=[pl.BlockSpec((tm, K), lambda i, j: (i, 0)),
                  pl.BlockSpec((None, K, tn), lambda i, j: (layer, 0, j)),
                  pl.BlockSpec((None, K, tn), lambda i, j: (layer, 0, j + nf))],
        out_specs=pl.BlockSpec((tm, tn), lambda i, j: (i, j)),
        out_shape=jax.ShapeDtypeStruct((T, F), BF16),
        compiler_params=_params(("parallel", "arbitrary"), nbytes),
        name="ffn_in",
    )(x, w_in, w_in)


class Units:
    def __init__(self, n_prompt, chunks_per_prompt, n_sample):
        self.ncp = chunks_per_prompt
        self.up = n_prompt * chunks_per_prompt
        self.n_prompt = n_prompt
        self.n_sample = n_sample
        self.total = self.up + n_sample
        self.n_streams = n_prompt + n_sample

    def is_sample(self, u):
        return u >= self.up

    def stream(self, u):
        return jnp.where(u < self.up, u // self.ncp, self.n_prompt + u - self.up)

    def sample_index(self, u):
        return jnp.maximum(u - self.up, 0)

    def first(self, u):
        return jnp.logical_or(u >= self.up, u % self.ncp == 0)

    def last(self, u):
        return jnp.logical_or(u >= self.up, u % self.ncp == self.ncp - 1)


def _mlstm_kernel(bi_ref, bf_ref, m0_ref, q_ref, k_ref, v_ref, og_ref, gate_ref, gh_ref, c0_ref, n0_ref,
                  hm_ref, cp_ref, np_ref, mp_ref, cs_ref, ns_ref, ms_ref, c_s, n_s, m_s, *, units, n_heads, layer):
    u = pl.program_id(0)
    is_sample = units.is_sample(u)
    is_prompt = jnp.logical_not(is_sample)
    L = CHUNK
    H = n_heads
    dqk = c_s.shape[1]
    dv = c_s.shape[2]

    @pl.when(jnp.logical_and(units.first(u), is_prompt))
    def _():
        c_s[...] = jnp.zeros_like(c_s)
        n_s[...] = jnp.zeros_like(n_s)
        m_s[...] = jnp.zeros_like(m_s)

    @pl.when(is_sample)
    def _():
        c_s[...] = c0_ref[...]
        n_s[...] = n0_ref[...]
        for hh in range(H):
            m_s[hh] = jnp.full(m_s.shape[1:], m0_ref[layer * units.n_sample + units.sample_index(u), hh], F32)

    gates = gate_ref[...]
    lane = lax.broadcasted_iota(jnp.int32, gates.shape, 1)
    row = lax.broadcasted_iota(jnp.int32, (L, L), 0)
    col = lax.broadcasted_iota(jnp.int32, (L, L), 1)
    eye = row == col
    tril = col <= row
    triu = row <= col

    for hh in range(H):
        ig_col = jnp.sum(jnp.where(lane == hh, gates, 0.0), axis=1, keepdims=True) + bi_ref[layer, hh]
        f_col = jnp.sum(jnp.where(lane == H + hh, gates, 0.0), axis=1, keepdims=True) + bf_ref[layer, hh]
        lf_col = _log_sigmoid(f_col)
        ig_row = jnp.sum(jnp.where(eye, ig_col, 0.0), axis=0, keepdims=True)
        lf_row = jnp.sum(jnp.where(eye, lf_col, 0.0), axis=0, keepdims=True)
        b_col = jnp.sum(jnp.where(tril, lf_row, 0.0), axis=1, keepdims=True)
        b_row = jnp.sum(jnp.where(triu, lf_col, 0.0), axis=0, keepdims=True)

        d = jnp.where(tril, b_col - b_row + ig_row, NEG_INF)
        m_prev = m_s[hh][:, :1]
        inter = b_col + m_prev
        m_t = jnp.maximum(inter, jnp.max(d, axis=1, keepdims=True))
        w_intra = jnp.exp(d - m_t)
        w_inter = jnp.exp(inter - m_t)

        q = q_ref[:, hh * dqk:(hh + 1) * dqk].astype(F32)
        k = k_ref[:, hh * dqk:(hh + 1) * dqk].astype(F32) * (dqk ** -0.5)
        vb = v_ref[:, hh * dv:(hh + 1) * dv].astype(BF16)
        qb = q_ref[:, hh * dqk:(hh + 1) * dqk].astype(BF16)
        c_prev = c_s[hh]
        n_prev = n_s[hh]
        a = lax.dot_general(qb, k.astype(BF16), NT, preferred_element_type=F32) * w_intra
        num = (jnp.dot(a.astype(BF16), vb, preferred_element_type=F32)
               + jnp.dot(qb, c_prev.astype(BF16), preferred_element_type=F32) * w_inter)
        den = jnp.sum(a, axis=1, keepdims=True) + jnp.sum(q * n_prev, axis=1, keepdims=True) * w_inter
        den = jnp.maximum(jnp.abs(den), jnp.exp(-m_t))
        hval = num / den

        y = hval * lax.rsqrt(jnp.mean(hval * hval, axis=1, keepdims=True) + EPS)
        gate_out = _sigmoid(og_ref[:, hh * dv:(hh + 1) * dv].astype(F32))
        hm_ref[:, hh * dv:(hh + 1) * dv] = (y * gh_ref[:, hh * dv:(hh + 1) * dv] * gate_out).astype(hm_ref.dtype)

        m_new = m_t[L - 1:L, :]
        b_last = b_col[L - 1:L, :]
        w_last = jnp.exp(b_last - b_col + ig_col - m_new)
        decay = jnp.exp(b_last + m_prev - m_new)
        k_w = k * w_last
        c_s[hh] = decay * c_prev + lax.dot_general(k_w.astype(BF16), vb, TN, preferred_element_type=F32)
        n_s[hh] = decay * n_prev + jnp.sum(k_w, axis=0, keepdims=True)
        m_s[hh] = jnp.broadcast_to(m_new, m_s.shape[1:])

    @pl.when(jnp.logical_and(units.last(u), is_prompt))
    def _():
        cp_ref[...] = c_s[...]
        np_ref[...] = n_s[...]
        mp_ref[...] = m_s[...]

    @pl.when(is_sample)
    def _():
        cs_ref[...] = c_s[...]
        ns_ref[...] = n_s[...]
        ms_ref[...] = m_s[...]


def mlstm_mixer(z, gates, b_i, b_f, g_head, c0, n0, m0, layer, units):
    T = z.shape[0]
    n_layers, DB, H, dqk, dv = c0.shape
    B = units.n_prompt
    qw = H * dqk
    vw = H * dv
    assert (2 * qw) % vw == 0
    v_blk = (2 * qw) // vw
    sidx = units.sample_index

    def pidx(u):
        return jnp.minimum(units.stream(u), B - 1)

    smem = pl.BlockSpec(memory_space=pltpu.SMEM)
    nbytes = (2 * (2 * CHUNK * qw * 2 + 2 * CHUNK * vw * 2 + CHUNK * LANES * 4 + CHUNK * vw * 2 + 3 * qw * dv * 4)
              + qw * dv * 4 * 4)

    def state_specs(idx):
        return [pl.BlockSpec((None, H, dqk, dv), lambda u: (idx(u), 0, 0, 0)),
                pl.BlockSpec((None, H, 1, dqk), lambda u: (idx(u), 0, 0, 0)),
                pl.BlockSpec((None, H, 1, LANES), lambda u: (idx(u), 0, 0, 0))]

    def state_shapes(n):
        return [jax.ShapeDtypeStruct((n, H, dqk, dv), F32), jax.ShapeDtypeStruct((n, H, 1, dqk), F32),
                jax.ShapeDtypeStruct((n, H, 1, LANES), F32)]

    hm, cp, np_, mp, cs, ns_, ms = pl.pallas_call(
        functools.partial(_mlstm_kernel, units=units, n_heads=H, layer=layer),
        grid=(units.total,),
        in_specs=[smem, smem, smem,
                  pl.BlockSpec((CHUNK, qw), lambda u: (u, 0)),
                  pl.BlockSpec((CHUNK, qw), lambda u: (u, 1)),
                  pl.BlockSpec((CHUNK, vw), lambda u: (u, v_blk)),
                  pl.BlockSpec((CHUNK, vw), lambda u: (u, v_blk + 1)),
                  pl.BlockSpec((CHUNK, LANES), lambda u: (u, 0)),
                  pl.BlockSpec((None, 1, vw), lambda u: (layer, 0, 0)),
                  pl.BlockSpec((None, None, H, dqk, dv), lambda u: (layer, sidx(u), 0, 0, 0)),
                  pl.BlockSpec((None, None, H, 1, dqk), lambda u: (layer, sidx(u), 0, 0, 0))],
        out_specs=[pl.BlockSpec((CHUNK, vw), lambda u: (u, 0))] + state_specs(pidx) + state_specs(sidx),
        out_shape=[jax.ShapeDtypeStruct((T, vw), BF16)] + state_shapes(B) + state_shapes(DB),
        scratch_shapes=[pltpu.VMEM((H, dqk, dv), F32), pltpu.VMEM((H, 1, dqk), F32), pltpu.VMEM((H, 1, LANES), F32)],
        compiler_params=_params(("arbitrary",), nbytes),
        name="mlstm",
    )(b_i, b_f, m0.reshape(n_layers * DB, H), z, z, z, z, gates, g_head[:, None, :], c0,
      n0.reshape(n_layers, DB, H, 1, dqk))
    return hm, (cp, np_[:, :, 0, :], mp[:, :, 0, 0]), (cs, ns_[:, :, 0, :], ms[:, :, 0, 0])


def _swa_kernel(sink_ref, q_ref, k0_ref, k1_ref, k2_ref, v0_ref, v1_ref, v2_ref, ck1_ref, ck2_ref, cv1_ref, cv2_ref,
                o_ref, bias_s, *, units, n_kv, layer):
    u = pl.program_id(0)
    is_sample = units.is_sample(u)
    L = CHUNK
    W = 2 * L
    NK = W + L
    hd = SWA_HEAD_DIM
    pairs = SWA_GROUP // 2
    R = pairs * L
    n_heads = n_kv * SWA_GROUP
    pair_of_row = lax.broadcasted_iota(jnp.int32, (R, 1), 0) // L

    @pl.when(u == 0)
    def _():
        jj = lax.broadcasted_iota(jnp.int32, (R, NK), 1)
        tt = lax.broadcasted_iota(jnp.int32, (R, NK), 0) % L
        dist = jnp.abs(tt + W - jj).astype(F32)
        for g in range(n_kv):
            for odd in range(2):
                head = g * SWA_GROUP + 2 * pair_of_row + odd
                slope = jnp.exp2(-8.0 * (head + 1).astype(F32) / n_heads) * LOG2E
                pen = slope * dist
                for v in range(3):
                    bias_s[v, 2 * g + odd] = jnp.where(jj >= W - v * L, pen, -NEG_INF)

    def pick(c_ref, z_ref):
        return jnp.where(is_sample, c_ref[...], z_ref[...])

    k_all = jnp.concatenate([pick(ck2_ref, k2_ref), pick(ck1_ref, k1_ref), k0_ref[...]], axis=0)
    v_all = jnp.concatenate([pick(cv2_ref, v2_ref), pick(cv1_ref, v1_ref), v0_ref[...]], axis=0)
    variant = jnp.where(is_sample, 2, jnp.minimum(u % units.ncp, 2))
    lane = lax.broadcasted_iota(jnp.int32, (NK, LANES), 1)
    lo_lanes = lane < hd

    for g in range(n_kv):
        tile = (g * hd) // LANES
        kt = k_all[:, tile * LANES:(tile + 1) * LANES]
        vt = v_all[:, tile * LANES:(tile + 1) * LANES]
        kr = pltpu.roll(kt, hd, axis=1)
        vr = pltpu.roll(vt, hd, axis=1)
        if (g * hd) % LANES == 0:
            k_lo, k_hi, v_lo, v_hi = kt, kr, vt, vr
        else:
            k_lo, k_hi, v_lo, v_hi = kr, kt, vr, vt
        k_lo = jnp.where(lo_lanes, k_lo, 0.0).astype(BF16)
        k_hi = jnp.where(lo_lanes, 0.0, k_hi).astype(BF16)
        v_lo = jnp.where(lo_lanes, v_lo, 0.0).astype(BF16)
        v_hi = jnp.where(lo_lanes, 0.0, v_hi).astype(BF16)
        qg = jnp.concatenate(
            [q_ref[:, (g * pairs + p) * LANES:(g * pairs + p + 1) * LANES] for p in range(pairs)], axis=0)
        qg = (qg * (hd ** -0.5 * LOG2E)).astype(BF16)

        def attend(k_half, v_half, odd):
            s = lax.dot_general(qg, k_half, NT, preferred_element_type=F32) - bias_s[variant, 2 * g + odd]
            sink = jnp.zeros((R, 1), F32)
            for p in range(pairs):
                sink = jnp.where(pair_of_row == p, sink_ref[layer, g * SWA_GROUP + 2 * p + odd] * LOG2E, sink)
            mx = jnp.maximum(jnp.max(s, axis=1, keepdims=True), sink)
            e = jnp.exp2(s - mx)
            den = jnp.sum(e, axis=1, keepdims=True) + jnp.exp2(sink - mx)
            return jnp.dot(e.astype(BF16), v_half, preferred_element_type=F32) * (1.0 / den)

        o = attend(k_lo, v_lo, 0) + attend(k_hi, v_hi, 1)
        for p in range(pairs):
            o_ref[:, (g * pairs + p) * LANES:(g * pairs + p + 1) * LANES] = o[p * L:(p + 1) * L].astype(o_ref.dtype)


def swa_mixer(z, sinks, cache_k, cache_v, layer, units, n_kv):
    T = z.shape[0]
    qw = n_kv * SWA_GROUP * SWA_HEAD_DIM
    kw = n_kv * SWA_HEAD_DIM
    assert kw % LANES == 0 and qw % kw == 0
    kb = qw // kw
    vb = kb + 1
    sidx = units.sample_index
    R = (SWA_GROUP // 2) * CHUNK
    NK = 3 * CHUNK

    def zspec(blk, back):
        return pl.BlockSpec((CHUNK, kw), lambda u: (jnp.maximum(u - back, 0), blk))

    def cspec(part):
        return pl.BlockSpec((None, CHUNK, kw), lambda u: (sidx(u), part, 0))

    bias_bytes = 3 * 2 * n_kv * R * 2 * LANES * 4
    nbytes = 2 * (CHUNK * qw * 4 + 10 * CHUNK * kw * 4 + CHUNK * qw * 2) + bias_bytes + 16 * R * NK * 4
    return pl.pallas_call(
        functools.partial(_swa_kernel, units=units, n_kv=n_kv, layer=layer),
        grid=(units.total,),
        in_specs=[pl.BlockSpec(memory_space=pltpu.SMEM),
                  pl.BlockSpec((CHUNK, qw), lambda u: (u, 0)),
                  zspec(kb, 0), zspec(kb, 1), zspec(kb, 2),
                  zspec(vb, 0), zspec(vb, 1), zspec(vb, 2),
                  cspec(1), cspec(0), cspec(1), cspec(0)],
        out_specs=pl.BlockSpec((CHUNK, qw), lambda u: (u, 0)),
        out_shape=jax.ShapeDtypeStruct((T, qw), BF16),
        scratch_shapes=[pltpu.VMEM((3, 2 * n_kv, R, NK), F32)],
        compiler_params=_params(("arbitrary",), nbytes),
        name="swa",
    )(sinks, z, z, z, z, z, z, z, cache_k, cache_k, cache_v, cache_v)


def _gla_kernel(q_ref, k_ref, v_ref, r_ref, glr_ref, wg_ref, bg_ref, gh_ref, s0_ref, o_ref, sp_ref, ss_ref, s_s, *,
                units):
    u = pl.program_id(1)
    is_sample = units.is_sample(u)
    is_prompt = jnp.logical_not(is_sample)
    L = CHUNK
    hb, dv, dk = s_s.shape

    @pl.when(jnp.logical_and(units.first(u), is_prompt))
    def _():
        s_s[...] = jnp.zeros_like(s_s)

    @pl.when(is_sample)
    def _():
        for hh in range(hb):
            s_s[hh] = s0_ref[hh].T

    glr = glr_ref[...].astype(BF16)
    row = lax.broadcasted_iota(jnp.int32, (L, L), 0)
    col = lax.broadcasted_iota(jnp.int32, (L, L), 1)
    tril = col <= row
    ones_tril = jnp.where(tril, 1.0, 0.0).astype(BF16)
    rr = lax.broadcasted_iota(jnp.int32, (L, dk), 0)
    n_sub = L // GLA_SUB

    for hh in range(hb):
        ks = slice(hh * dk, (hh + 1) * dk)
        vs = slice(hh * dv, (hh + 1) * dv)
        zg = jnp.dot(glr, wg_ref[:, ks], preferred_element_type=F32) + bg_ref[:, ks]
        lg = _log_sigmoid(zg) * (1.0 / GLA_TAU)
        bc = sum(jnp.dot(ones_tril, part, preferred_element_type=F32) for part in _split3(lg))

        q = q_ref[:, ks].astype(F32) * (dk ** -0.5)
        k = k_ref[:, ks].astype(F32)
        vb = v_ref[:, vs].astype(BF16)
        st = s_s[hh]

        refs = [bc[i * GLA_SUB:i * GLA_SUB + 1, :] for i in range(n_sub)]
        r_q = refs[n_sub - 1]
        for i in range(n_sub - 2, -1, -1):
            r_q = jnp.where(rr < (i + 1) * GLA_SUB, refs[i], r_q)
        q_t = (q * jnp.exp(bc - r_q)).astype(BF16)
        blocks = []
        for i in range(n_sub):
            e_i = jnp.where(rr < (i + 1) * GLA_SUB, refs[i] - bc, NEG_INF)
            k_i = (k * jnp.exp(e_i)).astype(BF16)
            blocks.append(lax.dot_general(q_t[i * GLA_SUB:(i + 1) * GLA_SUB], k_i, NT, preferred_element_type=F32))
        a = jnp.where(tril, jnp.concatenate(blocks, axis=0), 0.0)

        o = (jnp.dot(a.astype(BF16), vb, preferred_element_type=F32)
             + lax.dot_general((q * jnp.exp(bc)).astype(BF16), st.astype(BF16), NT, preferred_element_type=F32))

        b_last = bc[L - 1:L, :]
        k_dec = (k * jnp.exp(b_last - bc)).astype(BF16)
        s_s[hh] = jnp.exp(b_last) * st + lax.dot_general(vb, k_dec, TN, preferred_element_type=F32)

        y = o * lax.rsqrt(jnp.mean(o * o, axis=1, keepdims=True) + EPS)
        r = r_ref[:, vs].astype(F32)
        o_ref[:, vs] = (y * gh_ref[:, vs] * (r * _sigmoid(r))).astype(o_ref.dtype)

    @pl.when(jnp.logical_and(units.last(u), is_prompt))
    def _():
        for hh in range(hb):
            sp_ref[hh] = s_s[hh].T

    @pl.when(is_sample)
    def _():
        for hh in range(hb):
            ss_ref[hh] = s_s[hh].T


def gla_mixer(z, glr, w_g2, b_g, g_head, s0, layer, units):
    T = z.shape[0]
    _, DB, H, dk, dv = s0.shape
    B = units.n_prompt
    hb = GLA_HEADS_PER_STEP if H % GLA_HEADS_PER_STEP == 0 else 1
    ng = H // hb
    kw = hb * dk
    vw = hb * dv
    assert (2 * H * dk) % vw == 0
    v_blk0 = (2 * H * dk) // vw
    sidx = units.sample_index

    def pidx(u):
        return jnp.minimum(units.stream(u), B - 1)

    nbytes = (2 * (2 * CHUNK * kw * 2 + 2 * CHUNK * vw * 2 + CHUNK * LANES * 4 + LANES * kw * 2 + CHUNK * vw * 2
                   + 3 * kw * dv * 4) + kw * dv * 4 * 4)
    return pl.pallas_call(
        functools.partial(_gla_kernel, units=units),
        grid=(ng, units.total),
        in_specs=[pl.BlockSpec((CHUNK, kw), lambda g, u: (u, g)),
                  pl.BlockSpec((CHUNK, kw), lambda g, u: (u, ng + g)),
                  pl.BlockSpec((CHUNK, vw), lambda g, u: (u, v_blk0 + g)),
                  pl.BlockSpec((CHUNK, vw), lambda g, u: (u, v_blk0 + ng + g)),
                  pl.BlockSpec((CHUNK, LANES), lambda g, u: (u, 0)),
                  pl.BlockSpec((LANES, kw), lambda g, u: (0, g)),
                  pl.BlockSpec((None, 1, kw), lambda g, u: (layer, 0, g)),
                  pl.BlockSpec((None, 1, vw), lambda g, u: (layer, 0, g)),
                  pl.BlockSpec((None, None, hb, dk, dv), lambda g, u: (layer, sidx(u), g, 0, 0))],
        out_specs=[pl.BlockSpec((CHUNK, vw), lambda g, u: (u, g)),
                   pl.BlockSpec((None, hb, dk, dv), lambda g, u: (pidx(u), g, 0, 0)),
                   pl.BlockSpec((None, hb, dk, dv), lambda g, u: (sidx(u), g, 0, 0))],
        out_shape=[jax.ShapeDtypeStruct((T, H * dv), BF16),
                   jax.ShapeDtypeStruct((B, H, dk, dv), F32),
                   jax.ShapeDtypeStruct((DB, H, dk, dv), F32)],
        scratch_shapes=[pltpu.VMEM((hb, dv, dk), F32)],
        compiler_params=_params(("parallel", "arbitrary"), nbytes),
        name="gla",
    )(z, z, z, z, glr, w_g2, b_g[:, None, :], g_head[:, None, :], s0)


def _pad_rows(w, height):
    return jnp.pad(w, ((0, height - w.shape[0]), (0, 0)))


def kernel(x_prompt, x_sample, cache_swa_k, cache_swa_v, state_mlstm_C, state_mlstm_n, state_mlstm_m, state_gla_S,
           norm_mix, norm_ffn, norm_final, w_even_in, b_mlstm_i, b_mlstm_f, mlstm_head_norm, swa_sinks, w_even_out,
           w_odd_in, w_gla_gate, b_gla_gate, gla_head_norm, w_odd_out, w_ffn_in, w_ffn_out):
    B, S, D = x_prompt.shape
    DB, DS, _ = x_sample.shape
    depth = norm_mix.shape[0]
    assert DS == CHUNK and S % CHUNK == 0
    MH, dqk, dv = state_mlstm_C.shape[2:]
    n_kv = cache_swa_k.shape[3]
    w_buf = cache_swa_k.shape[2]
    assert w_buf == 2 * CHUNK and cache_swa_k.shape[4] == SWA_HEAD_DIM
    GH, gdk, gdv = state_gla_S.shape[2:]
    rank = w_gla_gate.shape[1]
    m_w = 2 * MH * dqk + 2 * MH * dv
    sq_w = n_kv * SWA_GROUP * SWA_HEAD_DIM
    kvw = n_kv * SWA_HEAD_DIM
    s_w = sq_w + 2 * kvw
    g_w = 2 * GH * gdk + 2 * GH * gdv
    units = Units(B, S // CHUNK, DB)
    TP = B * S

    x = jnp.concatenate([x_prompt.reshape(TP, D), x_sample.reshape(DB * DS, D)], axis=0)
    w_even_in_t = jnp.swapaxes(w_even_in, 1, 2)
    w_odd_in_t = jnp.swapaxes(w_odd_in, 1, 2)

    p_k, p_v, p_C, p_n, p_m, p_S = [], [], [], [], [], []
    s_k, s_v, s_C, s_n, s_m, s_S = [], [], [], [], [], []
    for l in range(depth):
        e = l // 2
        h = rmsnorm(x, norm_mix, l, BF16)
        if l % 2 == 0:
            w_gate = _pad_rows(w_even_in_t[e, m_w:m_w + 2 * MH], LANES)[None]
            w_attn = w_even_in_t[e, m_w + 2 * MH:][None]
            zm = project([h], w_even_in_t, e, m_w, BF16, w_nk=True, name="even_in_mlstm")
            zs = project([h], w_attn, 0, s_w, w_nk=True, name="even_in_attn")
            gates = project([h], w_gate, 0, LANES, w_nk=True, name="even_gates")
            hm, (pc, pn, pm), (sc, sn, sm) = mlstm_mixer(
                zm, gates, b_mlstm_i, b_mlstm_f, mlstm_head_norm, state_mlstm_C, state_mlstm_n, state_mlstm_m,
                e, units)
            ck = cache_swa_k[e].reshape(DB, w_buf, kvw)
            cv = cache_swa_v[e].reshape(DB, w_buf, kvw)
            hs = swa_mixer(zs, swa_sinks, ck, cv, e, units, n_kv)
            x = project([hm, hs], w_even_out, e, D, residual=x, name="even_out")
            zk = zs[:, sq_w:sq_w + kvw]
            zv = zs[:, sq_w + kvw:]
            kv_shape = (w_buf, n_kv, SWA_HEAD_DIM)
            p_k.append(zk[:TP].reshape(B, S, kvw)[:, -w_buf:].reshape((B,) + kv_shape))
            p_v.append(zv[:TP].reshape(B, S, kvw)[:, -w_buf:].reshape((B,) + kv_shape))
            s_k.append(jnp.concatenate([ck[:, CHUNK:], zk[TP:].reshape(DB, DS, kvw)], axis=1).reshape((DB,) + kv_shape))
            s_v.append(jnp.concatenate([cv[:, CHUNK:], zv[TP:].reshape(DB, DS, kvw)], axis=1).reshape((DB,) + kv_shape))
            p_C.append(pc); p_n.append(pn); p_m.append(pm)
            s_C.append(sc); s_n.append(sn); s_m.append(sm)
        else:
            z = project([h], w_odd_in_t, e, g_w, BF16, w_nk=True, name="odd_in")
            w_rank = _pad_rows(w_odd_in_t[e, g_w:], LANES)[None]
            glr = project([h], w_rank, 0, LANES, w_nk=True, name="odd_gate_rank")
            w_g2 = jnp.pad(w_gla_gate[e], ((0, LANES - rank), (0, 0))).astype(BF16)
            mix, ps, ss = gla_mixer(z, glr, w_g2, b_gla_gate, gla_head_norm, state_gla_S, e, units)
            x = project([mix], w_odd_out, e, D, residual=x, name="odd_out")
            p_S.append(ps); s_S.append(ss)
        hf = rmsnorm(x, norm_ffn, l, BF16)
        act = ffn_in(hf, w_ffn_in, l)
        x = project([act], w_ffn_out, l, D, residual=x, wide_k=True, name="ffn_out")
    y_prompt = rmsnorm(x, norm_final[None], 0, F32, 0, TP)
    y_sample = rmsnorm(x, norm_final[None], 0, F32, TP, DB * DS)
    return (y_prompt.reshape(B, S, D), y_sample.reshape(DB, DS, D),
            jnp.stack(p_k), jnp.stack(p_v), jnp.stack(p_C), jnp.stack(p_n), jnp.stack(p_m), jnp.stack(p_S),
            jnp.stack(s_k), jnp.stack(s_v), jnp.stack(s_C), jnp.stack(s_n), jnp.stack(s_m), jnp.stack(s_S))
```

```python
import functools

import jax
import jax.numpy as jnp
from jax import lax
from jax.experimental import pallas as pl
from jax.experimental.pallas import tpu as pltpu

F32 = jnp.float32
BF16 = jnp.bfloat16

CHUNK = 64
SWA_HEAD_DIM = 64
SWA_GROUP = 8
LANES = 128
GLA_SUB = 16
GLA_HEADS_PER_STEP = 8
GLA_TAU = 16.0
LOG2E = 1.4426950408889634
EPS = 1e-6
NEG_INF = float("-inf")

V7X_VMEM_BYTES = 64 * 1024 * 1024
VMEM_CAP_BYTES = V7X_VMEM_BYTES - 6 * 1024 * 1024

NT = (((1,), (1,)), ((), ()))
TN = (((0,), (0,)), ((), ()))


def _pick(n, cands):
    for c in cands:
        if n % c == 0:
            return c
    raise ValueError(f"no tile in {cands} divides {n}")


def _params(semantics, block_bytes):
    limit = min(VMEM_CAP_BYTES, block_bytes + 16 * 1024 * 1024)
    return pltpu.CompilerParams(dimension_semantics=semantics, vmem_limit_bytes=int(limit))


def _log_sigmoid(x):
    return jnp.minimum(x, 0.0) - jnp.log(1.0 + jnp.exp(-jnp.abs(x)))


def _sigmoid(x):
    return 1.0 / (1.0 + jnp.exp(-x))


def _split3(x):
    hi = x.astype(BF16)
    r1 = x - hi.astype(F32)
    mid = r1.astype(BF16)
    lo = (r1 - mid.astype(F32)).astype(BF16)
    return hi, mid, lo


def _rmsnorm_kernel(x_ref, g_ref, o_ref):
    x = x_ref[...]
    y = x * lax.rsqrt(jnp.mean(x * x, axis=-1, keepdims=True) + EPS)
    o_ref[...] = (y * g_ref[...]).astype(o_ref.dtype)


def rmsnorm(x, g, layer, out_dtype, row0=0, rows=None):
    D = x.shape[1]
    T = x.shape[0] if rows is None else rows
    tr = _pick(T, (272, 256, 160, 128, 64))
    assert row0 % tr == 0
    blk0 = row0 // tr
    nbytes = 2 * tr * D * (4 + jnp.dtype(out_dtype).itemsize)
    return pl.pallas_call(
        _rmsnorm_kernel,
        grid=(T // tr,),
        in_specs=[pl.BlockSpec((tr, D), lambda i: (blk0 + i, 0)), pl.BlockSpec((None, 1, D), lambda i: (layer, 0, 0))],
        out_specs=pl.BlockSpec((tr, D), lambda i: (i, 0)),
        out_shape=jax.ShapeDtypeStruct((T, D), out_dtype),
        compiler_params=_params(("parallel",), nbytes),
        name="rmsnorm",
    )(x, g[:, None, :])


def _m_tile(T):
    return _pick(T, (1088, 1024, 640, 512, 320, 256, 128, 64))


def _proj_kernel(*refs, n_x, has_res, w_nk):
    x_refs = refs[:n_x]
    w_ref = refs[n_x]
    r_ref = refs[n_x + 1] if has_res else None
    o_ref = refs[n_x + 1 + has_res]
    k0 = 0
    acc = None
    for x_ref in x_refs:
        kx = x_ref.shape[1]
        if w_nk:
            part = lax.dot_general(x_ref[...], w_ref[:, k0:k0 + kx].astype(BF16), NT, preferred_element_type=F32)
        else:
            part = jnp.dot(x_ref[...], w_ref[k0:k0 + kx, :].astype(BF16), preferred_element_type=F32)
        acc = part if acc is None else acc + part
        k0 += kx
    if has_res:
        acc = r_ref[...] + acc
    o_ref[...] = acc.astype(o_ref.dtype)


def project(xs, w, layer, n_cols, out_dtype=F32, residual=None, w_nk=False, wide_k=False, name="project"):
    T = xs[0].shape[0]
    K = w.shape[2] if w_nk else w.shape[1]
    assert sum(x.shape[1] for x in xs) == K
    tm = _m_tile(T)
    tn = _pick(n_cols, (256, 128) if wide_k else (512, 256, 128))
    osz = jnp.dtype(out_dtype).itemsize
    x_bufs = 1 if wide_k else 2
    nbytes = x_bufs * tm * K * 2 + 2 * (K * tn * 4 + tm * tn * osz) + K * tn * 2 + tm * tn * 4
    x_mode = dict(pipeline_mode=pl.Buffered(1)) if wide_k else {}
    in_specs = [pl.BlockSpec((tm, x.shape[1]), lambda i, j: (i, 0), **x_mode) for x in xs]
    if w_nk:
        in_specs.append(pl.BlockSpec((None, tn, K), lambda i, j: (layer, j, 0)))
    else:
        in_specs.append(pl.BlockSpec((None, K, tn), lambda i, j: (layer, 0, j)))
    args = list(xs) + [w]
    if residual is not None:
        in_specs.append(pl.BlockSpec((tm, tn), lambda i, j: (i, j)))
        args.append(residual)
        nbytes += 2 * tm * tn * 4
    return pl.pallas_call(
        functools.partial(_proj_kernel, n_x=len(xs), has_res=residual is not None, w_nk=w_nk),
        grid=(T // tm, n_cols // tn),
        in_specs=in_specs,
        out_specs=pl.BlockSpec((tm, tn), lambda i, j: (i, j)),
        out_shape=jax.ShapeDtypeStruct((T, n_cols), out_dtype),
        compiler_params=_params(("parallel", "arbitrary"), nbytes),
        name=name,
    )(*args)


def _ffn_in_kernel(x_ref, wg_ref, wu_ref, o_ref):
    x = x_ref[...]
    g = jnp.dot(x, wg_ref[...].astype(BF16), preferred_element_type=F32)
    u = jnp.dot(x, wu_ref[...].astype(BF16), preferred_element_type=F32)
    o_ref[...] = (g * _sigmoid(g) * u).astype(o_ref.dtype)


def ffn_in(x, w_in, layer):
    T, K = x.shape
    F = w_in.shape[2] // 2
    tm = _pick(T, (2176, 1088, 640, 512, 256, 128, 64))
    tn = _pick(F, (256, 128))
    nf = F // tn
    nbytes = 2 * (tm * K * 2 + 2 * K * tn * 4 + tm * tn * 2) + 2 * K * tn * 2 + 3 * tm * tn * 4
    return pl.pallas_call(
        _ffn_in_kernel,
        grid=(T // tm, nf),
        in_specs=[pl.BlockSpec((tm, K), lambda i, j: (i, 0)),
                  pl.BlockSpec((None, K, tn), lambda i, j: (layer, 0, j)),
                  pl.BlockSpec((None, K, tn), lambda i, j: (layer, 0, j + nf))],
        out_specs=pl.BlockSpec((tm, tn), lambda i, j: (i, j)),
        out_shape=jax.ShapeDtypeStruct((T, F), BF16),
        compiler_params=_params(("parallel", "arbitrary"), nbytes),
        name="ffn_in",
    )(x, w_in, w_in)


class Units:
    def __init__(self, n_prompt, chunks_per_prompt, n_sample):
        self.ncp = chunks_per_prompt
        self.up = n_prompt * chunks_per_prompt
        self.n_prompt = n_prompt
        self.n_sample = n_sample
        self.total = self.up + n_sample
        self.n_streams = n_prompt + n_sample

    def is_sample(self, u):
        return u >= self.up

    def stream(self, u):
        return jnp.where(u < self.up, u // self.ncp, self.n_prompt + u - self.up)

    def sample_index(self, u):
        return jnp.maximum(u - self.up, 0)

    def first(self, u):
        return jnp.logical_or(u >= self.up, u % self.ncp == 0)

    def last(self, u):
        return jnp.logical_or(u >= self.up, u % self.ncp == self.ncp - 1)


def _mlstm_kernel(bi_ref, bf_ref, m0_ref, q_ref, k_ref, v_ref, og_ref, gate_ref, gh_ref, c0_ref, n0_ref,
                  hm_ref, cp_ref, np_ref, mp_ref, cs_ref, ns_ref, ms_ref, c_s, n_s, m_s, *, units, n_heads, layer):
    u = pl.program_id(0)
    is_sample = units.is_sample(u)
    is_prompt = jnp.logical_not(is_sample)
    L = CHUNK
    H = n_heads
    dqk = c_s.shape[1]
    dv = c_s.shape[2]

    @pl.when(jnp.logical_and(units.first(u), is_prompt))
    def _():
        c_s[...] = jnp.zeros_like(c_s)
        n_s[...] = jnp.zeros_like(n_s)
        m_s[...] = jnp.zeros_like(m_s)

    @pl.when(is_sample)
    def _():
        c_s[...] = c0_ref[...]
        n_s[...] = n0_ref[...]
        for hh in range(H):
            m_s[hh] = jnp.full(m_s.shape[1:], m0_ref[layer * units.n_sample + units.sample_index(u), hh], F32)

    gates = gate_ref[...]
    lane = lax.broadcasted_iota(jnp.int32, gates.shape, 1)
    row = lax.broadcasted_iota(jnp.int32, (L, L), 0)
    col = lax.broadcasted_iota(jnp.int32, (L, L), 1)
    eye = row == col
    tril = col <= row
    triu = row <= col

    heads = range(H)
    qs = [slice(hh * dqk, (hh + 1) * dqk) for hh in heads]
    vs = [slice(hh * dv, (hh + 1) * dv) for hh in heads]

    ig_col, b_col, m_prev, m_t, w_intra, w_inter = [], [], [], [], [], []
    for hh in heads:
        ig_c = jnp.sum(jnp.where(lane == hh, gates, 0.0), axis=1, keepdims=True) + bi_ref[layer, hh]
        f_c = jnp.sum(jnp.where(lane == H + hh, gates, 0.0), axis=1, keepdims=True) + bf_ref[layer, hh]
        lf_c = _log_sigmoid(f_c)
        ig_row = jnp.sum(jnp.where(eye, ig_c, 0.0), axis=0, keepdims=True)
        lf_row = jnp.sum(jnp.where(eye, lf_c, 0.0), axis=0, keepdims=True)
        b_c = jnp.sum(jnp.where(tril, lf_row, 0.0), axis=1, keepdims=True)
        b_row = jnp.sum(jnp.where(triu, lf_c, 0.0), axis=0, keepdims=True)
        d = jnp.where(tril, b_c - b_row + ig_row, NEG_INF)
        m_p = m_s[hh][:, :1]
        inter = b_c + m_p
        m_c = jnp.maximum(inter, jnp.max(d, axis=1, keepdims=True))
        ig_col.append(ig_c); b_col.append(b_c); m_prev.append(m_p); m_t.append(m_c)
        w_intra.append(jnp.exp(d - m_c))
        w_inter.append(jnp.exp(inter - m_c))

    a = [lax.dot_general(q_ref[:, qs[hh]].astype(BF16),
                         (k_ref[:, qs[hh]].astype(F32) * (dqk ** -0.5)).astype(BF16), NT,
                         preferred_element_type=F32) * w_intra[hh] for hh in heads]

    hval = []
    for hh in heads:
        q = q_ref[:, qs[hh]].astype(F32)
        num = (jnp.dot(a[hh].astype(BF16), v_ref[:, vs[hh]].astype(BF16), preferred_element_type=F32)
               + jnp.dot(q_ref[:, qs[hh]].astype(BF16), c_s[hh].astype(BF16), preferred_element_type=F32)
               * w_inter[hh])
        den = (jnp.sum(a[hh], axis=1, keepdims=True)
               + jnp.sum(q * n_s[hh], axis=1, keepdims=True) * w_inter[hh])
        den = jnp.maximum(jnp.abs(den), jnp.exp(-m_t[hh]))
        hval.append(num / den)

    for hh in heads:
        y = hval[hh] * lax.rsqrt(jnp.mean(hval[hh] * hval[hh], axis=1, keepdims=True) + EPS)
        gate_out = _sigmoid(og_ref[:, vs[hh]].astype(F32))
        hm_ref[:, vs[hh]] = (y * gh_ref[:, vs[hh]] * gate_out).astype(hm_ref.dtype)

    for hh in heads:
        k = k_ref[:, qs[hh]].astype(F32) * (dqk ** -0.5)
        m_new = m_t[hh][L - 1:L, :]
        b_last = b_col[hh][L - 1:L, :]
        w_last = jnp.exp(b_last - b_col[hh] + ig_col[hh] - m_new)
        decay = jnp.exp(b_last + m_prev[hh] - m_new)
        k_w = k * w_last
        c_s[hh] = decay * c_s[hh] + lax.dot_general(k_w.astype(BF16), v_ref[:, vs[hh]].astype(BF16), TN,
                                                    preferred_element_type=F32)
        n_s[hh] = decay * n_s[hh] + jnp.sum(k_w, axis=0, keepdims=True)
        m_s[hh] = jnp.broadcast_to(m_new, m_s.shape[1:])

    @pl.when(jnp.logical_and(units.last(u), is_prompt))
    def _():
        cp_ref[...] = c_s[...]
        np_ref[...] = n_s[...]
        mp_ref[...] = m_s[...]

    @pl.when(is_sample)
    def _():
        cs_ref[...] = c_s[...]
        ns_ref[...] = n_s[...]
        ms_ref[...] = m_s[...]


def mlstm_mixer(z, gates, b_i, b_f, g_head, c0, n0, m0, layer, units):
    T = z.shape[0]
    n_layers, DB, H, dqk, dv = c0.shape
    B = units.n_prompt
    qw = H * dqk
    vw = H * dv
    assert (2 * qw) % vw == 0
    v_blk = (2 * qw) // vw
    sidx = units.sample_index

    def pidx(u):
        return jnp.minimum(units.stream(u), B - 1)

    smem = pl.BlockSpec(memory_space=pltpu.SMEM)
    nbytes = (2 * (2 * CHUNK * qw * 2 + 2 * CHUNK * vw * 2 + CHUNK * LANES * 4 + CHUNK * vw * 2 + 3 * qw * dv * 4)
              + qw * dv * 4 * 4)

    def state_specs(idx):
        return [pl.BlockSpec((None, H, dqk, dv), lambda u: (idx(u), 0, 0, 0)),
                pl.BlockSpec((None, H, 1, dqk), lambda u: (idx(u), 0, 0, 0)),
                pl.BlockSpec((None, H, 1, LANES), lambda u: (idx(u), 0, 0, 0))]

    def state_shapes(n):
        return [jax.ShapeDtypeStruct((n, H, dqk, dv), F32), jax.ShapeDtypeStruct((n, H, 1, dqk), F32),
                jax.ShapeDtypeStruct((n, H, 1, LANES), F32)]

    hm, cp, np_, mp, cs, ns_, ms = pl.pallas_call(
        functools.partial(_mlstm_kernel, units=units, n_heads=H, layer=layer),
        grid=(units.total,),
        in_specs=[smem, smem, smem,
                  pl.BlockSpec((CHUNK, qw), lambda u: (u, 0)),
                  pl.BlockSpec((CHUNK, qw), lambda u: (u, 1)),
                  pl.BlockSpec((CHUNK, vw), lambda u: (u, v_blk)),
                  pl.BlockSpec((CHUNK, vw), lambda u: (u, v_blk + 1)),
                  pl.BlockSpec((CHUNK, LANES), lambda u: (u, 0)),
                  pl.BlockSpec((None, 1, vw), lambda u: (layer, 0, 0)),
                  pl.BlockSpec((None, None, H, dqk, dv), lambda u: (layer, sidx(u), 0, 0, 0)),
                  pl.BlockSpec((None, None, H, 1, dqk), lambda u: (layer, sidx(u), 0, 0, 0))],
        out_specs=[pl.BlockSpec((CHUNK, vw), lambda u: (u, 0))] + state_specs(pidx) + state_specs(sidx),
        out_shape=[jax.ShapeDtypeStruct((T, vw), BF16)] + state_shapes(B) + state_shapes(DB),
        scratch_shapes=[pltpu.VMEM((H, dqk, dv), F32), pltpu.VMEM((H, 1, dqk), F32), pltpu.VMEM((H, 1, LANES), F32)],
        compiler_params=_params(("arbitrary",), nbytes),
        name="mlstm",
    )(b_i, b_f, m0.reshape(n_layers * DB, H), z, z, z, z, gates, g_head[:, None, :], c0,
      n0.reshape(n_layers, DB, H, 1, dqk))
    return hm, (cp, np_[:, :, 0, :], mp[:, :, 0, 0]), (cs, ns_[:, :, 0, :], ms[:, :, 0, 0])


def _swa_kernel(sink_ref, q_ref, k0_ref, k1_ref, k2_ref, v0_ref, v1_ref, v2_ref, ck1_ref, ck2_ref, cv1_ref, cv2_ref,
                o_ref, bias_s, *, units, n_kv, layer):
    u = pl.program_id(0)
    is_sample = units.is_sample(u)
    L = CHUNK
    W = 2 * L
    NK = W + L
    hd = SWA_HEAD_DIM
    pairs = SWA_GROUP // 2
    R = pairs * L
    n_heads = n_kv * SWA_GROUP
    pair_of_row = lax.broadcasted_iota(jnp.int32, (R, 1), 0) // L

    @pl.when(u == 0)
    def _():
        jj = lax.broadcasted_iota(jnp.int32, (R, NK), 1)
        tt = lax.broadcasted_iota(jnp.int32, (R, NK), 0) % L
        dist = jnp.abs(tt + W - jj).astype(F32)
        for g in range(n_kv):
            for odd in range(2):
                head = g * SWA_GROUP + 2 * pair_of_row + odd
                slope = jnp.exp2(-8.0 * (head + 1).astype(F32) / n_heads) * LOG2E
                pen = slope * dist
                for v in range(3):
                    bias_s[v, 2 * g + odd] = jnp.where(jj >= W - v * L, pen, -NEG_INF)

    def pick(c_ref, z_ref):
        return jnp.where(is_sample, c_ref[...], z_ref[...])

    k_all = jnp.concatenate([pick(ck2_ref, k2_ref), pick(ck1_ref, k1_ref), k0_ref[...]], axis=0)
    v_all = jnp.concatenate([pick(cv2_ref, v2_ref), pick(cv1_ref, v1_ref), v0_ref[...]], axis=0)
    variant = jnp.where(is_sample, 2, jnp.minimum(u % units.ncp, 2))
    lane = lax.broadcasted_iota(jnp.int32, (NK, LANES), 1)
    lo_lanes = lane < hd

    for g in range(n_kv):
        tile = (g * hd) // LANES
        kt = k_all[:, tile * LANES:(tile + 1) * LANES]
        vt = v_all[:, tile * LANES:(tile + 1) * LANES]
        kr = pltpu.roll(kt, hd, axis=1)
        vr = pltpu.roll(vt, hd, axis=1)
        if (g * hd) % LANES == 0:
            k_lo, k_hi, v_lo, v_hi = kt, kr, vt, vr
        else:
            k_lo, k_hi, v_lo, v_hi = kr, kt, vr, vt
        k_lo = jnp.where(lo_lanes, k_lo, 0.0).astype(BF16)
        k_hi = jnp.where(lo_lanes, 0.0, k_hi).astype(BF16)
        v_lo = jnp.where(lo_lanes, v_lo, 0.0).astype(BF16)
        v_hi = jnp.where(lo_lanes, 0.0, v_hi).astype(BF16)
        qg = jnp.concatenate(
            [q_ref[:, (g * pairs + p) * LANES:(g * pairs + p + 1) * LANES] for p in range(pairs)], axis=0)
        qg = (qg * (hd ** -0.5 * LOG2E)).astype(BF16)

        def attend(k_half, v_half, odd):
            s = lax.dot_general(qg, k_half, NT, preferred_element_type=F32) - bias_s[variant, 2 * g + odd]
            sink = jnp.zeros((R, 1), F32)
            for p in range(pairs):
                sink = jnp.where(pair_of_row == p, sink_ref[layer, g * SWA_GROUP + 2 * p + odd] * LOG2E, sink)
            mx = jnp.maximum(jnp.max(s, axis=1, keepdims=True), sink)
            e = jnp.exp2(s - mx)
            den = jnp.sum(e, axis=1, keepdims=True) + jnp.exp2(sink - mx)
            return jnp.dot(e.astype(BF16), v_half, preferred_element_type=F32) * (1.0 / den)

        o = attend(k_lo, v_lo, 0) + attend(k_hi, v_hi, 1)
        for p in range(pairs):
            o_ref[:, (g * pairs + p) * LANES:(g * pairs + p + 1) * LANES] = o[p * L:(p + 1) * L].astype(o_ref.dtype)


def swa_mixer(z, sinks, cache_k, cache_v, layer, units, n_kv):
    T = z.shape[0]
    qw = n_kv * SWA_GROUP * SWA_HEAD_DIM
    kw = n_kv * SWA_HEAD_DIM
    assert kw % LANES == 0 and qw % kw == 0
    kb = qw // kw
    vb = kb + 1
    sidx = units.sample_index
    R = (SWA_GROUP // 2) * CHUNK
    NK = 3 * CHUNK

    def zspec(blk, back):
        return pl.BlockSpec((CHUNK, kw), lambda u: (jnp.maximum(u - back, 0), blk))

    def cspec(part):
        return pl.BlockSpec((None, CHUNK, kw), lambda u: (sidx(u), part, 0))

    bias_bytes = 3 * 2 * n_kv * R * 2 * LANES * 4
    nbytes = 2 * (CHUNK * qw * 4 + 10 * CHUNK * kw * 4 + CHUNK * qw * 2) + bias_bytes + 16 * R * NK * 4
    return pl.pallas_call(
        functools.partial(_swa_kernel, units=units, n_kv=n_kv, layer=layer),
        grid=(units.total,),
        in_specs=[pl.BlockSpec(memory_space=pltpu.SMEM),
                  pl.BlockSpec((CHUNK, qw), lambda u: (u, 0)),
                  zspec(kb, 0), zspec(kb, 1), zspec(kb, 2),
                  zspec(vb, 0), zspec(vb, 1), zspec(vb, 2),
                  cspec(1), cspec(0), cspec(1), cspec(0)],
        out_specs=pl.BlockSpec((CHUNK, qw), lambda u: (u, 0)),
        out_shape=jax.ShapeDtypeStruct((T, qw), BF16),
        scratch_shapes=[pltpu.VMEM((3, 2 * n_kv, R, NK), F32)],
        compiler_params=_params(("arbitrary",), nbytes),
        name="swa",
    )(sinks, z, z, z, z, z, z, z, cache_k, cache_k, cache_v, cache_v)


def _gla_kernel(q_ref, k_ref, v_ref, r_ref, glr_ref, wg_ref, bg_ref, gh_ref, s0_ref, o_ref, sp_ref, ss_ref, s_s, *,
                units):
    u = pl.program_id(1)
    is_sample = units.is_sample(u)
    is_prompt = jnp.logical_not(is_sample)
    L = CHUNK
    hb, dv, dk = s_s.shape

    @pl.when(jnp.logical_and(units.first(u), is_prompt))
    def _():
        s_s[...] = jnp.zeros_like(s_s)

    @pl.when(is_sample)
    def _():
        for hh in range(hb):
            s_s[hh] = s0_ref[hh].T

    glr = glr_ref[...].astype(BF16)
    row = lax.broadcasted_iota(jnp.int32, (L, L), 0)
    col = lax.broadcasted_iota(jnp.int32, (L, L), 1)
    tril = col <= row
    ones_tril = jnp.where(tril, 1.0, 0.0).astype(BF16)
    rr = lax.broadcasted_iota(jnp.int32, (L, dk), 0)
    n_sub = L // GLA_SUB

    heads = range(hb)
    ks = [slice(hh * dk, (hh + 1) * dk) for hh in heads]
    vs = [slice(hh * dv, (hh + 1) * dv) for hh in heads]
    lg = [_log_sigmoid(jnp.dot(glr, wg_ref[:, ks[hh]], preferred_element_type=F32) + bg_ref[:, ks[hh]])
          * (1.0 / GLA_TAU) for hh in heads]
    bc = [sum(jnp.dot(ones_tril, part, preferred_element_type=F32) for part in _split3(lg[hh])) for hh in heads]

    refs = [[bc[hh][i * GLA_SUB:i * GLA_SUB + 1, :] for i in range(n_sub)] for hh in heads]
    q_t = []
    for hh in heads:
        r_q = refs[hh][n_sub - 1]
        for i in range(n_sub - 2, -1, -1):
            r_q = jnp.where(rr < (i + 1) * GLA_SUB, refs[hh][i], r_q)
        q_t.append((q_ref[:, ks[hh]].astype(F32) * (dk ** -0.5) * jnp.exp(bc[hh] - r_q)).astype(BF16))
    blocks = [[] for _ in heads]
    for i in range(n_sub):
        for hh in heads:
            e_i = jnp.where(rr < (i + 1) * GLA_SUB, refs[hh][i] - bc[hh], NEG_INF)
            k_i = (k_ref[:, ks[hh]].astype(F32) * jnp.exp(e_i)).astype(BF16)
            blocks[hh].append(lax.dot_general(q_t[hh][i * GLA_SUB:(i + 1) * GLA_SUB], k_i, NT,
                                              preferred_element_type=F32))
    a = [jnp.where(tril, jnp.concatenate(blocks[hh], axis=0), 0.0).astype(BF16) for hh in heads]

    o = []
    for hh in heads:
        q_dec = (q_ref[:, ks[hh]].astype(F32) * (dk ** -0.5) * jnp.exp(bc[hh])).astype(BF16)
        o.append(jnp.dot(a[hh], v_ref[:, vs[hh]].astype(BF16), preferred_element_type=F32)
                 + lax.dot_general(q_dec, s_s[hh].astype(BF16), NT, preferred_element_type=F32))

    for hh in heads:
        b_last = bc[hh][L - 1:L, :]
        k_dec = (k_ref[:, ks[hh]].astype(F32) * jnp.exp(b_last - bc[hh])).astype(BF16)
        s_s[hh] = jnp.exp(b_last) * s_s[hh] + lax.dot_general(v_ref[:, vs[hh]].astype(BF16), k_dec, TN,
                                                               preferred_element_type=F32)

    for hh in heads:
        y = o[hh] * lax.rsqrt(jnp.mean(o[hh] * o[hh], axis=1, keepdims=True) + EPS)
        r = r_ref[:, vs[hh]].astype(F32)
        o_ref[:, vs[hh]] = (y * gh_ref[:, vs[hh]] * (r * _sigmoid(r))).astype(o_ref.dtype)

    @pl.when(jnp.logical_and(units.last(u), is_prompt))
    def _():
        for hh in range(hb):
            sp_ref[hh] = s_s[hh].T

    @pl.when(is_sample)
    def _():
        for hh in range(hb):
            ss_ref[hh] = s_s[hh].T


def gla_mixer(z, glr, w_g2, b_g, g_head, s0, layer, units):
    T = z.shape[0]
    _, DB, H, dk, dv = s0.shape
    B = units.n_prompt
    hb = GLA_HEADS_PER_STEP if H % GLA_HEADS_PER_STEP == 0 else 1
    ng = H // hb
    kw = hb * dk
    vw = hb * dv
    assert (2 * H * dk) % vw == 0
    v_blk0 = (2 * H * dk) // vw
    sidx = units.sample_index

    def pidx(u):
        return jnp.minimum(units.stream(u), B - 1)

    nbytes = (2 * (2 * CHUNK * kw * 2 + 2 * CHUNK * vw * 2 + CHUNK * LANES * 4 + LANES * kw * 2 + CHUNK * vw * 2
                   + 3 * kw * dv * 4) + kw * dv * 4 * 4)
    return pl.pallas_call(
        functools.partial(_gla_kernel, units=units),
        grid=(ng, units.total),
        in_specs=[pl.BlockSpec((CHUNK, kw), lambda g, u: (u, g)),
                  pl.BlockSpec((CHUNK, kw), lambda g, u: (u, ng + g)),
                  pl.BlockSpec((CHUNK, vw), lambda g, u: (u, v_blk0 + g)),
                  pl.BlockSpec((CHUNK, vw), lambda g, u: (u, v_blk0 + ng + g)),
                  pl.BlockSpec((CHUNK, LANES), lambda g, u: (u, 0)),
                  pl.BlockSpec((LANES, kw), lambda g, u: (0, g)),
                  pl.BlockSpec((None, 1, kw), lambda g, u: (layer, 0, g)),
                  pl.BlockSpec((None, 1, vw), lambda g, u: (layer, 0, g)),
                  pl.BlockSpec((None, None, hb, dk, dv), lambda g, u: (layer, sidx(u), g, 0, 0))],
        out_specs=[pl.BlockSpec((CHUNK, vw), lambda g, u: (u, g)),
                   pl.BlockSpec((None, hb, dk, dv), lambda g, u: (pidx(u), g, 0, 0)),
                   pl.BlockSpec((None, hb, dk, dv), lambda g, u: (sidx(u), g, 0, 0))],
        out_shape=[jax.ShapeDtypeStruct((T, H * dv), BF16),
                   jax.ShapeDtypeStruct((B, H, dk, dv), F32),
                   jax.ShapeDtypeStruct((DB, H, dk, dv), F32)],
        scratch_shapes=[pltpu.VMEM((hb, dv, dk), F32)],
        compiler_params=_params(("parallel", "arbitrary"), nbytes),
        name="gla",
    )(z, z, z, z, glr, w_g2, b_g[:, None, :], g_head[:, None, :], s0)


def _pad_rows(w, height):
    return jnp.pad(w, ((0, height - w.shape[0]), (0, 0)))


def kernel(x_prompt, x_sample, cache_swa_k, cache_swa_v, state_mlstm_C, state_mlstm_n, state_mlstm_m, state_gla_S,
           norm_mix, norm_ffn, norm_final, w_even_in, b_mlstm_i, b_mlstm_f, mlstm_head_norm, swa_sinks, w_even_out,
           w_odd_in, w_gla_gate, b_gla_gate, gla_head_norm, w_odd_out, w_ffn_in, w_ffn_out):
    B, S, D = x_prompt.shape
    DB, DS, _ = x_sample.shape
    depth = norm_mix.shape[0]
    assert DS == CHUNK and S % CHUNK == 0
    MH, dqk, dv = state_mlstm_C.shape[2:]
    n_kv = cache_swa_k.shape[3]
    w_buf = cache_swa_k.shape[2]
    assert w_buf == 2 * CHUNK and cache_swa_k.shape[4] == SWA_HEAD_DIM
    GH, gdk, gdv = state_gla_S.shape[2:]
    rank = w_gla_gate.shape[1]
    m_w = 2 * MH * dqk + 2 * MH * dv
    sq_w = n_kv * SWA_GROUP * SWA_HEAD_DIM
    kvw = n_kv * SWA_HEAD_DIM
    s_w = sq_w + 2 * kvw
    g_w = 2 * GH * gdk + 2 * GH * gdv
    units = Units(B, S // CHUNK, DB)
    TP = B * S

    x = jnp.concatenate([x_prompt.reshape(TP, D), x_sample.reshape(DB * DS, D)], axis=0)
    w_even_in_t = jnp.swapaxes(w_even_in, 1, 2)
    w_odd_in_t = jnp.swapaxes(w_odd_in, 1, 2)

    p_k, p_v, p_C, p_n, p_m, p_S = [], [], [], [], [], []
    s_k, s_v, s_C, s_n, s_m, s_S = [], [], [], [], [], []
    for l in range(depth):
        e = l // 2
        h = rmsnorm(x, norm_mix, l, BF16)
        if l % 2 == 0:
            w_gate = _pad_rows(w_even_in_t[e, m_w:m_w + 2 * MH], LANES)[None]
            w_attn = w_even_in_t[e, m_w + 2 * MH:][None]
            zm = project([h], w_even_in_t, e, m_w, BF16, w_nk=True, name="even_in_mlstm")
            zs = project([h], w_attn, 0, s_w, w_nk=True, name="even_in_attn")
            gates = project([h], w_gate, 0, LANES, w_nk=True, name="even_gates")
            hm, (pc, pn, pm), (sc, sn, sm) = mlstm_mixer(
                zm, gates, b_mlstm_i, b_mlstm_f, mlstm_head_norm, state_mlstm_C, state_mlstm_n, state_mlstm_m,
                e, units)
            ck = cache_swa_k[e].reshape(DB, w_buf, kvw)
            cv = cache_swa_v[e].reshape(DB, w_buf, kvw)
            hs = swa_mixer(zs, swa_sinks, ck, cv, e, units, n_kv)
            x = project([hm, hs], w_even_out, e, D, residual=x, name="even_out")
            zk = zs[:, sq_w:sq_w + kvw]
            zv = zs[:, sq_w + kvw:]
            kv_shape = (w_buf, n_kv, SWA_HEAD_DIM)
            p_k.append(zk[:TP].reshape(B, S, kvw)[:, -w_buf:].reshape((B,) + kv_shape))
            p_v.append(zv[:TP].reshape(B, S, kvw)[:, -w_buf:].reshape((B,) + kv_shape))
            s_k.append(jnp.concatenate([ck[:, CHUNK:], zk[TP:].reshape(DB, DS, kvw)], axis=1).reshape((DB,) + kv_shape))
            s_v.append(jnp.concatenate([cv[:, CHUNK:], zv[TP:].reshape(DB, DS, kvw)], axis=1).reshape((DB,) + kv_shape))
            p_C.append(pc); p_n.append(pn); p_m.append(pm)
            s_C.append(sc); s_n.append(sn); s_m.append(sm)
        else:
            z = project([h], w_odd_in_t, e, g_w, BF16, w_nk=True, name="odd_in")
            w_rank = _pad_rows(w_odd_in_t[e, g_w:], LANES)[None]
            glr = project([h], w_rank, 0, LANES, w_nk=True, name="odd_gate_rank")
            w_g2 = jnp.pad(w_gla_gate[e], ((0, LANES - rank), (0, 0))).astype(BF16)
            mix, ps, ss = gla_mixer(z, glr, w_g2, b_gla_gate, gla_head_norm, state_gla_S, e, units)
            x = project([mix], w_odd_out, e, D, residual=x, name="odd_out")
            p_S.append(ps); s_S.append(ss)
        hf = rmsnorm(x, norm_ffn, l, BF16)
        act = ffn_in(hf, w_ffn_in, l)
        x = project([act], w_ffn_out, l, D, residual=x, wide_k=True, name="ffn_out")
    y_prompt = rmsnorm(x, norm_final[None], 0, F32, 0, TP)
    y_sample = rmsnorm(x, norm_final[None], 0, F32, TP, DB * DS)
    return (y_prompt.reshape(B, S, D), y_sample.reshape(DB, DS, D),
            jnp.stack(p_k), jnp.stack(p_v), jnp.stack(p_C), jnp.stack(p_n), jnp.stack(p_m), jnp.stack(p_S),
            jnp.stack(s_k), jnp.stack(s_v), jnp.stack(s_C), jnp.stack(s_n), jnp.stack(s_m), jnp.stack(s_S))
```

```python
import functools

import jax
import jax.numpy as jnp
from jax import lax
from jax.experimental import pallas as pl
from jax.experimental.pallas import tpu as pltpu

F32 = jnp.float32
BF16 = jnp.bfloat16

CHUNK = 64
SWA_HEAD_DIM = 64
SWA_GROUP = 8
LANES = 128
GLA_SUB = 16
GLA_HEADS_PER_STEP = 8
GLA_TAU = 16.0
LOG2E = 1.4426950408889634
EPS = 1e-6
NEG_INF = float("-inf")

V7X_VMEM_BYTES = 64 * 1024 * 1024
VMEM_CAP_BYTES = V7X_VMEM_BYTES - 4 * 1024 * 1024

NT = (((1,), (1,)), ((), ()))
TN = (((0,), (0,)), ((), ()))


def _pick(n, cands):
    for c in cands:
        if n % c == 0:
            return c
    raise ValueError(f"no tile in {cands} divides {n}")


def _params(semantics, block_bytes):
    limit = min(VMEM_CAP_BYTES, block_bytes + 16 * 1024 * 1024)
    return pltpu.CompilerParams(dimension_semantics=semantics, vmem_limit_bytes=int(limit))


def _log_sigmoid(x):
    return jnp.minimum(x, 0.0) - jnp.log(1.0 + jnp.exp(-jnp.abs(x)))


def _sigmoid(x):
    return 1.0 / (1.0 + jnp.exp(-x))


def _split3(x):
    hi = x.astype(BF16)
    r1 = x - hi.astype(F32)
    mid = r1.astype(BF16)
    lo = (r1 - mid.astype(F32)).astype(BF16)
    return hi, mid, lo


def _rmsnorm_kernel(x_ref, g_ref, o_ref):
    x = x_ref[...]
    y = x * lax.rsqrt(jnp.mean(x * x, axis=-1, keepdims=True) + EPS)
    o_ref[...] = (y * g_ref[...]).astype(o_ref.dtype)


def rmsnorm(x, g, layer, out_dtype, row0=0, rows=None):
    D = x.shape[1]
    T = x.shape[0] if rows is None else rows
    tr = _pick(T, (512, 272, 256, 160, 128, 64))
    assert row0 % tr == 0
    blk0 = row0 // tr
    nbytes = 2 * tr * D * (4 + jnp.dtype(out_dtype).itemsize)
    return pl.pallas_call(
        _rmsnorm_kernel,
        grid=(T // tr,),
        in_specs=[pl.BlockSpec((tr, D), lambda i: (blk0 + i, 0)), pl.BlockSpec((None, 1, D), lambda i: (layer, 0, 0))],
        out_specs=pl.BlockSpec((tr, D), lambda i: (i, 0)),
        out_shape=jax.ShapeDtypeStruct((T, D), out_dtype),
        compiler_params=_params(("parallel",), nbytes),
        name="rmsnorm",
    )(x, g[:, None, :])


def _m_tile(T):
    return _pick(T, (1088, 1024, 640, 512, 320, 256, 128, 64))


def _proj_kernel(*refs, n_x, has_res, w_nk):
    x_refs = refs[:n_x]
    w_ref = refs[n_x]
    r_ref = refs[n_x + 1] if has_res else None
    o_ref = refs[n_x + 1 + has_res]
    k0 = 0
    acc = None
    for x_ref in x_refs:
        kx = x_ref.shape[1]
        if w_nk:
            part = lax.dot_general(x_ref[...], w_ref[:, k0:k0 + kx].astype(BF16), NT, preferred_element_type=F32)
        else:
            part = jnp.dot(x_ref[...], w_ref[k0:k0 + kx, :].astype(BF16), preferred_element_type=F32)
        acc = part if acc is None else acc + part
        k0 += kx
    if has_res:
        acc = r_ref[...] + acc
    o_ref[...] = acc.astype(o_ref.dtype)


def project(xs, w, layer, n_cols, out_dtype=F32, residual=None, w_nk=False, wide_k=False, name="project"):
    T = xs[0].shape[0]
    K = w.shape[2] if w_nk else w.shape[1]
    assert sum(x.shape[1] for x in xs) == K
    tm = _m_tile(T)
    tn = _pick(n_cols, (256, 128) if wide_k else (512, 256, 128))
    osz = jnp.dtype(out_dtype).itemsize
    x_bufs = 1 if wide_k else 2
    nbytes = x_bufs * tm * K * 2 + 2 * (K * tn * 4 + tm * tn * osz) + K * tn * 2 + tm * tn * 4
    x_mode = dict(pipeline_mode=pl.Buffered(1)) if wide_k else {}
    in_specs = [pl.BlockSpec((tm, x.shape[1]), lambda i, j: (i, 0), **x_mode) for x in xs]
    if w_nk:
        in_specs.append(pl.BlockSpec((None, tn, K), lambda i, j: (layer, j, 0)))
    else:
        in_specs.append(pl.BlockSpec((None, K, tn), lambda i, j: (layer, 0, j)))
    args = list(xs) + [w]
    if residual is not None:
        in_specs.append(pl.BlockSpec((tm, tn), lambda i, j: (i, j)))
        args.append(residual)
        nbytes += 2 * tm * tn * 4
    return pl.pallas_call(
        functools.partial(_proj_kernel, n_x=len(xs), has_res=residual is not None, w_nk=w_nk),
        grid=(T // tm, n_cols // tn),
        in_specs=in_specs,
        out_specs=pl.BlockSpec((tm, tn), lambda i, j: (i, j)),
        out_shape=jax.ShapeDtypeStruct((T, n_cols), out_dtype),
        compiler_params=_params(("parallel", "arbitrary"), nbytes),
        name=name,
    )(*args)


def _normed_kernel(*refs, n_slabs, slab_rows, n_w, w_nk, emit_h, swiglu):
    x_ref, g_ref = refs[:2]
    w_refs = refs[2:2 + n_w]
    o_ref = refs[2 + n_w]
    hout_ref = refs[3 + n_w] if emit_h else None
    h2_ref = refs[-1]
    t = pl.program_id(0)
    slab = jnp.where(t < pl.num_programs(0) - 1, jnp.minimum(pl.program_id(1), n_slabs - 1), n_slabs - 1)

    def norm_slab(buf):
        x = x_ref[...]
        y = x * lax.rsqrt(jnp.mean(x * x, axis=-1, keepdims=True) + EPS)
        yb = (y * g_ref[...]).astype(BF16)
        h2_ref[buf, pl.ds(pl.multiple_of(slab * slab_rows, slab_rows), slab_rows), :] = yb
        if emit_h:
            hout_ref[...] = yb

    def matmul(h, w_ref):
        if w_nk:
            return lax.dot_general(h, w_ref[...].astype(BF16), NT, preferred_element_type=F32)
        return jnp.dot(h, w_ref[...].astype(BF16), preferred_element_type=F32)

    @pl.when(t == 0)
    def _():
        norm_slab(0)

    @pl.when(t > 0)
    def _():
        h = h2_ref[(t - 1) % 2]
        if swiglu:
            g = matmul(h, w_refs[0])
            u = matmul(h, w_refs[1])
            o_ref[...] = (g * _sigmoid(g) * u).astype(o_ref.dtype)
        else:
            o_ref[...] = matmul(h, w_refs[0]).astype(o_ref.dtype)
        norm_slab(t % 2)


def _slab_count(tm, nj):
    units16 = tm // 16
    return max(d for d in range(1, units16 + 1) if units16 % d == 0 and d <= nj)


def normed_project(x, g, g_layer, w, layer, n_cols, out_dtype, w_nk=False, swiglu=False, emit_h=False,
                   name="normed_project"):
    T, K = x.shape
    tm = _pick(T, (2176, 1088, 640, 512, 256, 128, 64)) if swiglu else _m_tile(T)
    cands = [c for c in ((256, 128) if swiglu else (512, 256, 128)) if n_cols % c == 0]
    tn = next((c for c in cands if n_cols // c >= tm // CHUNK), cands[-1])
    nj = n_cols // tn
    n_tiles = T // tm
    ns = _slab_count(tm, nj)
    rows = tm // ns
    osz = jnp.dtype(out_dtype).itemsize
    n_w = 2 if swiglu else 1

    def row_tile(t):
        return jnp.maximum(t - 1, 0)

    def col(t, j):
        return jnp.where(t == 0, 0, j)

    def slab_idx(t, j):
        return jnp.where(t < n_tiles, t * ns + jnp.minimum(j, ns - 1), n_tiles * ns - 1)

    if w_nk:
        w_specs = [pl.BlockSpec((None, tn, K), lambda t, j, o=o: (layer, col(t, j) + o, 0)) for o in range(n_w)]
    else:
        w_specs = [pl.BlockSpec((None, K, tn), lambda t, j, o=o: (layer, 0, col(t, j) + o * nj)) for o in range(n_w)]
    out_specs = [pl.BlockSpec((tm, tn), lambda t, j: (row_tile(t), col(t, j)))]
    out_shape = [jax.ShapeDtypeStruct((T, n_cols), out_dtype)]
    if emit_h:
        out_specs.append(pl.BlockSpec((rows, K), lambda t, j: (slab_idx(t, j), 0)))
        out_shape.append(jax.ShapeDtypeStruct((T, K), BF16))
    nbytes = (2 * tm * K * 2 + 2 * rows * K * (4 + 2 * emit_h) + n_w * (2 * K * tn * 4 + K * tn * 2)
              + 2 * tm * tn * osz + (1 + n_w) * tm * tn * 4)
    outs = pl.pallas_call(
        functools.partial(_normed_kernel, n_slabs=ns, slab_rows=rows, n_w=n_w, w_nk=w_nk, emit_h=emit_h,
                          swiglu=swiglu),
        grid=(n_tiles + 1, nj),
        in_specs=[pl.BlockSpec((rows, K), lambda t, j: (slab_idx(t, j), 0)),
                  pl.BlockSpec((None, 1, K), lambda t, j: (g_layer, 0, 0))] + w_specs,
        out_specs=out_specs,
        out_shape=out_shape,
        scratch_shapes=[pltpu.VMEM((2, tm, K), BF16)],
        compiler_params=_params(("arbitrary", "arbitrary"), nbytes),
        name=name,
    )(x, g[:, None, :], *([w] * n_w))
    return outs if emit_h else outs[0]


class Units:
    def __init__(self, n_prompt, chunks_per_prompt, n_sample):
        self.ncp = chunks_per_prompt
        self.up = n_prompt * chunks_per_prompt
        self.n_prompt = n_prompt
        self.n_sample = n_sample
        self.total = self.up + n_sample
        self.n_streams = n_prompt + n_sample

    def is_sample(self, u):
        return u >= self.up

    def stream(self, u):
        return jnp.where(u < self.up, u // self.ncp, self.n_prompt + u - self.up)

    def sample_index(self, u):
        return jnp.maximum(u - self.up, 0)

    def first(self, u):
        return jnp.logical_or(u >= self.up, u % self.ncp == 0)

    def last(self, u):
        return jnp.logical_or(u >= self.up, u % self.ncp == self.ncp - 1)


def _mlstm_kernel(bi_ref, bf_ref, m0_ref, q_ref, k_ref, v_ref, og_ref, gate_ref, gh_ref, c0_ref, n0_ref,
                  hm_ref, cp_ref, np_ref, mp_ref, cs_ref, ns_ref, ms_ref, c_s, n_s, m_s, *, units, n_heads, layer):
    u = pl.program_id(0)
    is_sample = units.is_sample(u)
    is_prompt = jnp.logical_not(is_sample)
    L = CHUNK
    H = n_heads
    dqk = c_s.shape[1]
    dv = c_s.shape[2]

    @pl.when(jnp.logical_and(units.first(u), is_prompt))
    def _():
        c_s[...] = jnp.zeros_like(c_s)
        n_s[...] = jnp.zeros_like(n_s)
        m_s[...] = jnp.zeros_like(m_s)

    @pl.when(is_sample)
    def _():
        c_s[...] = c0_ref[...]
        n_s[...] = n0_ref[...]
        for hh in range(H):
            m_s[hh] = jnp.full(m_s.shape[1:], m0_ref[layer * units.n_sample + units.sample_index(u), hh], F32)

    gates = gate_ref[...]
    lane = lax.broadcasted_iota(jnp.int32, gates.shape, 1)
    row = lax.broadcasted_iota(jnp.int32, (L, L), 0)
    col = lax.broadcasted_iota(jnp.int32, (L, L), 1)
    eye = row == col
    tril = col <= row
    triu = row <= col

    heads = range(H)
    qs = [slice(hh * dqk, (hh + 1) * dqk) for hh in heads]
    vs = [slice(hh * dv, (hh + 1) * dv) for hh in heads]

    ig_col, b_col, m_prev, m_t, w_intra, w_inter = [], [], [], [], [], []
    for hh in heads:
        ig_c = jnp.sum(jnp.where(lane == hh, gates, 0.0), axis=1, keepdims=True) + bi_ref[layer, hh]
        f_c = jnp.sum(jnp.where(lane == H + hh, gates, 0.0), axis=1, keepdims=True) + bf_ref[layer, hh]
        lf_c = _log_sigmoid(f_c)
        ig_row = jnp.sum(jnp.where(eye, ig_c, 0.0), axis=0, keepdims=True)
        lf_row = jnp.sum(jnp.where(eye, lf_c, 0.0), axis=0, keepdims=True)
        b_c = jnp.sum(jnp.where(tril, lf_row, 0.0), axis=1, keepdims=True)
        b_row = jnp.sum(jnp.where(triu, lf_c, 0.0), axis=0, keepdims=True)
        d = jnp.where(tril, b_c - b_row + ig_row, NEG_INF)
        m_p = m_s[hh][:, :1]
        inter = b_c + m_p
        m_c = jnp.maximum(inter, jnp.max(d, axis=1, keepdims=True))
        ig_col.append(ig_c); b_col.append(b_c); m_prev.append(m_p); m_t.append(m_c)
        w_intra.append(jnp.exp(d - m_c))
        w_inter.append(jnp.exp(inter - m_c))

    a = [lax.dot_general(q_ref[:, qs[hh]].astype(BF16),
                         (k_ref[:, qs[hh]].astype(F32) * (dqk ** -0.5)).astype(BF16), NT,
                         preferred_element_type=F32) * w_intra[hh] for hh in heads]

    hval = []
    for hh in heads:
        q = q_ref[:, qs[hh]].astype(F32)
        num = (jnp.dot(a[hh].astype(BF16), v_ref[:, vs[hh]].astype(BF16), preferred_element_type=F32)
               + jnp.dot(q_ref[:, qs[hh]].astype(BF16), c_s[hh].astype(BF16), preferred_element_type=F32)
               * w_inter[hh])
        den = (jnp.sum(a[hh], axis=1, keepdims=True)
               + jnp.sum(q * n_s[hh], axis=1, keepdims=True) * w_inter[hh])
        den = jnp.maximum(jnp.abs(den), jnp.exp(-m_t[hh]))
        hval.append(num / den)

    for hh in heads:
        y = hval[hh] * lax.rsqrt(jnp.mean(hval[hh] * hval[hh], axis=1, keepdims=True) + EPS)
        gate_out = _sigmoid(og_ref[:, vs[hh]].astype(F32))
        hm_ref[:, vs[hh]] = (y * gh_ref[:, vs[hh]] * gate_out).astype(hm_ref.dtype)

    for hh in heads:
        k = k_ref[:, qs[hh]].astype(F32) * (dqk ** -0.5)
        m_new = m_t[hh][L - 1:L, :]
        b_last = b_col[hh][L - 1:L, :]
        w_last = jnp.exp(b_last - b_col[hh] + ig_col[hh] - m_new)
        decay = jnp.exp(b_last + m_prev[hh] - m_new)
        k_w = k * w_last
        c_s[hh] = decay * c_s[hh] + lax.dot_general(k_w.astype(BF16), v_ref[:, vs[hh]].astype(BF16), TN,
                                                    preferred_element_type=F32)
        n_s[hh] = decay * n_s[hh] + jnp.sum(k_w, axis=0, keepdims=True)
        m_s[hh] = jnp.broadcast_to(m_new, m_s.shape[1:])

    @pl.when(jnp.logical_and(units.last(u), is_prompt))
    def _():
        cp_ref[...] = c_s[...]
        np_ref[...] = n_s[...]
        mp_ref[...] = m_s[...]

    @pl.when(is_sample)
    def _():
        cs_ref[...] = c_s[...]
        ns_ref[...] = n_s[...]
        ms_ref[...] = m_s[...]


def mlstm_mixer(z, gates, b_i, b_f, g_head, c0, n0, m0, layer, units):
    T = z.shape[0]
    n_layers, DB, H, dqk, dv = c0.shape
    B = units.n_prompt
    qw = H * dqk
    vw = H * dv
    assert (2 * qw) % vw == 0
    v_blk = (2 * qw) // vw
    sidx = units.sample_index

    def pidx(u):
        return jnp.minimum(units.stream(u), B - 1)

    smem = pl.BlockSpec(memory_space=pltpu.SMEM)
    nbytes = (2 * (2 * CHUNK * qw * 2 + 2 * CHUNK * vw * 2 + CHUNK * LANES * 4 + CHUNK * vw * 2 + 3 * qw * dv * 4)
              + qw * dv * 4 * 4)

    def state_specs(idx):
        return [pl.BlockSpec((None, H, dqk, dv), lambda u: (idx(u), 0, 0, 0)),
                pl.BlockSpec((None, H, 1, dqk), lambda u: (idx(u), 0, 0, 0)),
                pl.BlockSpec((None, H, 1, LANES), lambda u: (idx(u), 0, 0, 0))]

    def state_shapes(n):
        return [jax.ShapeDtypeStruct((n, H, dqk, dv), F32), jax.ShapeDtypeStruct((n, H, 1, dqk), F32),
                jax.ShapeDtypeStruct((n, H, 1, LANES), F32)]

    hm, cp, np_, mp, cs, ns_, ms = pl.pallas_call(
        functools.partial(_mlstm_kernel, units=units, n_heads=H, layer=layer),
        grid=(units.total,),
        in_specs=[smem, smem, smem,
                  pl.BlockSpec((CHUNK, qw), lambda u: (u, 0)),
                  pl.BlockSpec((CHUNK, qw), lambda u: (u, 1)),
                  pl.BlockSpec((CHUNK, vw), lambda u: (u, v_blk)),
                  pl.BlockSpec((CHUNK, vw), lambda u: (u, v_blk + 1)),
                  pl.BlockSpec((CHUNK, LANES), lambda u: (u, 0)),
                  pl.BlockSpec((None, 1, vw), lambda u: (layer, 0, 0)),
                  pl.BlockSpec((None, None, H, dqk, dv), lambda u: (layer, sidx(u), 0, 0, 0)),
                  pl.BlockSpec((None, None, H, 1, dqk), lambda u: (layer, sidx(u), 0, 0, 0))],
        out_specs=[pl.BlockSpec((CHUNK, vw), lambda u: (u, 0))] + state_specs(pidx) + state_specs(sidx),
        out_shape=[jax.ShapeDtypeStruct((T, vw), BF16)] + state_shapes(B) + state_shapes(DB),
        scratch_shapes=[pltpu.VMEM((H, dqk, dv), F32), pltpu.VMEM((H, 1, dqk), F32), pltpu.VMEM((H, 1, LANES), F32)],
        compiler_params=_params(("arbitrary",), nbytes),
        name="mlstm",
    )(b_i, b_f, m0.reshape(n_layers * DB, H), z, z, z, z, gates, g_head[:, None, :], c0,
      n0.reshape(n_layers, DB, H, 1, dqk))
    return hm, (cp, np_[:, :, 0, :], mp[:, :, 0, 0]), (cs, ns_[:, :, 0, :], ms[:, :, 0, 0])


def _swa_kernel(sink_ref, q_ref, k0_ref, k1_ref, k2_ref, v0_ref, v1_ref, v2_ref, ck1_ref, ck2_ref, cv1_ref, cv2_ref,
                o_ref, bias_s, *, units, n_kv, layer):
    u = pl.program_id(0)
    is_sample = units.is_sample(u)
    L = CHUNK
    W = 2 * L
    NK = W + L
    hd = SWA_HEAD_DIM
    pairs = SWA_GROUP // 2
    R = pairs * L
    n_heads = n_kv * SWA_GROUP
    pair_of_row = lax.broadcasted_iota(jnp.int32, (R, 1), 0) // L

    @pl.when(u == 0)
    def _():
        jj = lax.broadcasted_iota(jnp.int32, (R, NK), 1)
        tt = lax.broadcasted_iota(jnp.int32, (R, NK), 0) % L
        dist = jnp.abs(tt + W - jj).astype(F32)
        for g in range(n_kv):
            for odd in range(2):
                head = g * SWA_GROUP + 2 * pair_of_row + odd
                slope = jnp.exp2(-8.0 * (head + 1).astype(F32) / n_heads) * LOG2E
                pen = slope * dist
                for v in range(3):
                    bias_s[v, 2 * g + odd] = jnp.where(jj >= W - v * L, pen, -NEG_INF)

    def pick(c_ref, z_ref):
        return jnp.where(is_sample, c_ref[...], z_ref[...])

    k_all = jnp.concatenate([pick(ck2_ref, k2_ref), pick(ck1_ref, k1_ref), k0_ref[...]], axis=0)
    v_all = jnp.concatenate([pick(cv2_ref, v2_ref), pick(cv1_ref, v1_ref), v0_ref[...]], axis=0)
    variant = jnp.where(is_sample, 2, jnp.minimum(u % units.ncp, 2))
    lane = lax.broadcasted_iota(jnp.int32, (NK, LANES), 1)
    lo_lanes = lane < hd

    for g in range(n_kv):
        tile = (g * hd) // LANES
        kt = k_all[:, tile * LANES:(tile + 1) * LANES]
        vt = v_all[:, tile * LANES:(tile + 1) * LANES]
        kr = pltpu.roll(kt, hd, axis=1)
        vr = pltpu.roll(vt, hd, axis=1)
        if (g * hd) % LANES == 0:
            k_lo, k_hi, v_lo, v_hi = kt, kr, vt, vr
        else:
            k_lo, k_hi, v_lo, v_hi = kr, kt, vr, vt
        k_lo = jnp.where(lo_lanes, k_lo, 0.0).astype(BF16)
        k_hi = jnp.where(lo_lanes, 0.0, k_hi).astype(BF16)
        v_lo = jnp.where(lo_lanes, v_lo, 0.0).astype(BF16)
        v_hi = jnp.where(lo_lanes, 0.0, v_hi).astype(BF16)
        qg = jnp.concatenate(
            [q_ref[:, (g * pairs + p) * LANES:(g * pairs + p + 1) * LANES] for p in range(pairs)], axis=0)
        qg = (qg * (hd ** -0.5 * LOG2E)).astype(BF16)

        def attend(k_half, v_half, odd):
            s = lax.dot_general(qg, k_half, NT, preferred_element_type=F32) - bias_s[variant, 2 * g + odd]
            sink = jnp.zeros((R, 1), F32)
            for p in range(pairs):
                sink = jnp.where(pair_of_row == p, sink_ref[layer, g * SWA_GROUP + 2 * p + odd] * LOG2E, sink)
            mx = jnp.maximum(jnp.max(s, axis=1, keepdims=True), sink)
            e = jnp.exp2(s - mx)
            den = jnp.sum(e, axis=1, keepdims=True) + jnp.exp2(sink - mx)
            return jnp.dot(e.astype(BF16), v_half, preferred_element_type=F32) * (1.0 / den)

        o = attend(k_lo, v_lo, 0) + attend(k_hi, v_hi, 1)
        for p in range(pairs):
            o_ref[:, (g * pairs + p) * LANES:(g * pairs + p + 1) * LANES] = o[p * L:(p + 1) * L].astype(o_ref.dtype)


def swa_mixer(z, sinks, cache_k, cache_v, layer, units, n_kv):
    T = z.shape[0]
    qw = n_kv * SWA_GROUP * SWA_HEAD_DIM
    kw = n_kv * SWA_HEAD_DIM
    assert kw % LANES == 0 and qw % kw == 0
    kb = qw // kw
    vb = kb + 1
    sidx = units.sample_index
    R = (SWA_GROUP // 2) * CHUNK
    NK = 3 * CHUNK

    def zspec(blk, back):
        return pl.BlockSpec((CHUNK, kw), lambda u: (jnp.maximum(u - back, 0), blk))

    def cspec(part):
        return pl.BlockSpec((None, CHUNK, kw), lambda u: (sidx(u), part, 0))

    bias_bytes = 3 * 2 * n_kv * R * 2 * LANES * 4
    nbytes = 2 * (CHUNK * qw * 4 + 10 * CHUNK * kw * 4 + CHUNK * qw * 2) + bias_bytes + 16 * R * NK * 4
    return pl.pallas_call(
        functools.partial(_swa_kernel, units=units, n_kv=n_kv, layer=layer),
        grid=(units.total,),
        in_specs=[pl.BlockSpec(memory_space=pltpu.SMEM),
                  pl.BlockSpec((CHUNK, qw), lambda u: (u, 0)),
                  zspec(kb, 0), zspec(kb, 1), zspec(kb, 2),
                  zspec(vb, 0), zspec(vb, 1), zspec(vb, 2),
                  cspec(1), cspec(0), cspec(1), cspec(0)],
        out_specs=pl.BlockSpec((CHUNK, qw), lambda u: (u, 0)),
        out_shape=jax.ShapeDtypeStruct((T, qw), BF16),
        scratch_shapes=[pltpu.VMEM((3, 2 * n_kv, R, NK), F32)],
        compiler_params=_params(("arbitrary",), nbytes),
        name="swa",
    )(sinks, z, z, z, z, z, z, z, cache_k, cache_k, cache_v, cache_v)


def _gla_kernel(q_ref, k_ref, v_ref, r_ref, glr_ref, wg_ref, bg_ref, gh_ref, s0_ref, o_ref, sp_ref, ss_ref, s_s, *,
                units):
    u = pl.program_id(1)
    is_sample = units.is_sample(u)
    is_prompt = jnp.logical_not(is_sample)
    L = CHUNK
    hb, dv, dk = s_s.shape

    @pl.when(jnp.logical_and(units.first(u), is_prompt))
    def _():
        s_s[...] = jnp.zeros_like(s_s)

    @pl.when(is_sample)
    def _():
        for hh in range(hb):
            s_s[hh] = s0_ref[hh].T

    glr = glr_ref[...].astype(BF16)
    row = lax.broadcasted_iota(jnp.int32, (L, L), 0)
    col = lax.broadcasted_iota(jnp.int32, (L, L), 1)
    tril = col <= row
    ones_tril = jnp.where(tril, 1.0, 0.0).astype(BF16)
    rr = lax.broadcasted_iota(jnp.int32, (L, dk), 0)
    n_sub = L // GLA_SUB

    heads = range(hb)
    ks = [slice(hh * dk, (hh + 1) * dk) for hh in heads]
    vs = [slice(hh * dv, (hh + 1) * dv) for hh in heads]
    lg = [_log_sigmoid(jnp.dot(glr, wg_ref[:, ks[hh]], preferred_element_type=F32) + bg_ref[:, ks[hh]])
          * (1.0 / GLA_TAU) for hh in heads]
    bc = [sum(jnp.dot(ones_tril, part, preferred_element_type=F32) for part in _split3(lg[hh])) for hh in heads]

    refs = [[bc[hh][i * GLA_SUB:i * GLA_SUB + 1, :] for i in range(n_sub)] for hh in heads]
    q_t = []
    for hh in heads:
        r_q = refs[hh][n_sub - 1]
        for i in range(n_sub - 2, -1, -1):
            r_q = jnp.where(rr < (i + 1) * GLA_SUB, refs[hh][i], r_q)
        q_t.append((q_ref[:, ks[hh]].astype(F32) * (dk ** -0.5) * jnp.exp(bc[hh] - r_q)).astype(BF16))
    blocks = [[] for _ in heads]
    for i in range(n_sub):
        for hh in heads:
            e_i = jnp.where(rr < (i + 1) * GLA_SUB, refs[hh][i] - bc[hh], NEG_INF)
            k_i = (k_ref[:, ks[hh]].astype(F32) * jnp.exp(e_i)).astype(BF16)
            blocks[hh].append(lax.dot_general(q_t[hh][i * GLA_SUB:(i + 1) * GLA_SUB], k_i, NT,
                                              preferred_element_type=F32))
    a = [jnp.where(tril, jnp.concatenate(blocks[hh], axis=0), 0.0).astype(BF16) for hh in heads]

    o = []
    for hh in heads:
        q_dec = (q_ref[:, ks[hh]].astype(F32) * (dk ** -0.5) * jnp.exp(bc[hh])).astype(BF16)
        o.append(jnp.dot(a[hh], v_ref[:, vs[hh]].astype(BF16), preferred_element_type=F32)
                 + lax.dot_general(q_dec, s_s[hh].astype(BF16), NT, preferred_element_type=F32))

    for hh in heads:
        b_last = bc[hh][L - 1:L, :]
        k_dec = (k_ref[:, ks[hh]].astype(F32) * jnp.exp(b_last - bc[hh])).astype(BF16)
        s_s[hh] = jnp.exp(b_last) * s_s[hh] + lax.dot_general(v_ref[:, vs[hh]].astype(BF16), k_dec, TN,
                                                               preferred_element_type=F32)

    for hh in heads:
        y = o[hh] * lax.rsqrt(jnp.mean(o[hh] * o[hh], axis=1, keepdims=True) + EPS)
        r = r_ref[:, vs[hh]].astype(F32)
        o_ref[:, vs[hh]] = (y * gh_ref[:, vs[hh]] * (r * _sigmoid(r))).astype(o_ref.dtype)

    @pl.when(jnp.logical_and(units.last(u), is_prompt))
    def _():
        for hh in range(hb):
            sp_ref[hh] = s_s[hh].T

    @pl.when(is_sample)
    def _():
        for hh in range(hb):
            ss_ref[hh] = s_s[hh].T


def gla_mixer(z, glr, w_g2, b_g, g_head, s0, layer, units):
    T = z.shape[0]
    _, DB, H, dk, dv = s0.shape
    B = units.n_prompt
    hb = GLA_HEADS_PER_STEP if H % GLA_HEADS_PER_STEP == 0 else 1
    ng = H // hb
    kw = hb * dk
    vw = hb * dv
    assert (2 * H * dk) % vw == 0
    v_blk0 = (2 * H * dk) // vw
    sidx = units.sample_index

    def pidx(u):
        return jnp.minimum(units.stream(u), B - 1)

    nbytes = (2 * (2 * CHUNK * kw * 2 + 2 * CHUNK * vw * 2 + CHUNK * LANES * 4 + LANES * kw * 2 + CHUNK * vw * 2
                   + 3 * kw * dv * 4) + kw * dv * 4 * 4)
    return pl.pallas_call(
        functools.partial(_gla_kernel, units=units),
        grid=(ng, units.total),
        in_specs=[pl.BlockSpec((CHUNK, kw), lambda g, u: (u, g)),
                  pl.BlockSpec((CHUNK, kw), lambda g, u: (u, ng + g)),
                  pl.BlockSpec((CHUNK, vw), lambda g, u: (u, v_blk0 + g)),
                  pl.BlockSpec((CHUNK, vw), lambda g, u: (u, v_blk0 + ng + g)),
                  pl.BlockSpec((CHUNK, LANES), lambda g, u: (u, 0)),
                  pl.BlockSpec((LANES, kw), lambda g, u: (0, g)),
                  pl.BlockSpec((None, 1, kw), lambda g, u: (layer, 0, g)),
                  pl.BlockSpec((None, 1, vw), lambda g, u: (layer, 0, g)),
                  pl.BlockSpec((None, None, hb, dk, dv), lambda g, u: (layer, sidx(u), g, 0, 0))],
        out_specs=[pl.BlockSpec((CHUNK, vw), lambda g, u: (u, g)),
                   pl.BlockSpec((None, hb, dk, dv), lambda g, u: (pidx(u), g, 0, 0)),
                   pl.BlockSpec((None, hb, dk, dv), lambda g, u: (sidx(u), g, 0, 0))],
        out_shape=[jax.ShapeDtypeStruct((T, H * dv), BF16),
                   jax.ShapeDtypeStruct((B, H, dk, dv), F32),
                   jax.ShapeDtypeStruct((DB, H, dk, dv), F32)],
        scratch_shapes=[pltpu.VMEM((hb, dv, dk), F32)],
        compiler_params=_params(("parallel", "arbitrary"), nbytes),
        name="gla",
    )(z, z, z, z, glr, w_g2, b_g[:, None, :], g_head[:, None, :], s0)


def _pad_rows(w, height):
    return jnp.pad(w, ((0, height - w.shape[0]), (0, 0)))


def kernel(x_prompt, x_sample, cache_swa_k, cache_swa_v, state_mlstm_C, state_mlstm_n, state_mlstm_m, state_gla_S,
           norm_mix, norm_ffn, norm_final, w_even_in, b_mlstm_i, b_mlstm_f, mlstm_head_norm, swa_sinks, w_even_out,
           w_odd_in, w_gla_gate, b_gla_gate, gla_head_norm, w_odd_out, w_ffn_in, w_ffn_out):
    B, S, D = x_prompt.shape
    DB, DS, _ = x_sample.shape
    depth = norm_mix.shape[0]
    assert DS == CHUNK and S % CHUNK == 0
    MH, dqk, dv = state_mlstm_C.shape[2:]
    n_kv = cache_swa_k.shape[3]
    w_buf = cache_swa_k.shape[2]
    assert w_buf == 2 * CHUNK and cache_swa_k.shape[4] == SWA_HEAD_DIM
    GH, gdk, gdv = state_gla_S.shape[2:]
    rank = w_gla_gate.shape[1]
    m_w = 2 * MH * dqk + 2 * MH * dv
    sq_w = n_kv * SWA_GROUP * SWA_HEAD_DIM
    kvw = n_kv * SWA_HEAD_DIM
    s_w = sq_w + 2 * kvw
    g_w = 2 * GH * gdk + 2 * GH * gdv
    units = Units(B, S // CHUNK, DB)
    TP = B * S

    x = jnp.concatenate([x_prompt.reshape(TP, D), x_sample.reshape(DB * DS, D)], axis=0)
    w_even_in_t = jnp.swapaxes(w_even_in, 1, 2)
    w_odd_in_t = jnp.swapaxes(w_odd_in, 1, 2)

    p_k, p_v, p_C, p_n, p_m, p_S = [], [], [], [], [], []
    s_k, s_v, s_C, s_n, s_m, s_S = [], [], [], [], [], []
    for l in range(depth):
        e = l // 2
        if l % 2 == 0:
            w_gate = _pad_rows(w_even_in_t[e, m_w:m_w + 2 * MH], LANES)[None]
            w_attn = w_even_in_t[e, m_w + 2 * MH:][None]
            h = rmsnorm(x, norm_mix, l, BF16)
            zm = project([h], w_even_in_t, e, m_w, BF16, w_nk=True, name="even_in_mlstm")
            zs = project([h], w_attn, 0, s_w, w_nk=True, name="even_in_attn")
            gates = project([h], w_gate, 0, LANES, w_nk=True, name="even_gates")
            hm, (pc, pn, pm), (sc, sn, sm) = mlstm_mixer(
                zm, gates, b_mlstm_i, b_mlstm_f, mlstm_head_norm, state_mlstm_C, state_mlstm_n, state_mlstm_m,
                e, units)
            ck = cache_swa_k[e].reshape(DB, w_buf, kvw)
            cv = cache_swa_v[e].reshape(DB, w_buf, kvw)
            hs = swa_mixer(zs, swa_sinks, ck, cv, e, units, n_kv)
            x = project([hm, hs], w_even_out, e, D, residual=x, name="even_out")
            zk = zs[:, sq_w:sq_w + kvw]
            zv = zs[:, sq_w + kvw:]
            kv_shape = (w_buf, n_kv, SWA_HEAD_DIM)
            p_k.append(zk[:TP].reshape(B, S, kvw)[:, -w_buf:].reshape((B,) + kv_shape))
            p_v.append(zv[:TP].reshape(B, S, kvw)[:, -w_buf:].reshape((B,) + kv_shape))
            s_k.append(jnp.concatenate([ck[:, CHUNK:], zk[TP:].reshape(DB, DS, kvw)], axis=1).reshape((DB,) + kv_shape))
            s_v.append(jnp.concatenate([cv[:, CHUNK:], zv[TP:].reshape(DB, DS, kvw)], axis=1).reshape((DB,) + kv_shape))
            p_C.append(pc); p_n.append(pn); p_m.append(pm)
            s_C.append(sc); s_n.append(sn); s_m.append(sm)
        else:
            z, h = normed_project(x, norm_mix, l, w_odd_in_t, e, g_w, BF16, w_nk=True, emit_h=True,
                                  name="odd_in")
            w_rank = _pad_rows(w_odd_in_t[e, g_w:], LANES)[None]
            glr = project([h], w_rank, 0, LANES, w_nk=True, name="odd_gate_rank")
            w_g2 = jnp.pad(w_gla_gate[e], ((0, LANES - rank), (0, 0))).astype(BF16)
            mix, ps, ss = gla_mixer(z, glr, w_g2, b_gla_gate, gla_head_norm, state_gla_S, e, units)
            x = project([mix], w_odd_out, e, D, residual=x, name="odd_out")
            p_S.append(ps); s_S.append(ss)
        act = normed_project(x, norm_ffn, l, w_ffn_in, l, w_ffn_in.shape[2] // 2, BF16, swiglu=True, name="ffn_in")
        x = project([act], w_ffn_out, l, D, residual=x, wide_k=True, name="ffn_out")
    y_prompt = rmsnorm(x, norm_final[None], 0, F32, 0, TP)
    y_sample = rmsnorm(x, norm_final[None], 0, F32, TP, DB * DS)
    return (y_prompt.reshape(B, S, D), y_sample.reshape(DB, DS, D),
            jnp.stack(p_k), jnp.stack(p_v), jnp.stack(p_C), jnp.stack(p_n), jnp.stack(p_m), jnp.stack(p_S),
            jnp.stack(s_k), jnp.stack(s_v), jnp.stack(s_C), jnp.stack(s_n), jnp.stack(s_m), jnp.stack(s_S))
```

```python
import functools

import jax
import jax.numpy as jnp
from jax import lax
from jax.experimental import pallas as pl
from jax.experimental.pallas import tpu as pltpu

F32 = jnp.float32
BF16 = jnp.bfloat16

CHUNK = 64
SWA_HEAD_DIM = 64
SWA_GROUP = 8
LANES = 128
GLA_SUB = 16
GLA_HEADS_PER_STEP = 8
GLA_TAU = 16.0
LOG2E = 1.4426950408889634
EPS = 1e-6
NEG_INF = float("-inf")

V7X_VMEM_BYTES = 64 * 1024 * 1024
VMEM_CAP_BYTES = V7X_VMEM_BYTES - 4 * 1024 * 1024

NT = (((1,), (1,)), ((), ()))
TN = (((0,), (0,)), ((), ()))


def _pick(n, cands):
    for c in cands:
        if n % c == 0:
            return c
    raise ValueError(f"no tile in {cands} divides {n}")


def _params(semantics, block_bytes):
    limit = min(VMEM_CAP_BYTES, block_bytes + 16 * 1024 * 1024)
    return pltpu.CompilerParams(dimension_semantics=semantics, vmem_limit_bytes=int(limit))


def _log_sigmoid(x):
    return jnp.minimum(x, 0.0) - jnp.log(1.0 + jnp.exp(-jnp.abs(x)))


def _sigmoid(x):
    return 1.0 / (1.0 + jnp.exp(-x))


def _split3(x):
    hi = x.astype(BF16)
    r1 = x - hi.astype(F32)
    mid = r1.astype(BF16)
    lo = (r1 - mid.astype(F32)).astype(BF16)
    return hi, mid, lo


def _rmsnorm_kernel(x_ref, g_ref, o_ref):
    x = x_ref[...]
    y = x * lax.rsqrt(jnp.mean(x * x, axis=-1, keepdims=True) + EPS)
    o_ref[...] = (y * g_ref[...]).astype(o_ref.dtype)


def rmsnorm(x, g, layer, out_dtype, row0=0, rows=None):
    D = x.shape[1]
    T = x.shape[0] if rows is None else rows
    tr = _pick(T, (512, 272, 256, 160, 128, 64))
    assert row0 % tr == 0
    blk0 = row0 // tr
    nbytes = 2 * tr * D * (4 + jnp.dtype(out_dtype).itemsize)
    return pl.pallas_call(
        _rmsnorm_kernel,
        grid=(T // tr,),
        in_specs=[pl.BlockSpec((tr, D), lambda i: (blk0 + i, 0)), pl.BlockSpec((None, 1, D), lambda i: (layer, 0, 0))],
        out_specs=pl.BlockSpec((tr, D), lambda i: (i, 0)),
        out_shape=jax.ShapeDtypeStruct((T, D), out_dtype),
        compiler_params=_params(("parallel",), nbytes),
        name="rmsnorm",
    )(x, g[:, None, :])


def _stack_norm_kernel(xp_ref, xs_ref, g_ref, x_ref, h_ref, *, n_prompt_tiles):
    x = jnp.where(pl.program_id(0) < n_prompt_tiles, xp_ref[...], xs_ref[...])
    x_ref[...] = x
    y = x * lax.rsqrt(jnp.mean(x * x, axis=-1, keepdims=True) + EPS)
    h_ref[...] = (y * g_ref[...]).astype(h_ref.dtype)


def stack_and_norm(x_prompt, x_sample, g, layer):
    TP, D = x_prompt.shape
    TS = x_sample.shape[0]
    tr = next(c for c in (256, 128, 64) if TP % c == 0 and TS % c == 0)
    n_p = TP // tr
    T = TP + TS
    nbytes = 2 * tr * D * (4 + 4 + 4 + 2)
    return pl.pallas_call(
        functools.partial(_stack_norm_kernel, n_prompt_tiles=n_p),
        grid=(T // tr,),
        in_specs=[pl.BlockSpec((tr, D), lambda i: (jnp.minimum(i, n_p - 1), 0)),
                  pl.BlockSpec((tr, D), lambda i: (jnp.maximum(i - n_p, 0), 0)),
                  pl.BlockSpec((None, 1, D), lambda i: (layer, 0, 0))],
        out_specs=[pl.BlockSpec((tr, D), lambda i: (i, 0)), pl.BlockSpec((tr, D), lambda i: (i, 0))],
        out_shape=[jax.ShapeDtypeStruct((T, D), F32), jax.ShapeDtypeStruct((T, D), BF16)],
        compiler_params=_params(("parallel",), nbytes),
        name="stack_norm",
    )(x_prompt, x_sample, g[:, None, :])


def _m_tile(T):
    return _pick(T, (1088, 1024, 640, 512, 320, 256, 128, 64))


def _proj_kernel(*refs, n_x, has_res, w_nk):
    x_refs = refs[:n_x]
    w_ref = refs[n_x]
    r_ref = refs[n_x + 1] if has_res else None
    o_ref = refs[n_x + 1 + has_res]
    k0 = 0
    acc = None
    for x_ref in x_refs:
        kx = x_ref.shape[1]
        if w_nk:
            part = lax.dot_general(x_ref[...], w_ref[:, k0:k0 + kx].astype(BF16), NT, preferred_element_type=F32)
        else:
            part = jnp.dot(x_ref[...], w_ref[k0:k0 + kx, :].astype(BF16), preferred_element_type=F32)
        acc = part if acc is None else acc + part
        k0 += kx
    if has_res:
        acc = r_ref[...] + acc
    o_ref[...] = acc.astype(o_ref.dtype)


def project(xs, w, layer, n_cols, out_dtype=F32, residual=None, w_nk=False, wide_k=False, name="project"):
    T = xs[0].shape[0]
    K = w.shape[2] if w_nk else w.shape[1]
    assert sum(x.shape[1] for x in xs) == K
    tm = _m_tile(T)
    tn = _pick(n_cols, (256, 128) if wide_k else (512, 256, 128))
    osz = jnp.dtype(out_dtype).itemsize
    x_bufs = 1 if wide_k else 2
    nbytes = x_bufs * tm * K * 2 + 2 * (K * tn * 4 + tm * tn * osz) + K * tn * 2 + tm * tn * 4
    x_mode = dict(pipeline_mode=pl.Buffered(1)) if wide_k else {}
    in_specs = [pl.BlockSpec((tm, x.shape[1]), lambda i, j: (i, 0), **x_mode) for x in xs]
    if w_nk:
        in_specs.append(pl.BlockSpec((None, tn, K), lambda i, j: (layer, j, 0)))
    else:
        in_specs.append(pl.BlockSpec((None, K, tn), lambda i, j: (layer, 0, j)))
    args = list(xs) + [w]
    if residual is not None:
        in_specs.append(pl.BlockSpec((tm, tn), lambda i, j: (i, j)))
        args.append(residual)
        nbytes += 2 * tm * tn * 4
    return pl.pallas_call(
        functools.partial(_proj_kernel, n_x=len(xs), has_res=residual is not None, w_nk=w_nk),
        grid=(T // tm, n_cols // tn),
        in_specs=in_specs,
        out_specs=pl.BlockSpec((tm, tn), lambda i, j: (i, j)),
        out_shape=jax.ShapeDtypeStruct((T, n_cols), out_dtype),
        compiler_params=_params(("parallel", "arbitrary"), nbytes),
        name=name,
    )(*args)


def _normed_kernel(*refs, n_slabs, slab_rows, n_w, w_nk, emit_h, swiglu):
    x_ref, g_ref = refs[:2]
    w_refs = refs[2:2 + n_w]
    o_ref = refs[2 + n_w]
    hout_ref = refs[3 + n_w] if emit_h else None
    h2_ref = refs[-1]
    t = pl.program_id(0)
    slab = jnp.where(t < pl.num_programs(0) - 1, jnp.minimum(pl.program_id(1), n_slabs - 1), n_slabs - 1)

    def norm_slab(buf):
        x = x_ref[...]
        y = x * lax.rsqrt(jnp.mean(x * x, axis=-1, keepdims=True) + EPS)
        yb = (y * g_ref[...]).astype(BF16)
        h2_ref[buf, pl.ds(pl.multiple_of(slab * slab_rows, slab_rows), slab_rows), :] = yb
        if emit_h:
            hout_ref[...] = yb

    def matmul(h, w_ref):
        if w_nk:
            return lax.dot_general(h, w_ref[...].astype(BF16), NT, preferred_element_type=F32)
        return jnp.dot(h, w_ref[...].astype(BF16), preferred_element_type=F32)

    @pl.when(t == 0)
    def _():
        norm_slab(0)

    @pl.when(t > 0)
    def _():
        h = h2_ref[(t - 1) % 2]
        if swiglu:
            g = matmul(h, w_refs[0])
            u = matmul(h, w_refs[1])
            o_ref[...] = (g * _sigmoid(g) * u).astype(o_ref.dtype)
        else:
            o_ref[...] = matmul(h, w_refs[0]).astype(o_ref.dtype)
        norm_slab(t % 2)


def _slab_count(tm, nj):
    units16 = tm // 16
    return max(d for d in range(1, units16 + 1) if units16 % d == 0 and d <= nj)


def normed_project(x, g, g_layer, w, layer, n_cols, out_dtype, w_nk=False, swiglu=False, emit_h=False,
                   name="normed_project"):
    T, K = x.shape
    tm = _pick(T, (2176, 1088, 640, 512, 256, 128, 64)) if swiglu else _m_tile(T)
    cands = [c for c in ((256, 128) if swiglu else (512, 256, 128)) if n_cols % c == 0]
    tn = next((c for c in cands if n_cols // c >= tm // CHUNK), cands[-1])
    nj = n_cols // tn
    n_tiles = T // tm
    ns = _slab_count(tm, nj)
    rows = tm // ns
    osz = jnp.dtype(out_dtype).itemsize
    n_w = 2 if swiglu else 1

    def row_tile(t):
        return jnp.maximum(t - 1, 0)

    def col(t, j):
        return jnp.where(t == 0, 0, j)

    def slab_idx(t, j):
        return jnp.where(t < n_tiles, t * ns + jnp.minimum(j, ns - 1), n_tiles * ns - 1)

    if w_nk:
        w_specs = [pl.BlockSpec((None, tn, K), lambda t, j, o=o: (layer, col(t, j) + o, 0)) for o in range(n_w)]
    else:
        w_specs = [pl.BlockSpec((None, K, tn), lambda t, j, o=o: (layer, 0, col(t, j) + o * nj)) for o in range(n_w)]
    out_specs = [pl.BlockSpec((tm, tn), lambda t, j: (row_tile(t), col(t, j)))]
    out_shape = [jax.ShapeDtypeStruct((T, n_cols), out_dtype)]
    if emit_h:
        out_specs.append(pl.BlockSpec((rows, K), lambda t, j: (slab_idx(t, j), 0)))
        out_shape.append(jax.ShapeDtypeStruct((T, K), BF16))
    nbytes = (2 * tm * K * 2 + 2 * rows * K * (4 + 2 * emit_h) + n_w * (2 * K * tn * 4 + K * tn * 2)
              + 2 * tm * tn * osz + (1 + n_w) * tm * tn * 4)
    outs = pl.pallas_call(
        functools.partial(_normed_kernel, n_slabs=ns, slab_rows=rows, n_w=n_w, w_nk=w_nk, emit_h=emit_h,
                          swiglu=swiglu),
        grid=(n_tiles + 1, nj),
        in_specs=[pl.BlockSpec((rows, K), lambda t, j: (slab_idx(t, j), 0)),
                  pl.BlockSpec((None, 1, K), lambda t, j: (g_layer, 0, 0))] + w_specs,
        out_specs=out_specs,
        out_shape=out_shape,
        scratch_shapes=[pltpu.VMEM((2, tm, K), BF16)],
        compiler_params=_params(("arbitrary", "arbitrary"), nbytes),
        name=name,
    )(x, g[:, None, :], *([w] * n_w))
    return outs if emit_h else outs[0]


class Units:
    def __init__(self, n_prompt, chunks_per_prompt, n_sample):
        self.ncp = chunks_per_prompt
        self.up = n_prompt * chunks_per_prompt
        self.n_prompt = n_prompt
        self.n_sample = n_sample
        self.total = self.up + n_sample
        self.n_streams = n_prompt + n_sample

    def is_sample(self, u):
        return u >= self.up

    def stream(self, u):
        return jnp.where(u < self.up, u // self.ncp, self.n_prompt + u - self.up)

    def sample_index(self, u):
        return jnp.maximum(u - self.up, 0)

    def first(self, u):
        return jnp.logical_or(u >= self.up, u % self.ncp == 0)

    def last(self, u):
        return jnp.logical_or(u >= self.up, u % self.ncp == self.ncp - 1)


def _mlstm_kernel(bi_ref, bf_ref, m0_ref, q_ref, k_ref, v_ref, og_ref, gate_ref, gh_ref, c0_ref, n0_ref,
                  hm_ref, cp_ref, np_ref, mp_ref, cs_ref, ns_ref, ms_ref, c_s, n_s, m_s, *, units, n_heads, layer):
    u = pl.program_id(0)
    is_sample = units.is_sample(u)
    is_prompt = jnp.logical_not(is_sample)
    L = CHUNK
    H = n_heads
    dqk = c_s.shape[1]
    dv = c_s.shape[2]

    @pl.when(jnp.logical_and(units.first(u), is_prompt))
    def _():
        c_s[...] = jnp.zeros_like(c_s)
        n_s[...] = jnp.zeros_like(n_s)
        m_s[...] = jnp.zeros_like(m_s)

    @pl.when(is_sample)
    def _():
        c_s[...] = c0_ref[...]
        n_s[...] = n0_ref[...]
        for hh in range(H):
            m_s[hh] = jnp.full(m_s.shape[1:], m0_ref[layer * units.n_sample + units.sample_index(u), hh], F32)

    gates = gate_ref[...]
    lane = lax.broadcasted_iota(jnp.int32, gates.shape, 1)
    row = lax.broadcasted_iota(jnp.int32, (L, L), 0)
    col = lax.broadcasted_iota(jnp.int32, (L, L), 1)
    eye = row == col
    tril = col <= row
    triu = row <= col

    heads = range(H)
    qs = [slice(hh * dqk, (hh + 1) * dqk) for hh in heads]
    vs = [slice(hh * dv, (hh + 1) * dv) for hh in heads]

    ig_col = [jnp.sum(jnp.where(lane == hh, gates, 0.0), axis=1, keepdims=True) + bi_ref[layer, hh] for hh in heads]
    lf_col = [_log_sigmoid(jnp.sum(jnp.where(lane == H + hh, gates, 0.0), axis=1, keepdims=True) + bf_ref[layer, hh])
              for hh in heads]
    ig_row = [jnp.sum(jnp.where(eye, ig_col[hh], 0.0), axis=0, keepdims=True) for hh in heads]
    lf_row = [jnp.sum(jnp.where(eye, lf_col[hh], 0.0), axis=0, keepdims=True) for hh in heads]
    b_col = [jnp.sum(jnp.where(tril, lf_row[hh], 0.0), axis=1, keepdims=True) for hh in heads]
    b_row = [jnp.sum(jnp.where(triu, lf_col[hh], 0.0), axis=0, keepdims=True) for hh in heads]
    m_prev = [m_s[hh][:, :1] for hh in heads]
    m_t, w_intra, w_inter = [], [], []
    for hh in heads:
        d = jnp.where(tril, b_col[hh] - b_row[hh] + ig_row[hh], NEG_INF)
        inter = b_col[hh] + m_prev[hh]
        m_c = jnp.maximum(inter, jnp.max(d, axis=1, keepdims=True))
        m_t.append(m_c)
        w_intra.append(jnp.exp(d - m_c))
        w_inter.append(jnp.exp(inter - m_c))

    a = [lax.dot_general(q_ref[:, qs[hh]].astype(BF16),
                         (k_ref[:, qs[hh]].astype(F32) * (dqk ** -0.5)).astype(BF16), NT,
                         preferred_element_type=F32) * w_intra[hh] for hh in heads]

    hval = []
    for hh in heads:
        q = q_ref[:, qs[hh]].astype(F32)
        num = (jnp.dot(a[hh].astype(BF16), v_ref[:, vs[hh]].astype(BF16), preferred_element_type=F32)
               + jnp.dot(q_ref[:, qs[hh]].astype(BF16), c_s[hh].astype(BF16), preferred_element_type=F32)
               * w_inter[hh])
        den = (jnp.sum(a[hh], axis=1, keepdims=True)
               + jnp.sum(q * n_s[hh], axis=1, keepdims=True) * w_inter[hh])
        den = jnp.maximum(jnp.abs(den), jnp.exp(-m_t[hh]))
        hval.append(num / den)

    for hh in heads:
        y = hval[hh] * lax.rsqrt(jnp.mean(hval[hh] * hval[hh], axis=1, keepdims=True) + EPS)
        gate_out = _sigmoid(og_ref[:, vs[hh]].astype(F32))
        hm_ref[:, vs[hh]] = (y * gh_ref[:, vs[hh]] * gate_out).astype(hm_ref.dtype)

    for hh in heads:
        k = k_ref[:, qs[hh]].astype(F32) * (dqk ** -0.5)
        m_new = m_t[hh][L - 1:L, :]
        b_last = b_col[hh][L - 1:L, :]
        w_last = jnp.exp(b_last - b_col[hh] + ig_col[hh] - m_new)
        decay = jnp.exp(b_last + m_prev[hh] - m_new)
        k_w = k * w_last
        c_s[hh] = decay * c_s[hh] + lax.dot_general(k_w.astype(BF16), v_ref[:, vs[hh]].astype(BF16), TN,
                                                    preferred_element_type=F32)
        n_s[hh] = decay * n_s[hh] + jnp.sum(k_w, axis=0, keepdims=True)
        m_s[hh] = jnp.broadcast_to(m_new, m_s.shape[1:])

    @pl.when(jnp.logical_and(units.last(u), is_prompt))
    def _():
        cp_ref[...] = c_s[...]
        np_ref[...] = n_s[...]
        mp_ref[...] = m_s[...]

    @pl.when(is_sample)
    def _():
        cs_ref[...] = c_s[...]
        ns_ref[...] = n_s[...]
        ms_ref[...] = m_s[...]


def mlstm_mixer(z, gates, b_i, b_f, g_head, c0, n0, m0, layer, units):
    T = z.shape[0]
    n_layers, DB, H, dqk, dv = c0.shape
    B = units.n_prompt
    qw = H * dqk
    vw = H * dv
    assert (2 * qw) % vw == 0
    v_blk = (2 * qw) // vw
    sidx = units.sample_index

    def pidx(u):
        return jnp.minimum(units.stream(u), B - 1)

    smem = pl.BlockSpec(memory_space=pltpu.SMEM)
    nbytes = (2 * (2 * CHUNK * qw * 2 + 2 * CHUNK * vw * 2 + CHUNK * LANES * 4 + CHUNK * vw * 2 + 3 * qw * dv * 4)
              + qw * dv * 4 * 4)

    def state_specs(idx):
        return [pl.BlockSpec((None, H, dqk, dv), lambda u: (idx(u), 0, 0, 0)),
                pl.BlockSpec((None, H, 1, dqk), lambda u: (idx(u), 0, 0, 0)),
                pl.BlockSpec((None, H, 1, LANES), lambda u: (idx(u), 0, 0, 0))]

    def state_shapes(n):
        return [jax.ShapeDtypeStruct((n, H, dqk, dv), F32), jax.ShapeDtypeStruct((n, H, 1, dqk), F32),
                jax.ShapeDtypeStruct((n, H, 1, LANES), F32)]

    hm, cp, np_, mp, cs, ns_, ms = pl.pallas_call(
        functools.partial(_mlstm_kernel, units=units, n_heads=H, layer=layer),
        grid=(units.total,),
        in_specs=[smem, smem, smem,
                  pl.BlockSpec((CHUNK, qw), lambda u: (u, 0)),
                  pl.BlockSpec((CHUNK, qw), lambda u: (u, 1)),
                  pl.BlockSpec((CHUNK, vw), lambda u: (u, v_blk)),
                  pl.BlockSpec((CHUNK, vw), lambda u: (u, v_blk + 1)),
                  pl.BlockSpec((CHUNK, LANES), lambda u: (u, 0)),
                  pl.BlockSpec((None, 1, vw), lambda u: (layer, 0, 0)),
                  pl.BlockSpec((None, None, H, dqk, dv), lambda u: (layer, sidx(u), 0, 0, 0)),
                  pl.BlockSpec((None, None, H, 1, dqk), lambda u: (layer, sidx(u), 0, 0, 0))],
        out_specs=[pl.BlockSpec((CHUNK, vw), lambda u: (u, 0))] + state_specs(pidx) + state_specs(sidx),
        out_shape=[jax.ShapeDtypeStruct((T, vw), BF16)] + state_shapes(B) + state_shapes(DB),
        scratch_shapes=[pltpu.VMEM((H, dqk, dv), F32), pltpu.VMEM((H, 1, dqk), F32), pltpu.VMEM((H, 1, LANES), F32)],
        compiler_params=_params(("arbitrary",), nbytes),
        name="mlstm",
    )(b_i, b_f, m0.reshape(n_layers * DB, H), z, z, z, z, gates, g_head[:, None, :], c0,
      n0.reshape(n_layers, DB, H, 1, dqk))
    return hm, (cp, np_[:, :, 0, :], mp[:, :, 0, 0]), (cs, ns_[:, :, 0, :], ms[:, :, 0, 0])


def _swa_kernel(sink_ref, q_ref, k0_ref, k1_ref, k2_ref, v0_ref, v1_ref, v2_ref, ck1_ref, ck2_ref, cv1_ref, cv2_ref,
                o_ref, bias_s, *, units, n_kv, layer):
    u = pl.program_id(0)
    is_sample = units.is_sample(u)
    L = CHUNK
    W = 2 * L
    NK = W + L
    hd = SWA_HEAD_DIM
    pairs = SWA_GROUP // 2
    R = pairs * L
    n_heads = n_kv * SWA_GROUP
    pair_of_row = lax.broadcasted_iota(jnp.int32, (R, 1), 0) // L

    @pl.when(u == 0)
    def _():
        jj = lax.broadcasted_iota(jnp.int32, (R, NK), 1)
        tt = lax.broadcasted_iota(jnp.int32, (R, NK), 0) % L
        dist = jnp.abs(tt + W - jj).astype(F32)
        for g in range(n_kv):
            for odd in range(2):
                head = g * SWA_GROUP + 2 * pair_of_row + odd
                slope = jnp.exp2(-8.0 * (head + 1).astype(F32) / n_heads) * LOG2E
                pen = slope * dist
                for v in range(3):
                    bias_s[v, 2 * g + odd] = jnp.where(jj >= W - v * L, pen, -NEG_INF)

    def pick(c_ref, z_ref):
        return jnp.where(is_sample, c_ref[...], z_ref[...])

    k_all = jnp.concatenate([pick(ck2_ref, k2_ref), pick(ck1_ref, k1_ref), k0_ref[...]], axis=0)
    v_all = jnp.concatenate([pick(cv2_ref, v2_ref), pick(cv1_ref, v1_ref), v0_ref[...]], axis=0)
    variant = jnp.where(is_sample, 2, jnp.minimum(u % units.ncp, 2))
    lane = lax.broadcasted_iota(jnp.int32, (NK, LANES), 1)
    lo_lanes = lane < hd

    k_half, v_half, qg = [], [], []
    for g in range(n_kv):
        tile = (g * hd) // LANES
        kt = k_all[:, tile * LANES:(tile + 1) * LANES]
        vt = v_all[:, tile * LANES:(tile + 1) * LANES]
        kr = pltpu.roll(kt, hd, axis=1)
        vr = pltpu.roll(vt, hd, axis=1)
        if (g * hd) % LANES == 0:
            k_lo, k_hi, v_lo, v_hi = kt, kr, vt, vr
        else:
            k_lo, k_hi, v_lo, v_hi = kr, kt, vr, vt
        k_half += [jnp.where(lo_lanes, k_lo, 0.0).astype(BF16), jnp.where(lo_lanes, 0.0, k_hi).astype(BF16)]
        v_half += [jnp.where(lo_lanes, v_lo, 0.0).astype(BF16), jnp.where(lo_lanes, 0.0, v_hi).astype(BF16)]
        q = jnp.concatenate(
            [q_ref[:, (g * pairs + p) * LANES:(g * pairs + p + 1) * LANES] for p in range(pairs)], axis=0)
        qg.append((q * (hd ** -0.5 * LOG2E)).astype(BF16))

    chains = [(g, odd) for g in range(n_kv) for odd in range(2)]
    s = [lax.dot_general(qg[g], k_half[2 * g + odd], NT, preferred_element_type=F32) - bias_s[variant, 2 * g + odd]
         for g, odd in chains]
    sink = []
    for g, odd in chains:
        col = jnp.zeros((R, 1), F32)
        for p in range(pairs):
            col = jnp.where(pair_of_row == p, sink_ref[layer, g * SWA_GROUP + 2 * p + odd] * LOG2E, col)
        sink.append(col)
    mx = [jnp.maximum(jnp.max(s[c], axis=1, keepdims=True), sink[c]) for c in range(len(chains))]
    e = [jnp.exp2(s[c] - mx[c]) for c in range(len(chains))]
    inv_den = [1.0 / (jnp.sum(e[c], axis=1, keepdims=True) + jnp.exp2(sink[c] - mx[c])) for c in range(len(chains))]
    pv = [jnp.dot(e[c].astype(BF16), v_half[c], preferred_element_type=F32) * inv_den[c] for c in range(len(chains))]
    for g in range(n_kv):
        o = pv[2 * g] + pv[2 * g + 1]
        for p in range(pairs):
            o_ref[:, (g * pairs + p) * LANES:(g * pairs + p + 1) * LANES] = o[p * L:(p + 1) * L].astype(o_ref.dtype)


def swa_mixer(z, sinks, cache_k, cache_v, layer, units, n_kv):
    T = z.shape[0]
    qw = n_kv * SWA_GROUP * SWA_HEAD_DIM
    kw = n_kv * SWA_HEAD_DIM
    assert kw % LANES == 0 and qw % kw == 0
    kb = qw // kw
    vb = kb + 1
    sidx = units.sample_index
    R = (SWA_GROUP // 2) * CHUNK
    NK = 3 * CHUNK

    def zspec(blk, back):
        return pl.BlockSpec((CHUNK, kw), lambda u: (jnp.maximum(u - back, 0), blk))

    def cspec(part):
        return pl.BlockSpec((None, CHUNK, kw), lambda u: (sidx(u), part, 0))

    bias_bytes = 3 * 2 * n_kv * R * 2 * LANES * 4
    nbytes = 2 * (CHUNK * qw * 4 + 10 * CHUNK * kw * 4 + CHUNK * qw * 2) + bias_bytes + 16 * R * NK * 4
    return pl.pallas_call(
        functools.partial(_swa_kernel, units=units, n_kv=n_kv, layer=layer),
        grid=(units.total,),
        in_specs=[pl.BlockSpec(memory_space=pltpu.SMEM),
                  pl.BlockSpec((CHUNK, qw), lambda u: (u, 0)),
                  zspec(kb, 0), zspec(kb, 1), zspec(kb, 2),
                  zspec(vb, 0), zspec(vb, 1), zspec(vb, 2),
                  cspec(1), cspec(0), cspec(1), cspec(0)],
        out_specs=pl.BlockSpec((CHUNK, qw), lambda u: (u, 0)),
        out_shape=jax.ShapeDtypeStruct((T, qw), BF16),
        scratch_shapes=[pltpu.VMEM((3, 2 * n_kv, R, NK), F32)],
        compiler_params=_params(("arbitrary",), nbytes),
        name="swa",
    )(sinks, z, z, z, z, z, z, z, cache_k, cache_k, cache_v, cache_v)


def _gla_kernel(q_ref, k_ref, v_ref, r_ref, glr_ref, wg_ref, bg_ref, gh_ref, s0_ref, o_ref, sp_ref, ss_ref, s_s, *,
                units):
    u = pl.program_id(1)
    is_sample = units.is_sample(u)
    is_prompt = jnp.logical_not(is_sample)
    L = CHUNK
    hb, dv, dk = s_s.shape

    @pl.when(jnp.logical_and(units.first(u), is_prompt))
    def _():
        s_s[...] = jnp.zeros_like(s_s)

    @pl.when(is_sample)
    def _():
        for hh in range(hb):
            s_s[hh] = s0_ref[hh].T

    glr = glr_ref[...].astype(BF16)
    row = lax.broadcasted_iota(jnp.int32, (L, L), 0)
    col = lax.broadcasted_iota(jnp.int32, (L, L), 1)
    tril = col <= row
    ones_tril = jnp.where(tril, 1.0, 0.0).astype(BF16)
    rr = lax.broadcasted_iota(jnp.int32, (L, dk), 0)
    n_sub = L // GLA_SUB

    heads = range(hb)
    ks = [slice(hh * dk, (hh + 1) * dk) for hh in heads]
    vs = [slice(hh * dv, (hh + 1) * dv) for hh in heads]
    lg = [_log_sigmoid(jnp.dot(glr, wg_ref[:, ks[hh]], preferred_element_type=F32) + bg_ref[:, ks[hh]])
          * (1.0 / GLA_TAU) for hh in heads]
    bc = [sum(jnp.dot(ones_tril, part, preferred_element_type=F32) for part in _split3(lg[hh])) for hh in heads]

    refs = [[bc[hh][i * GLA_SUB:i * GLA_SUB + 1, :] for i in range(n_sub)] for hh in heads]
    q_t = []
    for hh in heads:
        r_q = refs[hh][n_sub - 1]
        for i in range(n_sub - 2, -1, -1):
            r_q = jnp.where(rr < (i + 1) * GLA_SUB, refs[hh][i], r_q)
        q_t.append((q_ref[:, ks[hh]].astype(F32) * (dk ** -0.5) * jnp.exp(bc[hh] - r_q)).astype(BF16))
    blocks = [[] for _ in heads]
    for i in range(n_sub):
        for hh in heads:
            e_i = jnp.where(rr < (i + 1) * GLA_SUB, refs[hh][i] - bc[hh], NEG_INF)
            k_i = (k_ref[:, ks[hh]].astype(F32) * jnp.exp(e_i)).astype(BF16)
            blocks[hh].append(lax.dot_general(q_t[hh][i * GLA_SUB:(i + 1) * GLA_SUB], k_i, NT,
                                              preferred_element_type=F32))
    a = [jnp.where(tril, jnp.concatenate(blocks[hh], axis=0), 0.0).astype(BF16) for hh in heads]

    o = []
    for hh in heads:
        q_dec = (q_ref[:, ks[hh]].astype(F32) * (dk ** -0.5) * jnp.exp(bc[hh])).astype(BF16)
        o.append(jnp.dot(a[hh], v_ref[:, vs[hh]].astype(BF16), preferred_element_type=F32)
                 + lax.dot_general(q_dec, s_s[hh].astype(BF16), NT, preferred_element_type=F32))

    for hh in heads:
        b_last = bc[hh][L - 1:L, :]
        k_dec = (k_ref[:, ks[hh]].astype(F32) * jnp.exp(b_last - bc[hh])).astype(BF16)
        s_s[hh] = jnp.exp(b_last) * s_s[hh] + lax.dot_general(v_ref[:, vs[hh]].astype(BF16), k_dec, TN,
                                                               preferred_element_type=F32)

    for hh in heads:
        y = o[hh] * lax.rsqrt(jnp.mean(o[hh] * o[hh], axis=1, keepdims=True) + EPS)
        r = r_ref[:, vs[hh]].astype(F32)
        o_ref[:, vs[hh]] = (y * gh_ref[:, vs[hh]] * (r * _sigmoid(r))).astype(o_ref.dtype)

    @pl.when(jnp.logical_and(units.last(u), is_prompt))
    def _():
        for hh in range(hb):
            sp_ref[hh] = s_s[hh].T

    @pl.when(is_sample)
    def _():
        for hh in range(hb):
            ss_ref[hh] = s_s[hh].T


def gla_mixer(z, glr, w_g2, b_g, g_head, s0, layer, units):
    T = z.shape[0]
    _, DB, H, dk, dv = s0.shape
    B = units.n_prompt
    hb = GLA_HEADS_PER_STEP if H % GLA_HEADS_PER_STEP == 0 else 1
    ng = H // hb
    kw = hb * dk
    vw = hb * dv
    assert (2 * H * dk) % vw == 0
    v_blk0 = (2 * H * dk) // vw
    sidx = units.sample_index

    def pidx(u):
        return jnp.minimum(units.stream(u), B - 1)

    nbytes = (2 * (2 * CHUNK * kw * 2 + 2 * CHUNK * vw * 2 + CHUNK * LANES * 4 + LANES * kw * 2 + CHUNK * vw * 2
                   + 3 * kw * dv * 4) + kw * dv * 4 * 4)
    return pl.pallas_call(
        functools.partial(_gla_kernel, units=units),
        grid=(ng, units.total),
        in_specs=[pl.BlockSpec((CHUNK, kw), lambda g, u: (u, g)),
                  pl.BlockSpec((CHUNK, kw), lambda g, u: (u, ng + g)),
                  pl.BlockSpec((CHUNK, vw), lambda g, u: (u, v_blk0 + g)),
                  pl.BlockSpec((CHUNK, vw), lambda g, u: (u, v_blk0 + ng + g)),
                  pl.BlockSpec((CHUNK, LANES), lambda g, u: (u, 0)),
                  pl.BlockSpec((LANES, kw), lambda g, u: (0, g)),
                  pl.BlockSpec((None, 1, kw), lambda g, u: (layer, 0, g)),
                  pl.BlockSpec((None, 1, vw), lambda g, u: (layer, 0, g)),
                  pl.BlockSpec((None, None, hb, dk, dv), lambda g, u: (layer, sidx(u), g, 0, 0))],
        out_specs=[pl.BlockSpec((CHUNK, vw), lambda g, u: (u, g)),
                   pl.BlockSpec((None, hb, dk, dv), lambda g, u: (pidx(u), g, 0, 0)),
                   pl.BlockSpec((None, hb, dk, dv), lambda g, u: (sidx(u), g, 0, 0))],
        out_shape=[jax.ShapeDtypeStruct((T, H * dv), BF16),
                   jax.ShapeDtypeStruct((B, H, dk, dv), F32),
                   jax.ShapeDtypeStruct((DB, H, dk, dv), F32)],
        scratch_shapes=[pltpu.VMEM((hb, dv, dk), F32)],
        compiler_params=_params(("parallel", "arbitrary"), nbytes),
        name="gla",
    )(z, z, z, z, glr, w_g2, b_g[:, None, :], g_head[:, None, :], s0)


def _pad_rows(w, height):
    return jnp.pad(w, ((0, height - w.shape[0]), (0, 0)))


def kernel(x_prompt, x_sample, cache_swa_k, cache_swa_v, state_mlstm_C, state_mlstm_n, state_mlstm_m, state_gla_S,
           norm_mix, norm_ffn, norm_final, w_even_in, b_mlstm_i, b_mlstm_f, mlstm_head_norm, swa_sinks, w_even_out,
           w_odd_in, w_gla_gate, b_gla_gate, gla_head_norm, w_odd_out, w_ffn_in, w_ffn_out):
    B, S, D = x_prompt.shape
    DB, DS, _ = x_sample.shape
    depth = norm_mix.shape[0]
    assert DS == CHUNK and S % CHUNK == 0
    MH, dqk, dv = state_mlstm_C.shape[2:]
    n_kv = cache_swa_k.shape[3]
    w_buf = cache_swa_k.shape[2]
    assert w_buf == 2 * CHUNK and cache_swa_k.shape[4] == SWA_HEAD_DIM
    GH, gdk, gdv = state_gla_S.shape[2:]
    rank = w_gla_gate.shape[1]
    m_w = 2 * MH * dqk + 2 * MH * dv
    sq_w = n_kv * SWA_GROUP * SWA_HEAD_DIM
    kvw = n_kv * SWA_HEAD_DIM
    s_w = sq_w + 2 * kvw
    g_w = 2 * GH * gdk + 2 * GH * gdv
    units = Units(B, S // CHUNK, DB)
    TP = B * S

    assert depth >= 1
    w_even_in_t = jnp.swapaxes(w_even_in, 1, 2)
    w_odd_in_t = jnp.swapaxes(w_odd_in, 1, 2)

    p_k, p_v, p_C, p_n, p_m, p_S = [], [], [], [], [], []
    s_k, s_v, s_C, s_n, s_m, s_S = [], [], [], [], [], []
    for l in range(depth):
        e = l // 2
        if l % 2 == 0:
            w_gate = _pad_rows(w_even_in_t[e, m_w:m_w + 2 * MH], LANES)[None]
            w_attn = w_even_in_t[e, m_w + 2 * MH:][None]
            if l == 0:
                x, h = stack_and_norm(x_prompt.reshape(TP, D), x_sample.reshape(DB * DS, D), norm_mix, 0)
            else:
                h = rmsnorm(x, norm_mix, l, BF16)
            zm = project([h], w_even_in_t, e, m_w, BF16, w_nk=True, name="even_in_mlstm")
            zs = project([h], w_attn, 0, s_w, w_nk=True, name="even_in_attn")
            gates = project([h], w_gate, 0, LANES, w_nk=True, name="even_gates")
            hm, (pc, pn, pm), (sc, sn, sm) = mlstm_mixer(
                zm, gates, b_mlstm_i, b_mlstm_f, mlstm_head_norm, state_mlstm_C, state_mlstm_n, state_mlstm_m,
                e, units)
            ck = cache_swa_k[e].reshape(DB, w_buf, kvw)
            cv = cache_swa_v[e].reshape(DB, w_buf, kvw)
            hs = swa_mixer(zs, swa_sinks, ck, cv, e, units, n_kv)
            x = project([hm, hs], w_even_out, e, D, residual=x, name="even_out")
            zk = zs[:, sq_w:sq_w + kvw]
            zv = zs[:, sq_w + kvw:]
            kv_shape = (w_buf, n_kv, SWA_HEAD_DIM)
            p_k.append(zk[:TP].reshape(B, S, kvw)[:, -w_buf:].reshape((B,) + kv_shape))
            p_v.append(zv[:TP].reshape(B, S, kvw)[:, -w_buf:].reshape((B,) + kv_shape))
            s_k.append(jnp.concatenate([ck[:, CHUNK:], zk[TP:].reshape(DB, DS, kvw)], axis=1).reshape((DB,) + kv_shape))
            s_v.append(jnp.concatenate([cv[:, CHUNK:], zv[TP:].reshape(DB, DS, kvw)], axis=1).reshape((DB,) + kv_shape))
            p_C.append(pc); p_n.append(pn); p_m.append(pm)
            s_C.append(sc); s_n.append(sn); s_m.append(sm)
        else:
            z, h = normed_project(x, norm_mix, l, w_odd_in_t, e, g_w, BF16, w_nk=True, emit_h=True,
                                  name="odd_in")
            w_rank = _pad_rows(w_odd_in_t[e, g_w:], LANES)[None]
            glr = project([h], w_rank, 0, LANES, w_nk=True, name="odd_gate_rank")
            w_g2 = jnp.pad(w_gla_gate[e], ((0, LANES - rank), (0, 0))).astype(BF16)
            mix, ps, ss = gla_mixer(z, glr, w_g2, b_gla_gate, gla_head_norm, state_gla_S, e, units)
            x = project([mix], w_odd_out, e, D, residual=x, name="odd_out")
            p_S.append(ps); s_S.append(ss)
        act = normed_project(x, norm_ffn, l, w_ffn_in, l, w_ffn_in.shape[2] // 2, BF16, swiglu=True, name="ffn_in")
        x = project([act], w_ffn_out, l, D, residual=x, wide_k=True, name="ffn_out")
    y_prompt = rmsnorm(x, norm_final[None], 0, F32, 0, TP)
    y_sample = rmsnorm(x, norm_final[None], 0, F32, TP, DB * DS)
    return (y_prompt.reshape(B, S, D), y_sample.reshape(DB, DS, D),
            jnp.stack(p_k), jnp.stack(p_v), jnp.stack(p_C), jnp.stack(p_n), jnp.stack(p_m), jnp.stack(p_S),
            jnp.stack(s_k), jnp.stack(s_v), jnp.stack(s_C), jnp.stack(s_n), jnp.stack(s_m), jnp.stack(s_S))
```

```python
import functools

import jax
import jax.numpy as jnp
from jax import lax
from jax.experimental import pallas as pl
from jax.experimental.pallas import tpu as pltpu

F32 = jnp.float32
BF16 = jnp.bfloat16

CHUNK = 64
SWA_HEAD_DIM = 64
SWA_GROUP = 8
LANES = 128
GLA_SUB = 16
GLA_HEADS_PER_STEP = 8
GLA_TAU = 16.0
LOG2E = 1.4426950408889634
EPS = 1e-6
NEG_INF = float("-inf")

V7X_VMEM_BYTES = 64 * 1024 * 1024
VMEM_CAP_BYTES = V7X_VMEM_BYTES - 4 * 1024 * 1024

NT = (((1,), (1,)), ((), ()))
TN = (((0,), (0,)), ((), ()))


def _pick(n, cands):
    for c in cands:
        if n % c == 0:
            return c
    raise ValueError(f"no tile in {cands} divides {n}")


def _params(semantics, block_bytes):
    limit = min(VMEM_CAP_BYTES, block_bytes + 16 * 1024 * 1024)
    return pltpu.CompilerParams(dimension_semantics=semantics, vmem_limit_bytes=int(limit))


def _log_sigmoid(x):
    return jnp.minimum(x, 0.0) - jnp.log(1.0 + jnp.exp(-jnp.abs(x)))


def _sigmoid(x):
    return 1.0 / (1.0 + jnp.exp(-x))


def _split3(x):
    hi = x.astype(BF16)
    r1 = x - hi.astype(F32)
    mid = r1.astype(BF16)
    lo = (r1 - mid.astype(F32)).astype(BF16)
    return hi, mid, lo


def _rmsnorm_kernel(x_ref, g_ref, o_ref):
    x = x_ref[...]
    y = x * lax.rsqrt(jnp.mean(x * x, axis=-1, keepdims=True) + EPS)
    o_ref[...] = (y * g_ref[...]).astype(o_ref.dtype)


def rmsnorm(x, g, layer, out_dtype, row0=0, rows=None):
    D = x.shape[1]
    T = x.shape[0] if rows is None else rows
    tr = _pick(T, (512, 272, 256, 160, 128, 64))
    assert row0 % tr == 0
    blk0 = row0 // tr
    nbytes = 2 * tr * D * (4 + jnp.dtype(out_dtype).itemsize)
    return pl.pallas_call(
        _rmsnorm_kernel,
        grid=(T // tr,),
        in_specs=[pl.BlockSpec((tr, D), lambda i: (blk0 + i, 0)), pl.BlockSpec((None, 1, D), lambda i: (layer, 0, 0))],
        out_specs=pl.BlockSpec((tr, D), lambda i: (i, 0)),
        out_shape=jax.ShapeDtypeStruct((T, D), out_dtype),
        compiler_params=_params(("parallel",), nbytes),
        name="rmsnorm",
    )(x, g[:, None, :])


def _stack_norm_kernel(xp_ref, xs_ref, g_ref, x_ref, h_ref, *, n_prompt_tiles):
    x = jnp.where(pl.program_id(0) < n_prompt_tiles, xp_ref[...], xs_ref[...])
    x_ref[...] = x
    y = x * lax.rsqrt(jnp.mean(x * x, axis=-1, keepdims=True) + EPS)
    h_ref[...] = (y * g_ref[...]).astype(h_ref.dtype)


def stack_and_norm(x_prompt, x_sample, g, layer):
    TP, D = x_prompt.shape
    TS = x_sample.shape[0]
    tr = next(c for c in (256, 128, 64) if TP % c == 0 and TS % c == 0)
    n_p = TP // tr
    T = TP + TS
    nbytes = 2 * tr * D * (4 + 4 + 4 + 2)
    return pl.pallas_call(
        functools.partial(_stack_norm_kernel, n_prompt_tiles=n_p),
        grid=(T // tr,),
        in_specs=[pl.BlockSpec((tr, D), lambda i: (jnp.minimum(i, n_p - 1), 0)),
                  pl.BlockSpec((tr, D), lambda i: (jnp.maximum(i - n_p, 0), 0)),
                  pl.BlockSpec((None, 1, D), lambda i: (layer, 0, 0))],
        out_specs=[pl.BlockSpec((tr, D), lambda i: (i, 0)), pl.BlockSpec((tr, D), lambda i: (i, 0))],
        out_shape=[jax.ShapeDtypeStruct((T, D), F32), jax.ShapeDtypeStruct((T, D), BF16)],
        compiler_params=_params(("parallel",), nbytes),
        name="stack_norm",
    )(x_prompt, x_sample, g[:, None, :])


def _m_tile(T):
    return _pick(T, (1088, 1024, 640, 512, 320, 256, 128, 64))


def _proj_kernel(*refs, n_x, has_res, w_nk, has_cast):
    x_refs = refs[:n_x]
    w_ref = refs[n_x]
    r_ref = refs[n_x + 1] if has_res else None
    o_ref = refs[n_x + 1 + has_res + has_cast]
    if has_cast:
        refs[-1][...] = refs[n_x + 1 + has_res][...].astype(BF16)
    k0 = 0
    acc = None
    for x_ref in x_refs:
        kx = x_ref.shape[1]
        if w_nk:
            part = lax.dot_general(x_ref[...], w_ref[:, k0:k0 + kx].astype(BF16), NT, preferred_element_type=F32)
        else:
            part = jnp.dot(x_ref[...], w_ref[k0:k0 + kx, :].astype(BF16), preferred_element_type=F32)
        acc = part if acc is None else acc + part
        k0 += kx
    if has_res:
        acc = r_ref[...] + acc
    o_ref[...] = acc.astype(o_ref.dtype)


def _cast_rows(n_rows, n_steps):
    return min(r for r in range(16, n_rows + 1, 16) if n_rows % r == 0 and n_rows // r <= n_steps)


def project(xs, w, layer, n_cols, out_dtype=F32, residual=None, w_nk=False, half_rows=False, cast=None,
            name="project"):
    T = xs[0].shape[0]
    K = w.shape[2] if w_nk else w.shape[1]
    assert sum(x.shape[1] for x in xs) == K
    tm = _m_tile(T)
    if half_rows and tm % 32 == 0:
        tm //= 2
    tn = _pick(n_cols, (512, 256, 128))
    nj = n_cols // tn
    osz = jnp.dtype(out_dtype).itemsize
    wsz = jnp.dtype(w.dtype).itemsize
    nbytes = 2 * (tm * K * 2 + K * tn * wsz + tm * tn * osz) + K * tn * 2 + tm * tn * 4
    in_specs = [pl.BlockSpec((tm, x.shape[1]), lambda i, j: (i, 0)) for x in xs]
    if w_nk:
        in_specs.append(pl.BlockSpec((None, tn, K), lambda i, j: (layer, j, 0)))
    else:
        in_specs.append(pl.BlockSpec((None, K, tn), lambda i, j: (layer, 0, j)))
    args = list(xs) + [w]
    if residual is not None:
        in_specs.append(pl.BlockSpec((tm, tn), lambda i, j: (i, j)))
        args.append(residual)
        nbytes += 2 * tm * tn * 4
    out_specs = [pl.BlockSpec((tm, tn), lambda i, j: (i, j))]
    out_shape = [jax.ShapeDtypeStruct((T, n_cols), out_dtype)]
    if cast is not None:
        src, src_layer = cast
        n_rows, width = src.shape[1:]
        crows = _cast_rows(n_rows, (T // tm) * nj)
        last = n_rows // crows - 1
        in_specs.append(pl.BlockSpec((None, crows, width), lambda i, j: (src_layer, jnp.minimum(i * nj + j, last), 0)))
        args.append(src)
        out_specs.append(pl.BlockSpec((crows, width), lambda i, j: (jnp.minimum(i * nj + j, last), 0)))
        out_shape.append(jax.ShapeDtypeStruct((n_rows, width), BF16))
        nbytes += 2 * crows * width * 6
    outs = pl.pallas_call(
        functools.partial(_proj_kernel, n_x=len(xs), has_res=residual is not None, w_nk=w_nk,
                          has_cast=cast is not None),
        grid=(T // tm, nj),
        in_specs=in_specs,
        out_specs=out_specs,
        out_shape=out_shape,
        compiler_params=_params(("parallel" if cast is None else "arbitrary", "arbitrary"), nbytes),
        name=name,
    )(*args)
    return outs if cast is not None else outs[0]


def _normed_kernel(*refs, n_slabs, slab_rows, n_w, w_nk, emit_h, swiglu, has_cast):
    x_ref, g_ref = refs[:2]
    w_refs = refs[2:2 + n_w]
    src_ref = refs[2 + n_w] if has_cast else None
    o_ref = refs[2 + n_w + has_cast]
    hout_ref = refs[3 + n_w + has_cast] if emit_h else None
    dst_ref = refs[-2] if has_cast else None
    h2_ref = refs[-1]
    t = pl.program_id(0)
    slab = jnp.where(t < pl.num_programs(0) - 1, jnp.minimum(pl.program_id(1), n_slabs - 1), n_slabs - 1)

    def norm_slab(buf):
        x = x_ref[...]
        y = x * lax.rsqrt(jnp.mean(x * x, axis=-1, keepdims=True) + EPS)
        yb = (y * g_ref[...]).astype(BF16)
        h2_ref[buf, pl.ds(pl.multiple_of(slab * slab_rows, slab_rows), slab_rows), :] = yb
        if emit_h:
            hout_ref[...] = yb

    def matmul(h, w_ref):
        if w_nk:
            return lax.dot_general(h, w_ref[...].astype(BF16), NT, preferred_element_type=F32)
        return jnp.dot(h, w_ref[...].astype(BF16), preferred_element_type=F32)

    @pl.when(t == 0)
    def _():
        norm_slab(0)

    @pl.when(t > 0)
    def _():
        h = h2_ref[(t - 1) % 2]
        if swiglu:
            g = matmul(h, w_refs[0])
            u = matmul(h, w_refs[1])
            o_ref[...] = (g * _sigmoid(g) * u).astype(o_ref.dtype)
        else:
            o_ref[...] = matmul(h, w_refs[0]).astype(o_ref.dtype)
        norm_slab(t % 2)
        if has_cast:
            dst_ref[...] = src_ref[...].astype(BF16)


def _slab_count(tm, nj):
    units16 = tm // 16
    return max(d for d in range(1, units16 + 1) if units16 % d == 0 and d <= nj)


def normed_project(x, g, g_layer, w, layer, n_cols, out_dtype, w_nk=False, swiglu=False, emit_h=False, cast=None,
                   name="normed_project"):
    T, K = x.shape
    tm = _pick(T, (2176, 1088, 640, 512, 256, 128, 64)) if swiglu else _m_tile(T)
    cands = [c for c in ((256, 128) if swiglu else (512, 256, 128)) if n_cols % c == 0]
    tn = next((c for c in cands if n_cols // c >= tm // CHUNK), cands[-1])
    nj = n_cols // tn
    n_tiles = T // tm
    ns = _slab_count(tm, nj)
    rows = tm // ns
    osz = jnp.dtype(out_dtype).itemsize
    n_w = 2 if swiglu else 1

    def row_tile(t):
        return jnp.maximum(t - 1, 0)

    def col(t, j):
        return jnp.where(t == 0, 0, j)

    def slab_idx(t, j):
        return jnp.where(t < n_tiles, t * ns + jnp.minimum(j, ns - 1), n_tiles * ns - 1)

    if w_nk:
        w_specs = [pl.BlockSpec((None, tn, K), lambda t, j, o=o: (layer, col(t, j) + o, 0)) for o in range(n_w)]
    else:
        w_specs = [pl.BlockSpec((None, K, tn), lambda t, j, o=o: (layer, 0, col(t, j) + o * nj)) for o in range(n_w)]
    out_specs = [pl.BlockSpec((tm, tn), lambda t, j: (row_tile(t), col(t, j)))]
    out_shape = [jax.ShapeDtypeStruct((T, n_cols), out_dtype)]
    if emit_h:
        out_specs.append(pl.BlockSpec((rows, K), lambda t, j: (slab_idx(t, j), 0)))
        out_shape.append(jax.ShapeDtypeStruct((T, K), BF16))
    nbytes = (2 * tm * K * 2 + 2 * rows * K * (4 + 2 * emit_h) + n_w * (2 * K * tn * 4 + K * tn * 2)
              + 2 * tm * tn * osz + (1 + n_w) * tm * tn * 4)
    args = [x, g[:, None, :]] + [w] * n_w
    in_specs = [pl.BlockSpec((rows, K), lambda t, j: (slab_idx(t, j), 0)),
                pl.BlockSpec((None, 1, K), lambda t, j: (g_layer, 0, 0))] + w_specs
    if cast is not None:
        src, src_layer = cast
        n_rows, width = src.shape[1:]
        crows = _cast_rows(n_rows, n_tiles * nj)
        last = n_rows // crows - 1

        def cast_idx(t, j):
            return jnp.clip((t - 1) * nj + j, 0, last)

        in_specs.append(pl.BlockSpec((None, crows, width), lambda t, j: (src_layer, cast_idx(t, j), 0)))
        args.append(src)
        out_specs.append(pl.BlockSpec((crows, width), lambda t, j: (cast_idx(t, j), 0)))
        out_shape.append(jax.ShapeDtypeStruct((n_rows, width), BF16))
        nbytes += 2 * crows * width * 6
    outs = pl.pallas_call(
        functools.partial(_normed_kernel, n_slabs=ns, slab_rows=rows, n_w=n_w, w_nk=w_nk, emit_h=emit_h,
                          swiglu=swiglu, has_cast=cast is not None),
        grid=(n_tiles + 1, nj),
        in_specs=in_specs,
        out_specs=out_specs,
        out_shape=out_shape,
        scratch_shapes=[pltpu.VMEM((2, tm, K), BF16)],
        compiler_params=_params(("arbitrary", "arbitrary"), nbytes),
        name=name,
    )(*args)
    return outs if len(outs) > 1 else outs[0]


class Units:
    def __init__(self, n_prompt, chunks_per_prompt, n_sample):
        self.ncp = chunks_per_prompt
        self.up = n_prompt * chunks_per_prompt
        self.n_prompt = n_prompt
        self.n_sample = n_sample
        self.total = self.up + n_sample
        self.n_streams = n_prompt + n_sample

    def is_sample(self, u):
        return u >= self.up

    def stream(self, u):
        return jnp.where(u < self.up, u // self.ncp, self.n_prompt + u - self.up)

    def sample_index(self, u):
        return jnp.maximum(u - self.up, 0)

    def first(self, u):
        return jnp.logical_or(u >= self.up, u % self.ncp == 0)

    def last(self, u):
        return jnp.logical_or(u >= self.up, u % self.ncp == self.ncp - 1)


def _mlstm_kernel(bi_ref, bf_ref, m0_ref, q_ref, k_ref, v_ref, og_ref, gate_ref, gh_ref, c0_ref, n0_ref,
                  hm_ref, cp_ref, np_ref, mp_ref, cs_ref, ns_ref, ms_ref, c_s, n_s, m_s, *, units, n_heads, layer):
    u = pl.program_id(0)
    is_sample = units.is_sample(u)
    is_prompt = jnp.logical_not(is_sample)
    L = CHUNK
    H = n_heads
    dqk = c_s.shape[1]
    dv = c_s.shape[2]

    @pl.when(jnp.logical_and(units.first(u), is_prompt))
    def _():
        c_s[...] = jnp.zeros_like(c_s)
        n_s[...] = jnp.zeros_like(n_s)
        m_s[...] = jnp.zeros_like(m_s)

    @pl.when(is_sample)
    def _():
        c_s[...] = c0_ref[...]
        n_s[...] = n0_ref[...]
        for hh in range(H):
            m_s[hh] = jnp.full(m_s.shape[1:], m0_ref[layer * units.n_sample + units.sample_index(u), hh], F32)

    gates = gate_ref[...]
    lane = lax.broadcasted_iota(jnp.int32, gates.shape, 1)
    row = lax.broadcasted_iota(jnp.int32, (L, L), 0)
    col = lax.broadcasted_iota(jnp.int32, (L, L), 1)
    eye = row == col
    tril = col <= row
    triu = row <= col

    heads = range(H)
    qs = [slice(hh * dqk, (hh + 1) * dqk) for hh in heads]
    vs = [slice(hh * dv, (hh + 1) * dv) for hh in heads]

    ig_col = [jnp.sum(jnp.where(lane == hh, gates, 0.0), axis=1, keepdims=True) + bi_ref[layer, hh] for hh in heads]
    lf_col = [_log_sigmoid(jnp.sum(jnp.where(lane == H + hh, gates, 0.0), axis=1, keepdims=True) + bf_ref[layer, hh])
              for hh in heads]
    ig_row = [jnp.sum(jnp.where(eye, ig_col[hh], 0.0), axis=0, keepdims=True) for hh in heads]
    lf_row = [jnp.sum(jnp.where(eye, lf_col[hh], 0.0), axis=0, keepdims=True) for hh in heads]
    b_col = [jnp.sum(jnp.where(tril, lf_row[hh], 0.0), axis=1, keepdims=True) for hh in heads]
    b_row = [jnp.sum(jnp.where(triu, lf_col[hh], 0.0), axis=0, keepdims=True) for hh in heads]
    m_prev = [m_s[hh][:, :1] for hh in heads]
    m_t, w_intra, w_inter = [], [], []
    for hh in heads:
        d = jnp.where(tril, b_col[hh] - b_row[hh] + ig_row[hh], NEG_INF)
        inter = b_col[hh] + m_prev[hh]
        m_c = jnp.maximum(inter, jnp.max(d, axis=1, keepdims=True))
        m_t.append(m_c)
        w_intra.append(jnp.exp(d - m_c))
        w_inter.append(jnp.exp(inter - m_c))

    a = [lax.dot_general(q_ref[:, qs[hh]].astype(BF16),
                         (k_ref[:, qs[hh]].astype(F32) * (dqk ** -0.5)).astype(BF16), NT,
                         preferred_element_type=F32) * w_intra[hh] for hh in heads]

    hval = []
    for hh in heads:
        q = q_ref[:, qs[hh]].astype(F32)
        num = (jnp.dot(a[hh].astype(BF16), v_ref[:, vs[hh]].astype(BF16), preferred_element_type=F32)
               + jnp.dot(q_ref[:, qs[hh]].astype(BF16), c_s[hh].astype(BF16), preferred_element_type=F32)
               * w_inter[hh])
        den = (jnp.sum(a[hh], axis=1, keepdims=True)
               + jnp.sum(q * n_s[hh], axis=1, keepdims=True) * w_inter[hh])
        den = jnp.maximum(jnp.abs(den), jnp.exp(-m_t[hh]))
        hval.append(num / den)

    for hh in heads:
        y = hval[hh] * lax.rsqrt(jnp.mean(hval[hh] * hval[hh], axis=1, keepdims=True) + EPS)
        gate_out = _sigmoid(og_ref[:, vs[hh]].astype(F32))
        hm_ref[:, vs[hh]] = (y * gh_ref[:, vs[hh]] * gate_out).astype(hm_ref.dtype)

    for hh in heads:
        k = k_ref[:, qs[hh]].astype(F32) * (dqk ** -0.5)
        m_new = m_t[hh][L - 1:L, :]
        b_last = b_col[hh][L - 1:L, :]
        w_last = jnp.exp(b_last - b_col[hh] + ig_col[hh] - m_new)
        decay = jnp.exp(b_last + m_prev[hh] - m_new)
        k_w = k * w_last
        c_s[hh] = decay * c_s[hh] + lax.dot_general(k_w.astype(BF16), v_ref[:, vs[hh]].astype(BF16), TN,
                                                    preferred_element_type=F32)
        n_s[hh] = decay * n_s[hh] + jnp.sum(k_w, axis=0, keepdims=True)
        m_s[hh] = jnp.broadcast_to(m_new, m_s.shape[1:])

    @pl.when(jnp.logical_and(units.last(u), is_prompt))
    def _():
        cp_ref[...] = c_s[...]
        np_ref[...] = n_s[...]
        mp_ref[...] = m_s[...]

    @pl.when(is_sample)
    def _():
        cs_ref[...] = c_s[...]
        ns_ref[...] = n_s[...]
        ms_ref[...] = m_s[...]


def mlstm_mixer(z, gates, b_i, b_f, g_head, c0, n0, m0, layer, units):
    T = z.shape[0]
    n_layers, DB, H, dqk, dv = c0.shape
    B = units.n_prompt
    qw = H * dqk
    vw = H * dv
    assert (2 * qw) % vw == 0
    v_blk = (2 * qw) // vw
    sidx = units.sample_index

    def pidx(u):
        return jnp.minimum(units.stream(u), B - 1)

    smem = pl.BlockSpec(memory_space=pltpu.SMEM)
    nbytes = (2 * (2 * CHUNK * qw * 2 + 2 * CHUNK * vw * 2 + CHUNK * LANES * 4 + CHUNK * vw * 2 + 3 * qw * dv * 4)
              + qw * dv * 4 * 4)

    def state_specs(idx):
        return [pl.BlockSpec((None, H, dqk, dv), lambda u: (idx(u), 0, 0, 0)),
                pl.BlockSpec((None, H, 1, dqk), lambda u: (idx(u), 0, 0, 0)),
                pl.BlockSpec((None, H, 1, LANES), lambda u: (idx(u), 0, 0, 0))]

    def state_shapes(n):
        return [jax.ShapeDtypeStruct((n, H, dqk, dv), F32), jax.ShapeDtypeStruct((n, H, 1, dqk), F32),
                jax.ShapeDtypeStruct((n, H, 1, LANES), F32)]

    hm, cp, np_, mp, cs, ns_, ms = pl.pallas_call(
        functools.partial(_mlstm_kernel, units=units, n_heads=H, layer=layer),
        grid=(units.total,),
        in_specs=[smem, smem, smem,
                  pl.BlockSpec((CHUNK, qw), lambda u: (u, 0)),
                  pl.BlockSpec((CHUNK, qw), lambda u: (u, 1)),
                  pl.BlockSpec((CHUNK, vw), lambda u: (u, v_blk)),
                  pl.BlockSpec((CHUNK, vw), lambda u: (u, v_blk + 1)),
                  pl.BlockSpec((CHUNK, LANES), lambda u: (u, 0)),
                  pl.BlockSpec((None, 1, vw), lambda u: (layer, 0, 0)),
                  pl.BlockSpec((None, None, H, dqk, dv), lambda u: (layer, sidx(u), 0, 0, 0)),
                  pl.BlockSpec((None, None, H, 1, dqk), lambda u: (layer, sidx(u), 0, 0, 0))],
        out_specs=[pl.BlockSpec((CHUNK, vw), lambda u: (u, 0))] + state_specs(pidx) + state_specs(sidx),
        out_shape=[jax.ShapeDtypeStruct((T, vw), BF16)] + state_shapes(B) + state_shapes(DB),
        scratch_shapes=[pltpu.VMEM((H, dqk, dv), F32), pltpu.VMEM((H, 1, dqk), F32), pltpu.VMEM((H, 1, LANES), F32)],
        compiler_params=_params(("arbitrary",), nbytes),
        name="mlstm",
    )(b_i, b_f, m0.reshape(n_layers * DB, H), z, z, z, z, gates, g_head[:, None, :], c0,
      n0.reshape(n_layers, DB, H, 1, dqk))
    return hm, (cp, np_[:, :, 0, :], mp[:, :, 0, 0]), (cs, ns_[:, :, 0, :], ms[:, :, 0, 0])


def _swa_kernel(sink_ref, q_ref, k0_ref, k1_ref, k2_ref, v0_ref, v1_ref, v2_ref, ck1_ref, ck2_ref, cv1_ref, cv2_ref,
                o_ref, bias_s, *, units, n_kv, layer):
    u = pl.program_id(0)
    is_sample = units.is_sample(u)
    L = CHUNK
    W = 2 * L
    NK = W + L
    hd = SWA_HEAD_DIM
    pairs = SWA_GROUP // 2
    R = pairs * L
    n_heads = n_kv * SWA_GROUP
    pair_of_row = lax.broadcasted_iota(jnp.int32, (R, 1), 0) // L

    @pl.when(u == 0)
    def _():
        jj = lax.broadcasted_iota(jnp.int32, (R, NK), 1)
        tt = lax.broadcasted_iota(jnp.int32, (R, NK), 0) % L
        dist = jnp.abs(tt + W - jj).astype(F32)
        for g in range(n_kv):
            for odd in range(2):
                head = g * SWA_GROUP + 2 * pair_of_row + odd
                slope = jnp.exp2(-8.0 * (head + 1).astype(F32) / n_heads) * LOG2E
                pen = slope * dist
                for v in range(3):
                    bias_s[v, 2 * g + odd] = jnp.where(jj >= W - v * L, pen, -NEG_INF)

    def pick(c_ref, z_ref):
        return jnp.where(is_sample, c_ref[...], z_ref[...])

    k_all = jnp.concatenate([pick(ck2_ref, k2_ref), pick(ck1_ref, k1_ref), k0_ref[...]], axis=0)
    v_all = jnp.concatenate([pick(cv2_ref, v2_ref), pick(cv1_ref, v1_ref), v0_ref[...]], axis=0)
    variant = jnp.where(is_sample, 2, jnp.minimum(u % units.ncp, 2))
    lane = lax.broadcasted_iota(jnp.int32, (NK, LANES), 1)
    lo_lanes = lane < hd

    k_half, v_half, qg = [], [], []
    for g in range(n_kv):
        tile = (g * hd) // LANES
        kt = k_all[:, tile * LANES:(tile + 1) * LANES]
        vt = v_all[:, tile * LANES:(tile + 1) * LANES]
        kr = pltpu.roll(kt, hd, axis=1)
        vr = pltpu.roll(vt, hd, axis=1)
        if (g * hd) % LANES == 0:
            k_lo, k_hi, v_lo, v_hi = kt, kr, vt, vr
        else:
            k_lo, k_hi, v_lo, v_hi = kr, kt, vr, vt
        k_half += [jnp.where(lo_lanes, k_lo, 0.0).astype(BF16), jnp.where(lo_lanes, 0.0, k_hi).astype(BF16)]
        v_half += [jnp.where(lo_lanes, v_lo, 0.0).astype(BF16), jnp.where(lo_lanes, 0.0, v_hi).astype(BF16)]
        q = jnp.concatenate(
            [q_ref[:, (g * pairs + p) * LANES:(g * pairs + p + 1) * LANES] for p in range(pairs)], axis=0)
        qg.append((q * (hd ** -0.5 * LOG2E)).astype(BF16))

    chains = [(g, odd) for g in range(n_kv) for odd in range(2)]
    s = [lax.dot_general(qg[g], k_half[2 * g + odd], NT, preferred_element_type=F32) - bias_s[variant, 2 * g + odd]
         for g, odd in chains]
    sink = []
    for g, odd in chains:
        col = jnp.zeros((R, 1), F32)
        for p in range(pairs):
            col = jnp.where(pair_of_row == p, sink_ref[layer, g * SWA_GROUP + 2 * p + odd] * LOG2E, col)
        sink.append(col)
    mx = [jnp.maximum(jnp.max(s[c], axis=1, keepdims=True), sink[c]) for c in range(len(chains))]
    e = [jnp.exp2(s[c] - mx[c]) for c in range(len(chains))]
    inv_den = [1.0 / (jnp.sum(e[c], axis=1, keepdims=True) + jnp.exp2(sink[c] - mx[c])) for c in range(len(chains))]
    pv = [jnp.dot(e[c].astype(BF16), v_half[c], preferred_element_type=F32) * inv_den[c] for c in range(len(chains))]
    for g in range(n_kv):
        o = pv[2 * g] + pv[2 * g + 1]
        for p in range(pairs):
            o_ref[:, (g * pairs + p) * LANES:(g * pairs + p + 1) * LANES] = o[p * L:(p + 1) * L].astype(o_ref.dtype)


def swa_mixer(z, sinks, cache_k, cache_v, layer, units, n_kv):
    T = z.shape[0]
    qw = n_kv * SWA_GROUP * SWA_HEAD_DIM
    kw = n_kv * SWA_HEAD_DIM
    assert kw % LANES == 0 and qw % kw == 0
    kb = qw // kw
    vb = kb + 1
    sidx = units.sample_index
    R = (SWA_GROUP // 2) * CHUNK
    NK = 3 * CHUNK

    def zspec(blk, back):
        return pl.BlockSpec((CHUNK, kw), lambda u: (jnp.maximum(u - back, 0), blk))

    def cspec(part):
        return pl.BlockSpec((None, CHUNK, kw), lambda u: (sidx(u), part, 0))

    bias_bytes = 3 * 2 * n_kv * R * 2 * LANES * 4
    nbytes = 2 * (CHUNK * qw * 4 + 10 * CHUNK * kw * 4 + CHUNK * qw * 2) + bias_bytes + 16 * R * NK * 4
    return pl.pallas_call(
        functools.partial(_swa_kernel, units=units, n_kv=n_kv, layer=layer),
        grid=(units.total,),
        in_specs=[pl.BlockSpec(memory_space=pltpu.SMEM),
                  pl.BlockSpec((CHUNK, qw), lambda u: (u, 0)),
                  zspec(kb, 0), zspec(kb, 1), zspec(kb, 2),
                  zspec(vb, 0), zspec(vb, 1), zspec(vb, 2),
                  cspec(1), cspec(0), cspec(1), cspec(0)],
        out_specs=pl.BlockSpec((CHUNK, qw), lambda u: (u, 0)),
        out_shape=jax.ShapeDtypeStruct((T, qw), BF16),
        scratch_shapes=[pltpu.VMEM((3, 2 * n_kv, R, NK), F32)],
        compiler_params=_params(("arbitrary",), nbytes),
        name="swa",
    )(sinks, z, z, z, z, z, z, z, cache_k, cache_k, cache_v, cache_v)


def _gla_kernel(q_ref, k_ref, v_ref, r_ref, glr_ref, wg_ref, bg_ref, gh_ref, s0_ref, o_ref, sp_ref, ss_ref, s_s, *,
                units):
    u = pl.program_id(1)
    is_sample = units.is_sample(u)
    is_prompt = jnp.logical_not(is_sample)
    L = CHUNK
    hb, dv, dk = s_s.shape

    @pl.when(jnp.logical_and(units.first(u), is_prompt))
    def _():
        s_s[...] = jnp.zeros_like(s_s)

    @pl.when(is_sample)
    def _():
        for hh in range(hb):
            s_s[hh] = s0_ref[hh].T

    glr = glr_ref[...].astype(BF16)
    row = lax.broadcasted_iota(jnp.int32, (L, L), 0)
    col = lax.broadcasted_iota(jnp.int32, (L, L), 1)
    tril = col <= row
    ones_tril = jnp.where(tril, 1.0, 0.0).astype(BF16)
    rr = lax.broadcasted_iota(jnp.int32, (L, dk), 0)
    n_sub = L // GLA_SUB

    heads = range(hb)
    ks = [slice(hh * dk, (hh + 1) * dk) for hh in heads]
    vs = [slice(hh * dv, (hh + 1) * dv) for hh in heads]
    lg = [_log_sigmoid(jnp.dot(glr, wg_ref[:, ks[hh]], preferred_element_type=F32) + bg_ref[:, ks[hh]])
          * (1.0 / GLA_TAU) for hh in heads]
    bc = [sum(jnp.dot(ones_tril, part, preferred_element_type=F32) for part in _split3(lg[hh])) for hh in heads]

    refs = [[bc[hh][i * GLA_SUB:i * GLA_SUB + 1, :] for i in range(n_sub)] for hh in heads]
    q_t = []
    for hh in heads:
        r_q = refs[hh][n_sub - 1]
        for i in range(n_sub - 2, -1, -1):
            r_q = jnp.where(rr < (i + 1) * GLA_SUB, refs[hh][i], r_q)
        q_t.append((q_ref[:, ks[hh]].astype(F32) * (dk ** -0.5) * jnp.exp(bc[hh] - r_q)).astype(BF16))
    blocks = [[] for _ in heads]
    for i in range(n_sub):
        for hh in heads:
            e_i = jnp.where(rr < (i + 1) * GLA_SUB, refs[hh][i] - bc[hh], NEG_INF)
            k_i = (k_ref[:, ks[hh]].astype(F32) * jnp.exp(e_i)).astype(BF16)
            blocks[hh].append(lax.dot_general(q_t[hh][i * GLA_SUB:(i + 1) * GLA_SUB], k_i, NT,
                                              preferred_element_type=F32))
    a = [jnp.where(tril, jnp.concatenate(blocks[hh], axis=0), 0.0).astype(BF16) for hh in heads]

    o = []
    for hh in heads:
        q_dec = (q_ref[:, ks[hh]].astype(F32) * (dk ** -0.5) * jnp.exp(bc[hh])).astype(BF16)
        o.append(jnp.dot(a[hh], v_ref[:, vs[hh]].astype(BF16), preferred_element_type=F32)
                 + lax.dot_general(q_dec, s_s[hh].astype(BF16), NT, preferred_element_type=F32))

    for hh in heads:
        b_last = bc[hh][L - 1:L, :]
        k_dec = (k_ref[:, ks[hh]].astype(F32) * jnp.exp(b_last - bc[hh])).astype(BF16)
        s_s[hh] = jnp.exp(b_last) * s_s[hh] + lax.dot_general(v_ref[:, vs[hh]].astype(BF16), k_dec, TN,
                                                               preferred_element_type=F32)

    for hh in heads:
        y = o[hh] * lax.rsqrt(jnp.mean(o[hh] * o[hh], axis=1, keepdims=True) + EPS)
        r = r_ref[:, vs[hh]].astype(F32)
        o_ref[:, vs[hh]] = (y * gh_ref[:, vs[hh]] * (r * _sigmoid(r))).astype(o_ref.dtype)

    @pl.when(jnp.logical_and(units.last(u), is_prompt))
    def _():
        for hh in range(hb):
            sp_ref[hh] = s_s[hh].T

    @pl.when(is_sample)
    def _():
        for hh in range(hb):
            ss_ref[hh] = s_s[hh].T


def gla_mixer(z, glr, w_g2, b_g, g_head, s0, layer, units):
    T = z.shape[0]
    _, DB, H, dk, dv = s0.shape
    B = units.n_prompt
    hb = GLA_HEADS_PER_STEP if H % GLA_HEADS_PER_STEP == 0 else 1
    ng = H // hb
    kw = hb * dk
    vw = hb * dv
    assert (2 * H * dk) % vw == 0
    v_blk0 = (2 * H * dk) // vw
    sidx = units.sample_index

    def pidx(u):
        return jnp.minimum(units.stream(u), B - 1)

    nbytes = (2 * (2 * CHUNK * kw * 2 + 2 * CHUNK * vw * 2 + CHUNK * LANES * 4 + LANES * kw * 2 + CHUNK * vw * 2
                   + 3 * kw * dv * 4) + kw * dv * 4 * 4)
    return pl.pallas_call(
        functools.partial(_gla_kernel, units=units),
        grid=(ng, units.total),
        in_specs=[pl.BlockSpec((CHUNK, kw), lambda g, u: (u, g)),
                  pl.BlockSpec((CHUNK, kw), lambda g, u: (u, ng + g)),
                  pl.BlockSpec((CHUNK, vw), lambda g, u: (u, v_blk0 + g)),
                  pl.BlockSpec((CHUNK, vw), lambda g, u: (u, v_blk0 + ng + g)),
                  pl.BlockSpec((CHUNK, LANES), lambda g, u: (u, 0)),
                  pl.BlockSpec((LANES, kw), lambda g, u: (0, g)),
                  pl.BlockSpec((None, 1, kw), lambda g, u: (layer, 0, g)),
                  pl.BlockSpec((None, 1, vw), lambda g, u: (layer, 0, g)),
                  pl.BlockSpec((None, None, hb, dk, dv), lambda g, u: (layer, sidx(u), g, 0, 0))],
        out_specs=[pl.BlockSpec((CHUNK, vw), lambda g, u: (u, g)),
                   pl.BlockSpec((None, hb, dk, dv), lambda g, u: (pidx(u), g, 0, 0)),
                   pl.BlockSpec((None, hb, dk, dv), lambda g, u: (sidx(u), g, 0, 0))],
        out_shape=[jax.ShapeDtypeStruct((T, H * dv), BF16),
                   jax.ShapeDtypeStruct((B, H, dk, dv), F32),
                   jax.ShapeDtypeStruct((DB, H, dk, dv), F32)],
        scratch_shapes=[pltpu.VMEM((hb, dv, dk), F32)],
        compiler_params=_params(("parallel", "arbitrary"), nbytes),
        name="gla",
    )(z, z, z, z, glr, w_g2, b_g[:, None, :], g_head[:, None, :], s0)


def _pad_rows(w, height):
    return jnp.pad(w, ((0, height - w.shape[0]), (0, 0)))


def kernel(x_prompt, x_sample, cache_swa_k, cache_swa_v, state_mlstm_C, state_mlstm_n, state_mlstm_m, state_gla_S,
           norm_mix, norm_ffn, norm_final, w_even_in, b_mlstm_i, b_mlstm_f, mlstm_head_norm, swa_sinks, w_even_out,
           w_odd_in, w_gla_gate, b_gla_gate, gla_head_norm, w_odd_out, w_ffn_in, w_ffn_out):
    B, S, D = x_prompt.shape
    DB, DS, _ = x_sample.shape
    depth = norm_mix.shape[0]
    assert DS == CHUNK and S % CHUNK == 0
    MH, dqk, dv = state_mlstm_C.shape[2:]
    n_kv = cache_swa_k.shape[3]
    w_buf = cache_swa_k.shape[2]
    assert w_buf == 2 * CHUNK and cache_swa_k.shape[4] == SWA_HEAD_DIM
    GH, gdk, gdv = state_gla_S.shape[2:]
    rank = w_gla_gate.shape[1]
    m_w = 2 * MH * dqk + 2 * MH * dv
    sq_w = n_kv * SWA_GROUP * SWA_HEAD_DIM
    kvw = n_kv * SWA_HEAD_DIM
    s_w = sq_w + 2 * kvw
    g_w = 2 * GH * gdk + 2 * GH * gdv
    units = Units(B, S // CHUNK, DB)
    TP = B * S

    assert depth >= 1
    w_even_in_t = jnp.swapaxes(w_even_in, 1, 2)
    w_odd_in_t = jnp.swapaxes(w_odd_in, 1, 2)

    p_k, p_v, p_C, p_n, p_m, p_S = [], [], [], [], [], []
    s_k, s_v, s_C, s_n, s_m, s_S = [], [], [], [], [], []
    for l in range(depth):
        e = l // 2
        if l % 2 == 0:
            w_gate = _pad_rows(w_even_in_t[e, m_w:m_w + 2 * MH], LANES)[None]
            w_attn = w_even_in_t[e, m_w + 2 * MH:][None]
            if l == 0:
                x, h = stack_and_norm(x_prompt.reshape(TP, D), x_sample.reshape(DB * DS, D), norm_mix, 0)
            else:
                h = rmsnorm(x, norm_mix, l, BF16)
            zm, w_out_b = project([h], w_even_in_t, e, m_w, BF16, w_nk=True, cast=(w_ffn_out, l),
                                  name="even_in_mlstm")
            zs = project([h], w_attn, 0, s_w, w_nk=True, name="even_in_attn")
            gates = project([h], w_gate, 0, LANES, w_nk=True, name="even_gates")
            hm, (pc, pn, pm), (sc, sn, sm) = mlstm_mixer(
                zm, gates, b_mlstm_i, b_mlstm_f, mlstm_head_norm, state_mlstm_C, state_mlstm_n, state_mlstm_m,
                e, units)
            ck = cache_swa_k[e].reshape(DB, w_buf, kvw)
            cv = cache_swa_v[e].reshape(DB, w_buf, kvw)
            hs = swa_mixer(zs, swa_sinks, ck, cv, e, units, n_kv)
            x = project([hm, hs], w_even_out, e, D, residual=x, name="even_out")
            zk = zs[:, sq_w:sq_w + kvw]
            zv = zs[:, sq_w + kvw:]
            kv_shape = (w_buf, n_kv, SWA_HEAD_DIM)
            p_k.append(zk[:TP].reshape(B, S, kvw)[:, -w_buf:].reshape((B,) + kv_shape))
            p_v.append(zv[:TP].reshape(B, S, kvw)[:, -w_buf:].reshape((B,) + kv_shape))
            s_k.append(jnp.concatenate([ck[:, CHUNK:], zk[TP:].reshape(DB, DS, kvw)], axis=1).reshape((DB,) + kv_shape))
            s_v.append(jnp.concatenate([cv[:, CHUNK:], zv[TP:].reshape(DB, DS, kvw)], axis=1).reshape((DB,) + kv_shape))
            p_C.append(pc); p_n.append(pn); p_m.append(pm)
            s_C.append(sc); s_n.append(sn); s_m.append(sm)
        else:
            z, h, w_out_b = normed_project(x, norm_mix, l, w_odd_in_t, e, g_w, BF16, w_nk=True, emit_h=True,
                                           cast=(w_ffn_out, l), name="odd_in")
            w_rank = _pad_rows(w_odd_in_t[e, g_w:], LANES)[None]
            glr = project([h], w_rank, 0, LANES, w_nk=True, name="odd_gate_rank")
            w_g2 = jnp.pad(w_gla_gate[e], ((0, LANES - rank), (0, 0))).astype(BF16)
            mix, ps, ss = gla_mixer(z, glr, w_g2, b_gla_gate, gla_head_norm, state_gla_S, e, units)
            x = project([mix], w_odd_out, e, D, residual=x, name="odd_out")
            p_S.append(ps); s_S.append(ss)
        act = normed_project(x, norm_ffn, l, w_ffn_in, l, w_ffn_in.shape[2] // 2, BF16, swiglu=True, name="ffn_in")
        x = project([act], w_out_b[None], 0, D, residual=x, half_rows=True, name="ffn_out")
    y_prompt = rmsnorm(x, norm_final[None], 0, F32, 0, TP)
    y_sample = rmsnorm(x, norm_final[None], 0, F32, TP, DB * DS)
    return (y_prompt.reshape(B, S, D), y_sample.reshape(DB, DS, D),
            jnp.stack(p_k), jnp.stack(p_v), jnp.stack(p_C), jnp.stack(p_n), jnp.stack(p_m), jnp.stack(p_S),
            jnp.stack(s_k), jnp.stack(s_v), jnp.stack(s_C), jnp.stack(s_n), jnp.stack(s_m), jnp.stack(s_S))
```

```python
import functools
import itertools

import jax
import jax.numpy as jnp
from jax import lax
from jax.experimental import pallas as pl
from jax.experimental.pallas import tpu as pltpu

F32 = jnp.float32
BF16 = jnp.bfloat16

CHUNK = 64
SWA_HEAD_DIM = 64
SWA_GROUP = 8
LANES = 128
GLA_SUB = 16
GLA_HEADS_PER_STEP = 8
GLA_TAU = 16.0
LOG2E = 1.4426950408889634
EPS = 1e-6
NEG_INF = float("-inf")

V7X_VMEM_BYTES = 64 * 1024 * 1024
VMEM_CAP_BYTES = V7X_VMEM_BYTES - 4 * 1024 * 1024

NT = (((1,), (1,)), ((), ()))
TN = (((0,), (0,)), ((), ()))


def _pick(n, cands):
    for c in cands:
        if n % c == 0:
            return c
    raise ValueError(f"no tile in {cands} divides {n}")


def _params(semantics, block_bytes):
    limit = min(VMEM_CAP_BYTES, block_bytes + 16 * 1024 * 1024)
    return pltpu.CompilerParams(dimension_semantics=semantics, vmem_limit_bytes=int(limit))


def _log_sigmoid(x):
    return jnp.minimum(x, 0.0) - jnp.log(1.0 + jnp.exp(-jnp.abs(x)))


def _sigmoid(x):
    return 1.0 / (1.0 + jnp.exp(-x))


def _split3(x):
    hi = x.astype(BF16)
    r1 = x - hi.astype(F32)
    mid = r1.astype(BF16)
    lo = (r1 - mid.astype(F32)).astype(BF16)
    return hi, mid, lo


def _rmsnorm_kernel(x_ref, g_ref, o_ref):
    x = x_ref[...]
    y = x * lax.rsqrt(jnp.mean(x * x, axis=-1, keepdims=True) + EPS)
    o_ref[...] = (y * g_ref[...]).astype(o_ref.dtype)


def rmsnorm(x, g, layer, out_dtype, row0=0, rows=None):
    D = x.shape[1]
    T = x.shape[0] if rows is None else rows
    tr = _pick(T, (512, 272, 256, 160, 128, 64))
    assert row0 % tr == 0
    blk0 = row0 // tr
    nbytes = 2 * tr * D * (4 + jnp.dtype(out_dtype).itemsize)
    return pl.pallas_call(
        _rmsnorm_kernel,
        grid=(T // tr,),
        in_specs=[pl.BlockSpec((tr, D), lambda i: (blk0 + i, 0)), pl.BlockSpec((None, 1, D), lambda i: (layer, 0, 0))],
        out_specs=pl.BlockSpec((tr, D), lambda i: (i, 0)),
        out_shape=jax.ShapeDtypeStruct((T, D), out_dtype),
        compiler_params=_params(("parallel",), nbytes),
        name="rmsnorm",
    )(x, g[:, None, :])


def _stack_norm_kernel(xp_ref, xs_ref, g_ref, x_ref, h_ref, *, n_prompt_tiles):
    x = jnp.where(pl.program_id(0) < n_prompt_tiles, xp_ref[...], xs_ref[...])
    x_ref[...] = x
    y = x * lax.rsqrt(jnp.mean(x * x, axis=-1, keepdims=True) + EPS)
    h_ref[...] = (y * g_ref[...]).astype(h_ref.dtype)


def stack_and_norm(x_prompt, x_sample, g, layer):
    TP, D = x_prompt.shape
    TS = x_sample.shape[0]
    tr = next(c for c in (256, 128, 64) if TP % c == 0 and TS % c == 0)
    n_p = TP // tr
    T = TP + TS
    nbytes = 2 * tr * D * (4 + 4 + 4 + 2)
    return pl.pallas_call(
        functools.partial(_stack_norm_kernel, n_prompt_tiles=n_p),
        grid=(T // tr,),
        in_specs=[pl.BlockSpec((tr, D), lambda i: (jnp.minimum(i, n_p - 1), 0)),
                  pl.BlockSpec((tr, D), lambda i: (jnp.maximum(i - n_p, 0), 0)),
                  pl.BlockSpec((None, 1, D), lambda i: (layer, 0, 0))],
        out_specs=[pl.BlockSpec((tr, D), lambda i: (i, 0)), pl.BlockSpec((tr, D), lambda i: (i, 0))],
        out_shape=[jax.ShapeDtypeStruct((T, D), F32), jax.ShapeDtypeStruct((T, D), BF16)],
        compiler_params=_params(("parallel",), nbytes),
        name="stack_norm",
    )(x_prompt, x_sample, g[:, None, :])


def _m_tile(T):
    return _pick(T, (1088, 1024, 640, 512, 320, 256, 128, 64))


def _proj_kernel(*refs, n_x, has_res, w_nk, has_cast):
    x_refs = refs[:n_x]
    w_ref = refs[n_x]
    r_ref = refs[n_x + 1] if has_res else None
    o_ref = refs[n_x + 1 + has_res + has_cast]
    if has_cast:
        refs[-1][...] = refs[n_x + 1 + has_res][...].astype(BF16)
    k0 = 0
    acc = None
    for x_ref in x_refs:
        kx = x_ref.shape[1]
        if w_nk:
            part = lax.dot_general(x_ref[...], w_ref[:, k0:k0 + kx].astype(BF16), NT, preferred_element_type=F32)
        else:
            part = jnp.dot(x_ref[...], w_ref[k0:k0 + kx, :].astype(BF16), preferred_element_type=F32)
        acc = part if acc is None else acc + part
        k0 += kx
    if has_res:
        acc = r_ref[...] + acc
    o_ref[...] = acc.astype(o_ref.dtype)


def _cast_rows(n_rows, n_steps):
    return min(r for r in range(16, n_rows + 1, 16) if n_rows % r == 0 and n_rows // r <= n_steps)


def project(xs, w, layer, n_cols, out_dtype=F32, residual=None, w_nk=False, half_rows=False, cast=None,
            name="project"):
    T = xs[0].shape[0]
    K = w.shape[2] if w_nk else w.shape[1]
    assert sum(x.shape[1] for x in xs) == K
    tm = _m_tile(T)
    if half_rows and tm % 32 == 0:
        tm //= 2
    tn = _pick(n_cols, (512, 256, 128))
    nj = n_cols // tn
    osz = jnp.dtype(out_dtype).itemsize
    wsz = jnp.dtype(w.dtype).itemsize
    nbytes = 2 * (tm * K * 2 + K * tn * wsz + tm * tn * osz) + K * tn * 2 + tm * tn * 4
    in_specs = [pl.BlockSpec((tm, x.shape[1]), lambda i, j: (i, 0)) for x in xs]
    if w_nk:
        in_specs.append(pl.BlockSpec((None, tn, K), lambda i, j: (layer, j, 0)))
    else:
        in_specs.append(pl.BlockSpec((None, K, tn), lambda i, j: (layer, 0, j)))
    args = list(xs) + [w]
    if residual is not None:
        in_specs.append(pl.BlockSpec((tm, tn), lambda i, j: (i, j)))
        args.append(residual)
        nbytes += 2 * tm * tn * 4
    out_specs = [pl.BlockSpec((tm, tn), lambda i, j: (i, j))]
    out_shape = [jax.ShapeDtypeStruct((T, n_cols), out_dtype)]
    if cast is not None:
        src, src_layer = cast
        n_rows, width = src.shape[1:]
        crows = _cast_rows(n_rows, (T // tm) * nj)
        last = n_rows // crows - 1
        in_specs.append(pl.BlockSpec((None, crows, width), lambda i, j: (src_layer, jnp.minimum(i * nj + j, last), 0)))
        args.append(src)
        out_specs.append(pl.BlockSpec((crows, width), lambda i, j: (jnp.minimum(i * nj + j, last), 0)))
        out_shape.append(jax.ShapeDtypeStruct((n_rows, width), BF16))
        nbytes += 2 * crows * width * 6
    outs = pl.pallas_call(
        functools.partial(_proj_kernel, n_x=len(xs), has_res=residual is not None, w_nk=w_nk,
                          has_cast=cast is not None),
        grid=(T // tm, nj),
        in_specs=in_specs,
        out_specs=out_specs,
        out_shape=out_shape,
        compiler_params=_params(("parallel" if cast is None else "arbitrary", "arbitrary"), nbytes),
        name=name,
    )(*args)
    return outs if cast is not None else outs[0]


def _normed_kernel(*refs, n_slabs, slab_rows, n_w, w_nk, emit_h, swiglu, has_cast):
    x_ref, g_ref = refs[:2]
    w_refs = refs[2:2 + n_w]
    src_ref = refs[2 + n_w] if has_cast else None
    o_ref = refs[2 + n_w + has_cast]
    hout_ref = refs[3 + n_w + has_cast] if emit_h else None
    dst_ref = refs[-2] if has_cast else None
    h2_ref = refs[-1]
    t = pl.program_id(0)
    slab = jnp.where(t < pl.num_programs(0) - 1, jnp.minimum(pl.program_id(1), n_slabs - 1), n_slabs - 1)

    def norm_slab(buf):
        x = x_ref[...]
        y = x * lax.rsqrt(jnp.mean(x * x, axis=-1, keepdims=True) + EPS)
        yb = (y * g_ref[...]).astype(BF16)
        h2_ref[buf, pl.ds(pl.multiple_of(slab * slab_rows, slab_rows), slab_rows), :] = yb
        if emit_h:
            hout_ref[...] = yb

    def matmul(h, w_ref):
        if w_nk:
            return lax.dot_general(h, w_ref[...].astype(BF16), NT, preferred_element_type=F32)
        return jnp.dot(h, w_ref[...].astype(BF16), preferred_element_type=F32)

    @pl.when(t == 0)
    def _():
        norm_slab(0)

    @pl.when(t > 0)
    def _():
        h = h2_ref[(t - 1) % 2]
        if swiglu:
            g = matmul(h, w_refs[0])
            u = matmul(h, w_refs[1])
            o_ref[...] = (g * _sigmoid(g) * u).astype(o_ref.dtype)
        else:
            o_ref[...] = matmul(h, w_refs[0]).astype(o_ref.dtype)
        norm_slab(t % 2)
        if has_cast:
            dst_ref[...] = src_ref[...].astype(BF16)


def _slab_count(tm, nj):
    units16 = tm // 16
    return max(d for d in range(1, units16 + 1) if units16 % d == 0 and d <= nj)


def normed_project(x, g, g_layer, w, layer, n_cols, out_dtype, w_nk=False, swiglu=False, emit_h=False, cast=None,
                   name="normed_project"):
    T, K = x.shape
    tm = _pick(T, (2176, 1088, 640, 512, 256, 128, 64)) if swiglu else _m_tile(T)
    cands = [c for c in ((256, 128) if swiglu else (512, 256, 128)) if n_cols % c == 0]
    tn = next((c for c in cands if n_cols // c >= tm // CHUNK), cands[-1])
    nj = n_cols // tn
    n_tiles = T // tm
    ns = _slab_count(tm, nj)
    rows = tm // ns
    osz = jnp.dtype(out_dtype).itemsize
    n_w = 2 if swiglu else 1

    def row_tile(t):
        return jnp.maximum(t - 1, 0)

    def col(t, j):
        return jnp.where(t == 0, 0, j)

    def slab_idx(t, j):
        return jnp.where(t < n_tiles, t * ns + jnp.minimum(j, ns - 1), n_tiles * ns - 1)

    if w_nk:
        w_specs = [pl.BlockSpec((None, tn, K), lambda t, j, o=o: (layer, col(t, j) + o, 0)) for o in range(n_w)]
    else:
        w_specs = [pl.BlockSpec((None, K, tn), lambda t, j, o=o: (layer, 0, col(t, j) + o * nj)) for o in range(n_w)]
    out_specs = [pl.BlockSpec((tm, tn), lambda t, j: (row_tile(t), col(t, j)))]
    out_shape = [jax.ShapeDtypeStruct((T, n_cols), out_dtype)]
    if emit_h:
        out_specs.append(pl.BlockSpec((rows, K), lambda t, j: (slab_idx(t, j), 0)))
        out_shape.append(jax.ShapeDtypeStruct((T, K), BF16))
    nbytes = (2 * tm * K * 2 + 2 * rows * K * (4 + 2 * emit_h) + n_w * (2 * K * tn * 4 + K * tn * 2)
              + 2 * tm * tn * osz + (1 + n_w) * tm * tn * 4)
    args = [x, g[:, None, :]] + [w] * n_w
    in_specs = [pl.BlockSpec((rows, K), lambda t, j: (slab_idx(t, j), 0)),
                pl.BlockSpec((None, 1, K), lambda t, j: (g_layer, 0, 0))] + w_specs
    if cast is not None:
        src, src_layer = cast
        n_rows, width = src.shape[1:]
        crows = _cast_rows(n_rows, n_tiles * nj)
        last = n_rows // crows - 1

        def cast_idx(t, j):
            return jnp.clip((t - 1) * nj + j, 0, last)

        in_specs.append(pl.BlockSpec((None, crows, width), lambda t, j: (src_layer, cast_idx(t, j), 0)))
        args.append(src)
        out_specs.append(pl.BlockSpec((crows, width), lambda t, j: (cast_idx(t, j), 0)))
        out_shape.append(jax.ShapeDtypeStruct((n_rows, width), BF16))
        nbytes += 2 * crows * width * 6
    outs = pl.pallas_call(
        functools.partial(_normed_kernel, n_slabs=ns, slab_rows=rows, n_w=n_w, w_nk=w_nk, emit_h=emit_h,
                          swiglu=swiglu, has_cast=cast is not None),
        grid=(n_tiles + 1, nj),
        in_specs=in_specs,
        out_specs=out_specs,
        out_shape=out_shape,
        scratch_shapes=[pltpu.VMEM((2, tm, K), BF16)],
        compiler_params=_params(("arbitrary", "arbitrary"), nbytes),
        name=name,
    )(*args)
    return outs if len(outs) > 1 else outs[0]


class Units:
    def __init__(self, n_prompt, chunks_per_prompt, n_sample):
        self.ncp = chunks_per_prompt
        self.up = n_prompt * chunks_per_prompt
        self.n_prompt = n_prompt
        self.n_sample = n_sample
        self.total = self.up + n_sample
        self.n_streams = n_prompt + n_sample

    def is_sample(self, u):
        return u >= self.up

    def stream(self, u):
        return jnp.where(u < self.up, u // self.ncp, self.n_prompt + u - self.up)

    def sample_index(self, u):
        return jnp.maximum(u - self.up, 0)

    def first(self, u):
        return jnp.logical_or(u >= self.up, u % self.ncp == 0)

    def last(self, u):
        return jnp.logical_or(u >= self.up, u % self.ncp == self.ncp - 1)


def _mlstm_steps(bi_ref, bf_ref, m0_ref, q_ref, k_ref, v_ref, og_ref, gate_ref, gh_ref, c0_ref, n0_ref,
                 hm_ref, cp_ref, np_ref, mp_ref, cs_ref, ns_ref, ms_ref, c_s, n_s, m_s, *, units, n_heads, layer):
    u = pl.program_id(0)
    is_sample = units.is_sample(u)
    is_prompt = jnp.logical_not(is_sample)
    L = CHUNK
    H = n_heads
    dqk = c_s.shape[1]
    dv = c_s.shape[2]

    @pl.when(jnp.logical_and(units.first(u), is_prompt))
    def _():
        c_s[...] = jnp.zeros_like(c_s)
        n_s[...] = jnp.zeros_like(n_s)
        m_s[...] = jnp.zeros_like(m_s)

    @pl.when(is_sample)
    def _():
        c_s[...] = c0_ref[...]
        n_s[...] = n0_ref[...]
        for hh in range(H):
            m_s[hh] = jnp.full(m_s.shape[1:], m0_ref[layer * units.n_sample + units.sample_index(u), hh], F32)

    yield
    gates = gate_ref[...]
    lane = lax.broadcasted_iota(jnp.int32, gates.shape, 1)
    row = lax.broadcasted_iota(jnp.int32, (L, L), 0)
    col = lax.broadcasted_iota(jnp.int32, (L, L), 1)
    eye = row == col
    tril = col <= row
    triu = row <= col

    heads = range(H)
    qs = [slice(hh * dqk, (hh + 1) * dqk) for hh in heads]
    vs = [slice(hh * dv, (hh + 1) * dv) for hh in heads]

    ig_col = [jnp.sum(jnp.where(lane == hh, gates, 0.0), axis=1, keepdims=True) + bi_ref[layer, hh] for hh in heads]
    lf_col = [_log_sigmoid(jnp.sum(jnp.where(lane == H + hh, gates, 0.0), axis=1, keepdims=True) + bf_ref[layer, hh])
              for hh in heads]
    ig_row = [jnp.sum(jnp.where(eye, ig_col[hh], 0.0), axis=0, keepdims=True) for hh in heads]
    lf_row = [jnp.sum(jnp.where(eye, lf_col[hh], 0.0), axis=0, keepdims=True) for hh in heads]
    b_col = [jnp.sum(jnp.where(tril, lf_row[hh], 0.0), axis=1, keepdims=True) for hh in heads]
    b_row = [jnp.sum(jnp.where(triu, lf_col[hh], 0.0), axis=0, keepdims=True) for hh in heads]
    m_prev = [m_s[hh][:, :1] for hh in heads]
    m_t, w_intra, w_inter = [], [], []
    for hh in heads:
        d = jnp.where(tril, b_col[hh] - b_row[hh] + ig_row[hh], NEG_INF)
        inter = b_col[hh] + m_prev[hh]
        m_c = jnp.maximum(inter, jnp.max(d, axis=1, keepdims=True))
        m_t.append(m_c)
        w_intra.append(jnp.exp(d - m_c))
        w_inter.append(jnp.exp(inter - m_c))

    yield
    a = [lax.dot_general(q_ref[:, qs[hh]].astype(BF16),
                         (k_ref[:, qs[hh]].astype(F32) * (dqk ** -0.5)).astype(BF16), NT,
                         preferred_element_type=F32) * w_intra[hh] for hh in heads]

    yield
    hval = []
    for hh in heads:
        q = q_ref[:, qs[hh]].astype(F32)
        num = (jnp.dot(a[hh].astype(BF16), v_ref[:, vs[hh]].astype(BF16), preferred_element_type=F32)
               + jnp.dot(q_ref[:, qs[hh]].astype(BF16), c_s[hh].astype(BF16), preferred_element_type=F32)
               * w_inter[hh])
        den = (jnp.sum(a[hh], axis=1, keepdims=True)
               + jnp.sum(q * n_s[hh], axis=1, keepdims=True) * w_inter[hh])
        den = jnp.maximum(jnp.abs(den), jnp.exp(-m_t[hh]))
        hval.append(num / den)

    yield
    for hh in heads:
        y = hval[hh] * lax.rsqrt(jnp.mean(hval[hh] * hval[hh], axis=1, keepdims=True) + EPS)
        gate_out = _sigmoid(og_ref[:, vs[hh]].astype(F32))
        hm_ref[:, vs[hh]] = (y * gh_ref[:, vs[hh]] * gate_out).astype(hm_ref.dtype)

    yield
    for hh in heads:
        k = k_ref[:, qs[hh]].astype(F32) * (dqk ** -0.5)
        m_new = m_t[hh][L - 1:L, :]
        b_last = b_col[hh][L - 1:L, :]
        w_last = jnp.exp(b_last - b_col[hh] + ig_col[hh] - m_new)
        decay = jnp.exp(b_last + m_prev[hh] - m_new)
        k_w = k * w_last
        c_s[hh] = decay * c_s[hh] + lax.dot_general(k_w.astype(BF16), v_ref[:, vs[hh]].astype(BF16), TN,
                                                    preferred_element_type=F32)
        n_s[hh] = decay * n_s[hh] + jnp.sum(k_w, axis=0, keepdims=True)
        m_s[hh] = jnp.broadcast_to(m_new, m_s.shape[1:])

    yield
    @pl.when(jnp.logical_and(units.last(u), is_prompt))
    def _():
        cp_ref[...] = c_s[...]
        np_ref[...] = n_s[...]
        mp_ref[...] = m_s[...]

    @pl.when(is_sample)
    def _():
        cs_ref[...] = c_s[...]
        ns_ref[...] = n_s[...]
        ms_ref[...] = m_s[...]


def mlstm_plan(z, gates, b_i, b_f, g_head, c0, n0, m0, layer, units):
    T = z.shape[0]
    n_layers, DB, H, dqk, dv = c0.shape
    B = units.n_prompt
    qw = H * dqk
    vw = H * dv
    assert (2 * qw) % vw == 0
    v_blk = (2 * qw) // vw
    sidx = units.sample_index

    def pidx(u):
        return jnp.minimum(units.stream(u), B - 1)

    smem = pl.BlockSpec(memory_space=pltpu.SMEM)

    def state_specs(idx):
        return [pl.BlockSpec((None, H, dqk, dv), lambda u: (idx(u), 0, 0, 0)),
                pl.BlockSpec((None, H, 1, dqk), lambda u: (idx(u), 0, 0, 0)),
                pl.BlockSpec((None, H, 1, LANES), lambda u: (idx(u), 0, 0, 0))]

    def state_shapes(n):
        return [jax.ShapeDtypeStruct((n, H, dqk, dv), F32), jax.ShapeDtypeStruct((n, H, 1, dqk), F32),
                jax.ShapeDtypeStruct((n, H, 1, LANES), F32)]

    return dict(
        steps=functools.partial(_mlstm_steps, units=units, n_heads=H, layer=layer),
        in_specs=[smem, smem, smem,
                  pl.BlockSpec((CHUNK, qw), lambda u: (u, 0)),
                  pl.BlockSpec((CHUNK, qw), lambda u: (u, 1)),
                  pl.BlockSpec((CHUNK, vw), lambda u: (u, v_blk)),
                  pl.BlockSpec((CHUNK, vw), lambda u: (u, v_blk + 1)),
                  pl.BlockSpec((CHUNK, LANES), lambda u: (u, 0)),
                  pl.BlockSpec((None, 1, vw), lambda u: (layer, 0, 0)),
                  pl.BlockSpec((None, None, H, dqk, dv), lambda u: (layer, sidx(u), 0, 0, 0)),
                  pl.BlockSpec((None, None, H, 1, dqk), lambda u: (layer, sidx(u), 0, 0, 0))],
        args=[b_i, b_f, m0.reshape(n_layers * DB, H), z, z, z, z, gates, g_head[:, None, :], c0,
              n0.reshape(n_layers, DB, H, 1, dqk)],
        out_specs=[pl.BlockSpec((CHUNK, vw), lambda u: (u, 0))] + state_specs(pidx) + state_specs(sidx),
        out_shape=[jax.ShapeDtypeStruct((T, vw), BF16)] + state_shapes(B) + state_shapes(DB),
        scratch=[pltpu.VMEM((H, dqk, dv), F32), pltpu.VMEM((H, 1, dqk), F32), pltpu.VMEM((H, 1, LANES), F32)],
        nbytes=(2 * (2 * CHUNK * qw * 2 + 2 * CHUNK * vw * 2 + CHUNK * LANES * 4 + CHUNK * vw * 2 + 3 * qw * dv * 4)
                + qw * dv * 4 * 4))


def _unit_kernels(*refs, plans):
    n_in = sum(len(p["in_specs"]) for p in plans)
    n_out = sum(len(p["out_specs"]) for p in plans)
    i0, o0, s0 = 0, n_in, n_in + n_out
    gens = []
    for p in plans:
        ni, no, nscr = len(p["in_specs"]), len(p["out_specs"]), len(p["scratch"])
        gens.append(p["steps"](*refs[i0:i0 + ni], *refs[o0:o0 + no], *refs[s0:s0 + nscr]))
        i0, o0, s0 = i0 + ni, o0 + no, s0 + nscr
    for _ in itertools.zip_longest(*gens):
        pass


def run_unit_kernels(plans, units, name):
    outs = pl.pallas_call(
        functools.partial(_unit_kernels, plans=plans),
        grid=(units.total,),
        in_specs=[spec for p in plans for spec in p["in_specs"]],
        out_specs=[spec for p in plans for spec in p["out_specs"]],
        out_shape=[shape for p in plans for shape in p["out_shape"]],
        scratch_shapes=[scr for p in plans for scr in p["scratch"]],
        compiler_params=_params(("arbitrary",), sum(p["nbytes"] for p in plans)),
        name=name,
    )(*[arg for p in plans for arg in p["args"]])
    split, k = [], 0
    for p in plans:
        split.append(outs[k:k + len(p["out_specs"])])
        k += len(p["out_specs"])
    return split


def _swa_steps(sink_ref, q_ref, k0_ref, k1_ref, k2_ref, v0_ref, v1_ref, v2_ref, ck1_ref, ck2_ref, cv1_ref, cv2_ref,
               o_ref, bias_s, *, units, n_kv, layer):
    u = pl.program_id(0)
    is_sample = units.is_sample(u)
    L = CHUNK
    W = 2 * L
    NK = W + L
    hd = SWA_HEAD_DIM
    pairs = SWA_GROUP // 2
    R = pairs * L
    n_heads = n_kv * SWA_GROUP
    pair_of_row = lax.broadcasted_iota(jnp.int32, (R, 1), 0) // L

    @pl.when(u == 0)
    def _():
        jj = lax.broadcasted_iota(jnp.int32, (R, NK), 1)
        tt = lax.broadcasted_iota(jnp.int32, (R, NK), 0) % L
        dist = jnp.abs(tt + W - jj).astype(F32)
        for g in range(n_kv):
            for odd in range(2):
                head = g * SWA_GROUP + 2 * pair_of_row + odd
                slope = jnp.exp2(-8.0 * (head + 1).astype(F32) / n_heads) * LOG2E
                pen = slope * dist
                for v in range(3):
                    bias_s[v, 2 * g + odd] = jnp.where(jj >= W - v * L, pen, -NEG_INF)

    yield
    def pick(c_ref, z_ref):
        return jnp.where(is_sample, c_ref[...], z_ref[...])

    k_all = jnp.concatenate([pick(ck2_ref, k2_ref), pick(ck1_ref, k1_ref), k0_ref[...]], axis=0)
    v_all = jnp.concatenate([pick(cv2_ref, v2_ref), pick(cv1_ref, v1_ref), v0_ref[...]], axis=0)
    variant = jnp.where(is_sample, 2, jnp.minimum(u % units.ncp, 2))
    lane = lax.broadcasted_iota(jnp.int32, (NK, LANES), 1)
    lo_lanes = lane < hd

    k_half, v_half, qg = [], [], []
    for g in range(n_kv):
        tile = (g * hd) // LANES
        kt = k_all[:, tile * LANES:(tile + 1) * LANES]
        vt = v_all[:, tile * LANES:(tile + 1) * LANES]
        kr = pltpu.roll(kt, hd, axis=1)
        vr = pltpu.roll(vt, hd, axis=1)
        if (g * hd) % LANES == 0:
            k_lo, k_hi, v_lo, v_hi = kt, kr, vt, vr
        else:
            k_lo, k_hi, v_lo, v_hi = kr, kt, vr, vt
        k_half += [jnp.where(lo_lanes, k_lo, 0.0).astype(BF16), jnp.where(lo_lanes, 0.0, k_hi).astype(BF16)]
        v_half += [jnp.where(lo_lanes, v_lo, 0.0).astype(BF16), jnp.where(lo_lanes, 0.0, v_hi).astype(BF16)]
        q = jnp.concatenate(
            [q_ref[:, (g * pairs + p) * LANES:(g * pairs + p + 1) * LANES] for p in range(pairs)], axis=0)
        qg.append((q * (hd ** -0.5 * LOG2E)).astype(BF16))

    yield
    chains = [(g, odd) for g in range(n_kv) for odd in range(2)]
    s = [lax.dot_general(qg[g], k_half[2 * g + odd], NT, preferred_element_type=F32) - bias_s[variant, 2 * g + odd]
         for g, odd in chains]
    sink = []
    for g, odd in chains:
        col = jnp.zeros((R, 1), F32)
        for p in range(pairs):
            col = jnp.where(pair_of_row == p, sink_ref[layer, g * SWA_GROUP + 2 * p + odd] * LOG2E, col)
        sink.append(col)
    yield
    mx = [jnp.maximum(jnp.max(s[c], axis=1, keepdims=True), sink[c]) for c in range(len(chains))]
    e = [jnp.exp2(s[c] - mx[c]) for c in range(len(chains))]
    inv_den = [1.0 / (jnp.sum(e[c], axis=1, keepdims=True) + jnp.exp2(sink[c] - mx[c])) for c in range(len(chains))]
    yield
    pv = [jnp.dot(e[c].astype(BF16), v_half[c], preferred_element_type=F32) * inv_den[c] for c in range(len(chains))]
    yield
    for g in range(n_kv):
        o = pv[2 * g] + pv[2 * g + 1]
        for p in range(pairs):
            o_ref[:, (g * pairs + p) * LANES:(g * pairs + p + 1) * LANES] = o[p * L:(p + 1) * L].astype(o_ref.dtype)


def swa_plan(z, sinks, cache_k, cache_v, layer, units, n_kv):
    T = z.shape[0]
    qw = n_kv * SWA_GROUP * SWA_HEAD_DIM
    kw = n_kv * SWA_HEAD_DIM
    assert kw % LANES == 0 and qw % kw == 0
    kb = qw // kw
    vb = kb + 1
    sidx = units.sample_index
    R = (SWA_GROUP // 2) * CHUNK
    NK = 3 * CHUNK

    def zspec(blk, back):
        return pl.BlockSpec((CHUNK, kw), lambda u: (jnp.maximum(u - back, 0), blk))

    def cspec(part):
        return pl.BlockSpec((None, CHUNK, kw), lambda u: (sidx(u), part, 0))

    bias_bytes = 3 * 2 * n_kv * R * 2 * LANES * 4
    return dict(
        steps=functools.partial(_swa_steps, units=units, n_kv=n_kv, layer=layer),
        in_specs=[pl.BlockSpec(memory_space=pltpu.SMEM),
                  pl.BlockSpec((CHUNK, qw), lambda u: (u, 0)),
                  zspec(kb, 0), zspec(kb, 1), zspec(kb, 2),
                  zspec(vb, 0), zspec(vb, 1), zspec(vb, 2),
                  cspec(1), cspec(0), cspec(1), cspec(0)],
        args=[sinks, z, z, z, z, z, z, z, cache_k, cache_k, cache_v, cache_v],
        out_specs=[pl.BlockSpec((CHUNK, qw), lambda u: (u, 0))],
        out_shape=[jax.ShapeDtypeStruct((T, qw), BF16)],
        scratch=[pltpu.VMEM((3, 2 * n_kv, R, NK), F32)],
        nbytes=2 * (CHUNK * qw * 4 + 10 * CHUNK * kw * 4 + CHUNK * qw * 2) + bias_bytes + 16 * R * NK * 4)


def _gla_kernel(q_ref, k_ref, v_ref, r_ref, glr_ref, wg_ref, bg_ref, gh_ref, s0_ref, o_ref, sp_ref, ss_ref, s_s, *,
                units):
    u = pl.program_id(1)
    is_sample = units.is_sample(u)
    is_prompt = jnp.logical_not(is_sample)
    L = CHUNK
    hb, dv, dk = s_s.shape

    @pl.when(jnp.logical_and(units.first(u), is_prompt))
    def _():
        s_s[...] = jnp.zeros_like(s_s)

    @pl.when(is_sample)
    def _():
        for hh in range(hb):
            s_s[hh] = s0_ref[hh].T

    glr = glr_ref[...].astype(BF16)
    row = lax.broadcasted_iota(jnp.int32, (L, L), 0)
    col = lax.broadcasted_iota(jnp.int32, (L, L), 1)
    tril = col <= row
    ones_tril = jnp.where(tril, 1.0, 0.0).astype(BF16)
    rr = lax.broadcasted_iota(jnp.int32, (L, dk), 0)
    n_sub = L // GLA_SUB

    heads = range(hb)
    ks = [slice(hh * dk, (hh + 1) * dk) for hh in heads]
    vs = [slice(hh * dv, (hh + 1) * dv) for hh in heads]
    lg = [_log_sigmoid(jnp.dot(glr, wg_ref[:, ks[hh]], preferred_element_type=F32) + bg_ref[:, ks[hh]])
          * (1.0 / GLA_TAU) for hh in heads]
    bc = [sum(jnp.dot(ones_tril, part, preferred_element_type=F32) for part in _split3(lg[hh])) for hh in heads]

    refs = [[bc[hh][i * GLA_SUB:i * GLA_SUB + 1, :] for i in range(n_sub)] for hh in heads]
    q_t = []
    for hh in heads:
        r_q = refs[hh][n_sub - 1]
        for i in range(n_sub - 2, -1, -1):
            r_q = jnp.where(rr < (i + 1) * GLA_SUB, refs[hh][i], r_q)
        q_t.append((q_ref[:, ks[hh]].astype(F32) * (dk ** -0.5) * jnp.exp(bc[hh] - r_q)).astype(BF16))
    blocks = [[] for _ in heads]
    for i in range(n_sub):
        for hh in heads:
            e_i = jnp.where(rr < (i + 1) * GLA_SUB, refs[hh][i] - bc[hh], NEG_INF)
            k_i = (k_ref[:, ks[hh]].astype(F32) * jnp.exp(e_i)).astype(BF16)
            blocks[hh].append(lax.dot_general(q_t[hh][i * GLA_SUB:(i + 1) * GLA_SUB], k_i, NT,
                                              preferred_element_type=F32))
    a = [jnp.where(tril, jnp.concatenate(blocks[hh], axis=0), 0.0).astype(BF16) for hh in heads]

    o = []
    for hh in heads:
        q_dec = (q_ref[:, ks[hh]].astype(F32) * (dk ** -0.5) * jnp.exp(bc[hh])).astype(BF16)
        o.append(jnp.dot(a[hh], v_ref[:, vs[hh]].astype(BF16), preferred_element_type=F32)
                 + lax.dot_general(q_dec, s_s[hh].astype(BF16), NT, preferred_element_type=F32))

    for hh in heads:
        b_last = bc[hh][L - 1:L, :]
        k_dec = (k_ref[:, ks[hh]].astype(F32) * jnp.exp(b_last - bc[hh])).astype(BF16)
        s_s[hh] = jnp.exp(b_last) * s_s[hh] + lax.dot_general(v_ref[:, vs[hh]].astype(BF16), k_dec, TN,
                                                               preferred_element_type=F32)

    for hh in heads:
        y = o[hh] * lax.rsqrt(jnp.mean(o[hh] * o[hh], axis=1, keepdims=True) + EPS)
        r = r_ref[:, vs[hh]].astype(F32)
        o_ref[:, vs[hh]] = (y * gh_ref[:, vs[hh]] * (r * _sigmoid(r))).astype(o_ref.dtype)

    @pl.when(jnp.logical_and(units.last(u), is_prompt))
    def _():
        for hh in range(hb):
            sp_ref[hh] = s_s[hh].T

    @pl.when(is_sample)
    def _():
        for hh in range(hb):
            ss_ref[hh] = s_s[hh].T


def gla_mixer(z, glr, w_g2, b_g, g_head, s0, layer, units):
    T = z.shape[0]
    _, DB, H, dk, dv = s0.shape
    B = units.n_prompt
    hb = GLA_HEADS_PER_STEP if H % GLA_HEADS_PER_STEP == 0 else 1
    ng = H // hb
    kw = hb * dk
    vw = hb * dv
    assert (2 * H * dk) % vw == 0
    v_blk0 = (2 * H * dk) // vw
    sidx = units.sample_index

    def pidx(u):
        return jnp.minimum(units.stream(u), B - 1)

    nbytes = (2 * (2 * CHUNK * kw * 2 + 2 * CHUNK * vw * 2 + CHUNK * LANES * 4 + LANES * kw * 2 + CHUNK * vw * 2
                   + 3 * kw * dv * 4) + kw * dv * 4 * 4)
    return pl.pallas_call(
        functools.partial(_gla_kernel, units=units),
        grid=(ng, units.total),
        in_specs=[pl.BlockSpec((CHUNK, kw), lambda g, u: (u, g)),
                  pl.BlockSpec((CHUNK, kw), lambda g, u: (u, ng + g)),
                  pl.BlockSpec((CHUNK, vw), lambda g, u: (u, v_blk0 + g)),
                  pl.BlockSpec((CHUNK, vw), lambda g, u: (u, v_blk0 + ng + g)),
                  pl.BlockSpec((CHUNK, LANES), lambda g, u: (u, 0)),
                  pl.BlockSpec((LANES, kw), lambda g, u: (0, g)),
                  pl.BlockSpec((None, 1, kw), lambda g, u: (layer, 0, g)),
                  pl.BlockSpec((None, 1, vw), lambda g, u: (layer, 0, g)),
                  pl.BlockSpec((None, None, hb, dk, dv), lambda g, u: (layer, sidx(u), g, 0, 0))],
        out_specs=[pl.BlockSpec((CHUNK, vw), lambda g, u: (u, g)),
                   pl.BlockSpec((None, hb, dk, dv), lambda g, u: (pidx(u), g, 0, 0)),
                   pl.BlockSpec((None, hb, dk, dv), lambda g, u: (sidx(u), g, 0, 0))],
        out_shape=[jax.ShapeDtypeStruct((T, H * dv), BF16),
                   jax.ShapeDtypeStruct((B, H, dk, dv), F32),
                   jax.ShapeDtypeStruct((DB, H, dk, dv), F32)],
        scratch_shapes=[pltpu.VMEM((hb, dv, dk), F32)],
        compiler_params=_params(("parallel", "arbitrary"), nbytes),
        name="gla",
    )(z, z, z, z, glr, w_g2, b_g[:, None, :], g_head[:, None, :], s0)


def _pad_rows(w, height):
    return jnp.pad(w, ((0, height - w.shape[0]), (0, 0)))


def kernel(x_prompt, x_sample, cache_swa_k, cache_swa_v, state_mlstm_C, state_mlstm_n, state_mlstm_m, state_gla_S,
           norm_mix, norm_ffn, norm_final, w_even_in, b_mlstm_i, b_mlstm_f, mlstm_head_norm, swa_sinks, w_even_out,
           w_odd_in, w_gla_gate, b_gla_gate, gla_head_norm, w_odd_out, w_ffn_in, w_ffn_out):
    B, S, D = x_prompt.shape
    DB, DS, _ = x_sample.shape
    depth = norm_mix.shape[0]
    assert DS == CHUNK and S % CHUNK == 0
    MH, dqk, dv = state_mlstm_C.shape[2:]
    n_kv = cache_swa_k.shape[3]
    w_buf = cache_swa_k.shape[2]
    assert w_buf == 2 * CHUNK and cache_swa_k.shape[4] == SWA_HEAD_DIM
    GH, gdk, gdv = state_gla_S.shape[2:]
    rank = w_gla_gate.shape[1]
    m_w = 2 * MH * dqk + 2 * MH * dv
    sq_w = n_kv * SWA_GROUP * SWA_HEAD_DIM
    kvw = n_kv * SWA_HEAD_DIM
    s_w = sq_w + 2 * kvw
    g_w = 2 * GH * gdk + 2 * GH * gdv
    units = Units(B, S // CHUNK, DB)
    TP = B * S

    assert depth >= 1
    w_even_in_t = jnp.swapaxes(w_even_in, 1, 2)
    w_odd_in_t = jnp.swapaxes(w_odd_in, 1, 2)

    p_k, p_v, p_C, p_n, p_m, p_S = [], [], [], [], [], []
    s_k, s_v, s_C, s_n, s_m, s_S = [], [], [], [], [], []
    for l in range(depth):
        e = l // 2
        if l % 2 == 0:
            w_gate = _pad_rows(w_even_in_t[e, m_w:m_w + 2 * MH], LANES)[None]
            w_attn = w_even_in_t[e, m_w + 2 * MH:][None]
            if l == 0:
                x, h = stack_and_norm(x_prompt.reshape(TP, D), x_sample.reshape(DB * DS, D), norm_mix, 0)
            else:
                h = rmsnorm(x, norm_mix, l, BF16)
            zm, w_out_b = project([h], w_even_in_t, e, m_w, BF16, w_nk=True, cast=(w_ffn_out, l),
                                  name="even_in_mlstm")
            zs = project([h], w_attn, 0, s_w, w_nk=True, name="even_in_attn")
            gates = project([h], w_gate, 0, LANES, w_nk=True, name="even_gates")
            ck = cache_swa_k[e].reshape(DB, w_buf, kvw)
            cv = cache_swa_v[e].reshape(DB, w_buf, kvw)
            (hm, pc, pn, pm, sc, sn, sm), (hs,) = run_unit_kernels(
                [mlstm_plan(zm, gates, b_mlstm_i, b_mlstm_f, mlstm_head_norm, state_mlstm_C, state_mlstm_n,
                            state_mlstm_m, e, units),
                 swa_plan(zs, swa_sinks, ck, cv, e, units, n_kv)], units, "even_mixers")
            pn, pm, sn, sm = pn[:, :, 0, :], pm[:, :, 0, 0], sn[:, :, 0, :], sm[:, :, 0, 0]
            x = project([hm, hs], w_even_out, e, D, residual=x, name="even_out")
            zk = zs[:, sq_w:sq_w + kvw]
            zv = zs[:, sq_w + kvw:]
            kv_shape = (w_buf, n_kv, SWA_HEAD_DIM)
            p_k.append(zk[:TP].reshape(B, S, kvw)[:, -w_buf:].reshape((B,) + kv_shape))
            p_v.append(zv[:TP].reshape(B, S, kvw)[:, -w_buf:].reshape((B,) + kv_shape))
            s_k.append(jnp.concatenate([ck[:, CHUNK:], zk[TP:].reshape(DB, DS, kvw)], axis=1).reshape((DB,) + kv_shape))
            s_v.append(jnp.concatenate([cv[:, CHUNK:], zv[TP:].reshape(DB, DS, kvw)], axis=1).reshape((DB,) + kv_shape))
            p_C.append(pc); p_n.append(pn); p_m.append(pm)
            s_C.append(sc); s_n.append(sn); s_m.append(sm)
        else:
            z, h, w_out_b = normed_project(x, norm_mix, l, w_odd_in_t, e, g_w, BF16, w_nk=True, emit_h=True,
                                           cast=(w_ffn_out, l), name="odd_in")
            w_rank = _pad_rows(w_odd_in_t[e, g_w:], LANES)[None]
            glr = project([h], w_rank, 0, LANES, w_nk=True, name="odd_gate_rank")
            w_g2 = jnp.pad(w_gla_gate[e], ((0, LANES - rank), (0, 0))).astype(BF16)
            mix, ps, ss = gla_mixer(z, glr, w_g2, b_gla_gate, gla_head_norm, state_gla_S, e, units)
            x = project([mix], w_odd_out, e, D, residual=x, name="odd_out")
            p_S.append(ps); s_S.append(ss)
        act = normed_project(x, norm_ffn, l, w_ffn_in, l, w_ffn_in.shape[2] // 2, BF16, swiglu=True, name="ffn_in")
        x = project([act], w_out_b[None], 0, D, residual=x, half_rows=True, name="ffn_out")
    y_prompt = rmsnorm(x, norm_final[None], 0, F32, 0, TP)
    y_sample = rmsnorm(x, norm_final[None], 0, F32, TP, DB * DS)
    return (y_prompt.reshape(B, S, D), y_sample.reshape(DB, DS, D),
            jnp.stack(p_k), jnp.stack(p_v), jnp.stack(p_C), jnp.stack(p_n), jnp.stack(p_m), jnp.stack(p_S),
            jnp.stack(s_k), jnp.stack(s_v), jnp.stack(s_C), jnp.stack(s_n), jnp.stack(s_m), jnp.stack(s_S))
```

```python
import functools
import itertools

import jax
import jax.numpy as jnp
from jax import lax
from jax.experimental import pallas as pl
from jax.experimental.pallas import tpu as pltpu

F32 = jnp.float32
BF16 = jnp.bfloat16

CHUNK = 64
SWA_HEAD_DIM = 64
SWA_GROUP = 8
LANES = 128
GLA_SUB = 16
GLA_HEADS_PER_STEP = 8
GLA_TAU = 16.0
LOG2E = 1.4426950408889634
EPS = 1e-6
NEG_INF = float("-inf")

V7X_VMEM_BYTES = 64 * 1024 * 1024
VMEM_CAP_BYTES = V7X_VMEM_BYTES - 4 * 1024 * 1024

NT = (((1,), (1,)), ((), ()))
TN = (((0,), (0,)), ((), ()))


def _pick(n, cands):
    for c in cands:
        if n % c == 0:
            return c
    raise ValueError(f"no tile in {cands} divides {n}")


def _params(semantics, block_bytes):
    limit = min(VMEM_CAP_BYTES, block_bytes + 16 * 1024 * 1024)
    return pltpu.CompilerParams(dimension_semantics=semantics, vmem_limit_bytes=int(limit))


def _log_sigmoid(x):
    return jnp.minimum(x, 0.0) - jnp.log(1.0 + jnp.exp(-jnp.abs(x)))


def _sigmoid(x):
    return 1.0 / (1.0 + jnp.exp(-x))


def _split3(x):
    hi = x.astype(BF16)
    r1 = x - hi.astype(F32)
    mid = r1.astype(BF16)
    lo = (r1 - mid.astype(F32)).astype(BF16)
    return hi, mid, lo


def _rmsnorm_kernel(x_ref, g_ref, o_ref):
    x = x_ref[...]
    y = x * lax.rsqrt(jnp.mean(x * x, axis=-1, keepdims=True) + EPS)
    o_ref[...] = (y * g_ref[...]).astype(o_ref.dtype)


def rmsnorm(x, g, layer, out_dtype, row0=0, rows=None):
    D = x.shape[1]
    T = x.shape[0] if rows is None else rows
    tr = _pick(T, (512, 272, 256, 160, 128, 64))
    assert row0 % tr == 0
    blk0 = row0 // tr
    nbytes = 2 * tr * D * (4 + jnp.dtype(out_dtype).itemsize)
    return pl.pallas_call(
        _rmsnorm_kernel,
        grid=(T // tr,),
        in_specs=[pl.BlockSpec((tr, D), lambda i: (blk0 + i, 0)), pl.BlockSpec((None, 1, D), lambda i: (layer, 0, 0))],
        out_specs=pl.BlockSpec((tr, D), lambda i: (i, 0)),
        out_shape=jax.ShapeDtypeStruct((T, D), out_dtype),
        compiler_params=_params(("parallel",), nbytes),
        name="rmsnorm",
    )(x, g[:, None, :])


def _stack_norm_kernel(xp_ref, xs_ref, g_ref, x_ref, h_ref, *, n_prompt_tiles):
    x = jnp.where(pl.program_id(0) < n_prompt_tiles, xp_ref[...], xs_ref[...])
    x_ref[...] = x
    y = x * lax.rsqrt(jnp.mean(x * x, axis=-1, keepdims=True) + EPS)
    h_ref[...] = (y * g_ref[...]).astype(h_ref.dtype)


def stack_and_norm(x_prompt, x_sample, g, layer):
    TP, D = x_prompt.shape
    TS = x_sample.shape[0]
    tr = next(c for c in (256, 128, 64) if TP % c == 0 and TS % c == 0)
    n_p = TP // tr
    T = TP + TS
    nbytes = 2 * tr * D * (4 + 4 + 4 + 2)
    return pl.pallas_call(
        functools.partial(_stack_norm_kernel, n_prompt_tiles=n_p),
        grid=(T // tr,),
        in_specs=[pl.BlockSpec((tr, D), lambda i: (jnp.minimum(i, n_p - 1), 0)),
                  pl.BlockSpec((tr, D), lambda i: (jnp.maximum(i - n_p, 0), 0)),
                  pl.BlockSpec((None, 1, D), lambda i: (layer, 0, 0))],
        out_specs=[pl.BlockSpec((tr, D), lambda i: (i, 0)), pl.BlockSpec((tr, D), lambda i: (i, 0))],
        out_shape=[jax.ShapeDtypeStruct((T, D), F32), jax.ShapeDtypeStruct((T, D), BF16)],
        compiler_params=_params(("parallel",), nbytes),
        name="stack_norm",
    )(x_prompt, x_sample, g[:, None, :])


def _m_tile(T):
    return _pick(T, (1088, 1024, 640, 512, 320, 256, 128, 64))


def _proj_kernel(*refs, n_x, has_res, w_nk, has_cast):
    x_refs = refs[:n_x]
    w_ref = refs[n_x]
    r_ref = refs[n_x + 1] if has_res else None
    o_ref = refs[n_x + 1 + has_res + has_cast]
    if has_cast:
        refs[-1][...] = refs[n_x + 1 + has_res][...].astype(BF16)
    k0 = 0
    acc = None
    for x_ref in x_refs:
        kx = x_ref.shape[1]
        if w_nk:
            w_blk = w_ref[0, :, k0:k0 + kx] if len(w_ref.shape) == 3 else w_ref[:, k0:k0 + kx]
            part = lax.dot_general(x_ref[...], w_blk.astype(BF16), NT, preferred_element_type=F32)
        else:
            part = jnp.dot(x_ref[...], w_ref[k0:k0 + kx, :].astype(BF16), preferred_element_type=F32)
        acc = part if acc is None else acc + part
        k0 += kx
    if has_res:
        acc = r_ref[...] + acc
    o_ref[...] = acc.astype(o_ref.dtype)


def _cast_rows(n_rows, n_steps):
    return min(r for r in range(16, n_rows + 1, 16) if n_rows % r == 0 and n_rows // r <= n_steps)


def project(xs, w, layer, n_cols, out_dtype=F32, residual=None, w_nk=False, w_row0=0, half_rows=False, cast=None,
            name="project"):
    T = xs[0].shape[0]
    K = w.shape[2] if w_nk else w.shape[1]
    assert sum(x.shape[1] for x in xs) == K
    tm = _m_tile(T)
    if half_rows and tm % 32 == 0:
        tm //= 2
    tn = _pick(n_cols, (512, 256, 128))
    nj = n_cols // tn
    osz = jnp.dtype(out_dtype).itemsize
    wsz = jnp.dtype(w.dtype).itemsize
    nbytes = 2 * (tm * K * 2 + K * tn * wsz + tm * tn * osz) + K * tn * 2 + tm * tn * 4
    in_specs = [pl.BlockSpec((tm, x.shape[1]), lambda i, j: (i, 0)) for x in xs]
    if w_nk and w_row0:
        assert w_row0 % 8 == 0
        in_specs.append(pl.BlockSpec((pl.Element(1), pl.Element(tn), pl.Element(K)),
                                     lambda i, j: (layer, pl.multiple_of(w_row0 + j * tn, 8), 0)))
    elif w_nk:
        in_specs.append(pl.BlockSpec((None, tn, K), lambda i, j: (layer, j, 0)))
    else:
        assert w_row0 == 0
        in_specs.append(pl.BlockSpec((None, K, tn), lambda i, j: (layer, 0, j)))
    args = list(xs) + [w]
    if residual is not None:
        in_specs.append(pl.BlockSpec((tm, tn), lambda i, j: (i, j)))
        args.append(residual)
        nbytes += 2 * tm * tn * 4
    out_specs = [pl.BlockSpec((tm, tn), lambda i, j: (i, j))]
    out_shape = [jax.ShapeDtypeStruct((T, n_cols), out_dtype)]
    if cast is not None:
        src, src_layer = cast
        n_rows, width = src.shape[1:]
        crows = _cast_rows(n_rows, (T // tm) * nj)
        last = n_rows // crows - 1
        in_specs.append(pl.BlockSpec((None, crows, width), lambda i, j: (src_layer, jnp.minimum(i * nj + j, last), 0)))
        args.append(src)
        out_specs.append(pl.BlockSpec((crows, width), lambda i, j: (jnp.minimum(i * nj + j, last), 0)))
        out_shape.append(jax.ShapeDtypeStruct((n_rows, width), BF16))
        nbytes += 2 * crows * width * 6
    outs = pl.pallas_call(
        functools.partial(_proj_kernel, n_x=len(xs), has_res=residual is not None, w_nk=w_nk,
                          has_cast=cast is not None),
        grid=(T // tm, nj),
        in_specs=in_specs,
        out_specs=out_specs,
        out_shape=out_shape,
        compiler_params=_params(("parallel" if cast is None else "arbitrary", "arbitrary"), nbytes),
        name=name,
    )(*args)
    return outs if cast is not None else outs[0]


def _normed_kernel(*refs, n_slabs, slab_rows, n_w, w_nk, emit_h, swiglu, has_cast):
    x_ref, g_ref = refs[:2]
    w_refs = refs[2:2 + n_w]
    src_ref = refs[2 + n_w] if has_cast else None
    o_ref = refs[2 + n_w + has_cast]
    hout_ref = refs[3 + n_w + has_cast] if emit_h else None
    dst_ref = refs[-2] if has_cast else None
    h2_ref = refs[-1]
    t = pl.program_id(0)
    slab = jnp.where(t < pl.num_programs(0) - 1, jnp.minimum(pl.program_id(1), n_slabs - 1), n_slabs - 1)

    def norm_slab(buf):
        x = x_ref[...]
        y = x * lax.rsqrt(jnp.mean(x * x, axis=-1, keepdims=True) + EPS)
        yb = (y * g_ref[...]).astype(BF16)
        h2_ref[buf, pl.ds(pl.multiple_of(slab * slab_rows, slab_rows), slab_rows), :] = yb
        if emit_h:
            hout_ref[...] = yb

    def matmul(h, w_ref):
        if w_nk:
            return lax.dot_general(h, w_ref[...].astype(BF16), NT, preferred_element_type=F32)
        return jnp.dot(h, w_ref[...].astype(BF16), preferred_element_type=F32)

    @pl.when(t == 0)
    def _():
        norm_slab(0)

    @pl.when(t > 0)
    def _():
        h = h2_ref[(t - 1) % 2]
        if swiglu:
            g = matmul(h, w_refs[0])
            u = matmul(h, w_refs[1])
            o_ref[...] = (g * _sigmoid(g) * u).astype(o_ref.dtype)
        else:
            o_ref[...] = matmul(h, w_refs[0]).astype(o_ref.dtype)
        norm_slab(t % 2)
        if has_cast:
            dst_ref[...] = src_ref[...].astype(BF16)


def _slab_count(tm, nj):
    units16 = tm // 16
    return max(d for d in range(1, units16 + 1) if units16 % d == 0 and d <= nj)


def normed_project(x, g, g_layer, w, layer, n_cols, out_dtype, w_nk=False, swiglu=False, emit_h=False, cast=None,
                   name="normed_project"):
    T, K = x.shape
    tm = _pick(T, (2176, 1088, 640, 512, 256, 128, 64)) if swiglu else _m_tile(T)
    cands = [c for c in ((256, 128) if swiglu else (512, 256, 128)) if n_cols % c == 0]
    tn = next((c for c in cands if n_cols // c >= tm // CHUNK), cands[-1])
    nj = n_cols // tn
    n_tiles = T // tm
    ns = _slab_count(tm, nj)
    rows = tm // ns
    osz = jnp.dtype(out_dtype).itemsize
    n_w = 2 if swiglu else 1

    def row_tile(t):
        return jnp.maximum(t - 1, 0)

    def col(t, j):
        return jnp.where(t == 0, 0, j)

    def slab_idx(t, j):
        return jnp.where(t < n_tiles, t * ns + jnp.minimum(j, ns - 1), n_tiles * ns - 1)

    if w_nk:
        w_specs = [pl.BlockSpec((None, tn, K), lambda t, j, o=o: (layer, col(t, j) + o, 0)) for o in range(n_w)]
    else:
        w_specs = [pl.BlockSpec((None, K, tn), lambda t, j, o=o: (layer, 0, col(t, j) + o * nj)) for o in range(n_w)]
    out_specs = [pl.BlockSpec((tm, tn), lambda t, j: (row_tile(t), col(t, j)))]
    out_shape = [jax.ShapeDtypeStruct((T, n_cols), out_dtype)]
    if emit_h:
        out_specs.append(pl.BlockSpec((rows, K), lambda t, j: (slab_idx(t, j), 0)))
        out_shape.append(jax.ShapeDtypeStruct((T, K), BF16))
    nbytes = (2 * tm * K * 2 + 2 * rows * K * (4 + 2 * emit_h) + n_w * (2 * K * tn * 4 + K * tn * 2)
              + 2 * tm * tn * osz + (1 + n_w) * tm * tn * 4)
    args = [x, g[:, None, :]] + [w] * n_w
    in_specs = [pl.BlockSpec((rows, K), lambda t, j: (slab_idx(t, j), 0)),
                pl.BlockSpec((None, 1, K), lambda t, j: (g_layer, 0, 0))] + w_specs
    if cast is not None:
        src, src_layer = cast
        n_rows, width = src.shape[1:]
        crows = _cast_rows(n_rows, n_tiles * nj)
        last = n_rows // crows - 1

        def cast_idx(t, j):
            return jnp.clip((t - 1) * nj + j, 0, last)

        in_specs.append(pl.BlockSpec((None, crows, width), lambda t, j: (src_layer, cast_idx(t, j), 0)))
        args.append(src)
        out_specs.append(pl.BlockSpec((crows, width), lambda t, j: (cast_idx(t, j), 0)))
        out_shape.append(jax.ShapeDtypeStruct((n_rows, width), BF16))
        nbytes += 2 * crows * width * 6
    outs = pl.pallas_call(
        functools.partial(_normed_kernel, n_slabs=ns, slab_rows=rows, n_w=n_w, w_nk=w_nk, emit_h=emit_h,
                          swiglu=swiglu, has_cast=cast is not None),
        grid=(n_tiles + 1, nj),
        in_specs=in_specs,
        out_specs=out_specs,
        out_shape=out_shape,
        scratch_shapes=[pltpu.VMEM((2, tm, K), BF16)],
        compiler_params=_params(("arbitrary", "arbitrary"), nbytes),
        name=name,
    )(*args)
    return outs if len(outs) > 1 else outs[0]


class Units:
    def __init__(self, n_prompt, chunks_per_prompt, n_sample):
        self.ncp = chunks_per_prompt
        self.up = n_prompt * chunks_per_prompt
        self.n_prompt = n_prompt
        self.n_sample = n_sample
        self.total = self.up + n_sample
        self.n_streams = n_prompt + n_sample

    def is_sample(self, u):
        return u >= self.up

    def stream(self, u):
        return jnp.where(u < self.up, u // self.ncp, self.n_prompt + u - self.up)

    def sample_index(self, u):
        return jnp.maximum(u - self.up, 0)

    def first(self, u):
        return jnp.logical_or(u >= self.up, u % self.ncp == 0)

    def last(self, u):
        return jnp.logical_or(u >= self.up, u % self.ncp == self.ncp - 1)


def _mlstm_steps(bi_ref, bf_ref, m0_ref, q_ref, k_ref, v_ref, og_ref, gate_ref, gh_ref, c0_ref, n0_ref,
                 hm_ref, cp_ref, np_ref, mp_ref, cs_ref, ns_ref, ms_ref, c_s, n_s, m_s, *, units, n_heads, layer):
    u = pl.program_id(0)
    is_sample = units.is_sample(u)
    is_prompt = jnp.logical_not(is_sample)
    L = CHUNK
    H = n_heads
    dqk = c_s.shape[1]
    dv = c_s.shape[2]

    @pl.when(jnp.logical_and(units.first(u), is_prompt))
    def _():
        c_s[...] = jnp.zeros_like(c_s)
        n_s[...] = jnp.zeros_like(n_s)
        m_s[...] = jnp.zeros_like(m_s)

    @pl.when(is_sample)
    def _():
        c_s[...] = c0_ref[...]
        n_s[...] = n0_ref[...]
        for hh in range(H):
            m_s[hh] = jnp.full(m_s.shape[1:], m0_ref[layer * units.n_sample + units.sample_index(u), hh], F32)

    yield
    gates = gate_ref[...]
    lane = lax.broadcasted_iota(jnp.int32, gates.shape, 1)
    row = lax.broadcasted_iota(jnp.int32, (L, L), 0)
    col = lax.broadcasted_iota(jnp.int32, (L, L), 1)
    eye = row == col
    tril = col <= row
    triu = row <= col

    heads = range(H)
    qs = [slice(hh * dqk, (hh + 1) * dqk) for hh in heads]
    vs = [slice(hh * dv, (hh + 1) * dv) for hh in heads]

    ig_col = [jnp.sum(jnp.where(lane == hh, gates, 0.0), axis=1, keepdims=True) + bi_ref[layer, hh] for hh in heads]
    lf_col = [_log_sigmoid(jnp.sum(jnp.where(lane == H + hh, gates, 0.0), axis=1, keepdims=True) + bf_ref[layer, hh])
              for hh in heads]
    ig_row = [jnp.sum(jnp.where(eye, ig_col[hh], 0.0), axis=0, keepdims=True) for hh in heads]
    lf_row = [jnp.sum(jnp.where(eye, lf_col[hh], 0.0), axis=0, keepdims=True) for hh in heads]
    b_col = [jnp.sum(jnp.where(tril, lf_row[hh], 0.0), axis=1, keepdims=True) for hh in heads]
    b_row = [jnp.sum(jnp.where(triu, lf_col[hh], 0.0), axis=0, keepdims=True) for hh in heads]
    m_prev = [m_s[hh][:, :1] for hh in heads]
    m_t, w_intra, w_inter = [], [], []
    for hh in heads:
        d = jnp.where(tril, b_col[hh] - b_row[hh] + ig_row[hh], NEG_INF)
        inter = b_col[hh] + m_prev[hh]
        m_c = jnp.maximum(inter, jnp.max(d, axis=1, keepdims=True))
        m_t.append(m_c)
        w_intra.append(jnp.exp(d - m_c))
        w_inter.append(jnp.exp(inter - m_c))

    yield
    a = [lax.dot_general(q_ref[:, qs[hh]].astype(BF16),
                         (k_ref[:, qs[hh]].astype(F32) * (dqk ** -0.5)).astype(BF16), NT,
                         preferred_element_type=F32) * w_intra[hh] for hh in heads]

    yield
    hval = []
    for hh in heads:
        q = q_ref[:, qs[hh]].astype(F32)
        num = (jnp.dot(a[hh].astype(BF16), v_ref[:, vs[hh]].astype(BF16), preferred_element_type=F32)
               + jnp.dot(q_ref[:, qs[hh]].astype(BF16), c_s[hh].astype(BF16), preferred_element_type=F32)
               * w_inter[hh])
        den = (jnp.sum(a[hh], axis=1, keepdims=True)
               + jnp.sum(q * n_s[hh], axis=1, keepdims=True) * w_inter[hh])
        den = jnp.maximum(jnp.abs(den), jnp.exp(-m_t[hh]))
        hval.append(num / den)

    yield
    for hh in heads:
        y = hval[hh] * lax.rsqrt(jnp.mean(hval[hh] * hval[hh], axis=1, keepdims=True) + EPS)
        gate_out = _sigmoid(og_ref[:, vs[hh]].astype(F32))
        hm_ref[:, vs[hh]] = (y * gh_ref[:, vs[hh]] * gate_out).astype(hm_ref.dtype)

    yield
    for hh in heads:
        k = k_ref[:, qs[hh]].astype(F32) * (dqk ** -0.5)
        m_new = m_t[hh][L - 1:L, :]
        b_last = b_col[hh][L - 1:L, :]
        w_last = jnp.exp(b_last - b_col[hh] + ig_col[hh] - m_new)
        decay = jnp.exp(b_last + m_prev[hh] - m_new)
        k_w = k * w_last
        c_s[hh] = decay * c_s[hh] + lax.dot_general(k_w.astype(BF16), v_ref[:, vs[hh]].astype(BF16), TN,
                                                    preferred_element_type=F32)
        n_s[hh] = decay * n_s[hh] + jnp.sum(k_w, axis=0, keepdims=True)
        m_s[hh] = jnp.broadcast_to(m_new, m_s.shape[1:])

    yield
    @pl.when(jnp.logical_and(units.last(u), is_prompt))
    def _():
        cp_ref[...] = c_s[...]
        np_ref[...] = n_s[...]
        mp_ref[...] = m_s[...]

    @pl.when(is_sample)
    def _():
        cs_ref[...] = c_s[...]
        ns_ref[...] = n_s[...]
        ms_ref[...] = m_s[...]


def mlstm_plan(z, gates, b_i, b_f, g_head, c0, n0, m0, layer, units):
    T = z.shape[0]
    n_layers, DB, H, dqk, dv = c0.shape
    B = units.n_prompt
    qw = H * dqk
    vw = H * dv
    assert (2 * qw) % vw == 0
    v_blk = (2 * qw) // vw
    sidx = units.sample_index

    def pidx(u):
        return jnp.minimum(units.stream(u), B - 1)

    smem = pl.BlockSpec(memory_space=pltpu.SMEM)

    def state_specs(idx):
        return [pl.BlockSpec((None, H, dqk, dv), lambda u: (idx(u), 0, 0, 0)),
                pl.BlockSpec((None, H, 1, dqk), lambda u: (idx(u), 0, 0, 0)),
                pl.BlockSpec((None, H, 1, LANES), lambda u: (idx(u), 0, 0, 0))]

    def state_shapes(n):
        return [jax.ShapeDtypeStruct((n, H, dqk, dv), F32), jax.ShapeDtypeStruct((n, H, 1, dqk), F32),
                jax.ShapeDtypeStruct((n, H, 1, LANES), F32)]

    return dict(
        steps=functools.partial(_mlstm_steps, units=units, n_heads=H, layer=layer),
        in_specs=[smem, smem, smem,
                  pl.BlockSpec((CHUNK, qw), lambda u: (u, 0)),
                  pl.BlockSpec((CHUNK, qw), lambda u: (u, 1)),
                  pl.BlockSpec((CHUNK, vw), lambda u: (u, v_blk)),
                  pl.BlockSpec((CHUNK, vw), lambda u: (u, v_blk + 1)),
                  pl.BlockSpec((CHUNK, LANES), lambda u: (u, 0)),
                  pl.BlockSpec((None, 1, vw), lambda u: (layer, 0, 0)),
                  pl.BlockSpec((None, None, H, dqk, dv), lambda u: (layer, sidx(u), 0, 0, 0)),
                  pl.BlockSpec((None, None, H, 1, dqk), lambda u: (layer, sidx(u), 0, 0, 0))],
        args=[b_i, b_f, m0.reshape(n_layers * DB, H), z, z, z, z, gates, g_head[:, None, :], c0,
              n0.reshape(n_layers, DB, H, 1, dqk)],
        out_specs=[pl.BlockSpec((CHUNK, vw), lambda u: (u, 0))] + state_specs(pidx) + state_specs(sidx),
        out_shape=[jax.ShapeDtypeStruct((T, vw), BF16)] + state_shapes(B) + state_shapes(DB),
        scratch=[pltpu.VMEM((H, dqk, dv), F32), pltpu.VMEM((H, 1, dqk), F32), pltpu.VMEM((H, 1, LANES), F32)],
        nbytes=(2 * (2 * CHUNK * qw * 2 + 2 * CHUNK * vw * 2 + CHUNK * LANES * 4 + CHUNK * vw * 2 + 3 * qw * dv * 4)
                + qw * dv * 4 * 4))


def _unit_kernels(*refs, plans):
    n_in = sum(len(p["in_specs"]) for p in plans)
    n_out = sum(len(p["out_specs"]) for p in plans)
    i0, o0, s0 = 0, n_in, n_in + n_out
    gens = []
    for p in plans:
        ni, no, nscr = len(p["in_specs"]), len(p["out_specs"]), len(p["scratch"])
        gens.append(p["steps"](*refs[i0:i0 + ni], *refs[o0:o0 + no], *refs[s0:s0 + nscr]))
        i0, o0, s0 = i0 + ni, o0 + no, s0 + nscr
    for _ in itertools.zip_longest(*gens):
        pass


def run_unit_kernels(plans, units, name):
    outs = pl.pallas_call(
        functools.partial(_unit_kernels, plans=plans),
        grid=(units.total,),
        in_specs=[spec for p in plans for spec in p["in_specs"]],
        out_specs=[spec for p in plans for spec in p["out_specs"]],
        out_shape=[shape for p in plans for shape in p["out_shape"]],
        scratch_shapes=[scr for p in plans for scr in p["scratch"]],
        compiler_params=_params(("arbitrary",), sum(p["nbytes"] for p in plans)),
        name=name,
    )(*[arg for p in plans for arg in p["args"]])
    split, k = [], 0
    for p in plans:
        split.append(outs[k:k + len(p["out_specs"])])
        k += len(p["out_specs"])
    return split


def _swa_steps(sink_ref, q_ref, k0_ref, k1_ref, k2_ref, v0_ref, v1_ref, v2_ref, ck1_ref, ck2_ref, cv1_ref, cv2_ref,
               o_ref, bias_s, *, units, n_kv, layer):
    u = pl.program_id(0)
    is_sample = units.is_sample(u)
    L = CHUNK
    W = 2 * L
    NK = W + L
    hd = SWA_HEAD_DIM
    pairs = SWA_GROUP // 2
    R = pairs * L
    n_heads = n_kv * SWA_GROUP
    pair_of_row = lax.broadcasted_iota(jnp.int32, (R, 1), 0) // L

    @pl.when(u == 0)
    def _():
        jj = lax.broadcasted_iota(jnp.int32, (R, NK), 1)
        tt = lax.broadcasted_iota(jnp.int32, (R, NK), 0) % L
        dist = jnp.abs(tt + W - jj).astype(F32)
        for g in range(n_kv):
            for odd in range(2):
                head = g * SWA_GROUP + 2 * pair_of_row + odd
                slope = jnp.exp2(-8.0 * (head + 1).astype(F32) / n_heads) * LOG2E
                pen = slope * dist
                for v in range(3):
                    bias_s[v, 2 * g + odd] = jnp.where(jj >= W - v * L, pen, -NEG_INF)

    yield
    def pick(c_ref, z_ref):
        return jnp.where(is_sample, c_ref[...], z_ref[...])

    k_all = jnp.concatenate([pick(ck2_ref, k2_ref), pick(ck1_ref, k1_ref), k0_ref[...]], axis=0)
    v_all = jnp.concatenate([pick(cv2_ref, v2_ref), pick(cv1_ref, v1_ref), v0_ref[...]], axis=0)
    variant = jnp.where(is_sample, 2, jnp.minimum(u % units.ncp, 2))
    lane = lax.broadcasted_iota(jnp.int32, (NK, LANES), 1)
    lo_lanes = lane < hd

    k_half, v_half, qg = [], [], []
    for g in range(n_kv):
        tile = (g * hd) // LANES
        kt = k_all[:, tile * LANES:(tile + 1) * LANES]
        vt = v_all[:, tile * LANES:(tile + 1) * LANES]
        kr = pltpu.roll(kt, hd, axis=1)
        vr = pltpu.roll(vt, hd, axis=1)
        if (g * hd) % LANES == 0:
            k_lo, k_hi, v_lo, v_hi = kt, kr, vt, vr
        else:
            k_lo, k_hi, v_lo, v_hi = kr, kt, vr, vt
        k_half += [jnp.where(lo_lanes, k_lo, 0.0).astype(BF16), jnp.where(lo_lanes, 0.0, k_hi).astype(BF16)]
        v_half += [jnp.where(lo_lanes, v_lo, 0.0).astype(BF16), jnp.where(lo_lanes, 0.0, v_hi).astype(BF16)]
        q = jnp.concatenate(
            [q_ref[:, (g * pairs + p) * LANES:(g * pairs + p + 1) * LANES] for p in range(pairs)], axis=0)
        qg.append((q * (hd ** -0.5 * LOG2E)).astype(BF16))

    yield
    chains = [(g, odd) for g in range(n_kv) for odd in range(2)]
    s = [lax.dot_general(qg[g], k_half[2 * g + odd], NT, preferred_element_type=F32) - bias_s[variant, 2 * g + odd]
         for g, odd in chains]
    sink = []
    for g, odd in chains:
        col = jnp.zeros((R, 1), F32)
        for p in range(pairs):
            col = jnp.where(pair_of_row == p, sink_ref[layer, g * SWA_GROUP + 2 * p + odd] * LOG2E, col)
        sink.append(col)
    yield
    mx = [jnp.maximum(jnp.max(s[c], axis=1, keepdims=True), sink[c]) for c in range(len(chains))]
    e = [jnp.exp2(s[c] - mx[c]) for c in range(len(chains))]
    inv_den = [1.0 / (jnp.sum(e[c], axis=1, keepdims=True) + jnp.exp2(sink[c] - mx[c])) for c in range(len(chains))]
    yield
    pv = [jnp.dot(e[c].astype(BF16), v_half[c], preferred_element_type=F32) * inv_den[c] for c in range(len(chains))]
    yield
    for g in range(n_kv):
        o = pv[2 * g] + pv[2 * g + 1]
        for p in range(pairs):
            o_ref[:, (g * pairs + p) * LANES:(g * pairs + p + 1) * LANES] = o[p * L:(p + 1) * L].astype(o_ref.dtype)


def swa_plan(z, sinks, cache_k, cache_v, layer, units, n_kv):
    T = z.shape[0]
    qw = n_kv * SWA_GROUP * SWA_HEAD_DIM
    kw = n_kv * SWA_HEAD_DIM
    assert kw % LANES == 0 and qw % kw == 0
    kb = qw // kw
    vb = kb + 1
    sidx = units.sample_index
    R = (SWA_GROUP // 2) * CHUNK
    NK = 3 * CHUNK

    def zspec(blk, back):
        return pl.BlockSpec((CHUNK, kw), lambda u: (jnp.maximum(u - back, 0), blk))

    def cspec(part):
        return pl.BlockSpec((None, CHUNK, kw), lambda u: (sidx(u), part, 0))

    bias_bytes = 3 * 2 * n_kv * R * 2 * LANES * 4
    return dict(
        steps=functools.partial(_swa_steps, units=units, n_kv=n_kv, layer=layer),
        in_specs=[pl.BlockSpec(memory_space=pltpu.SMEM),
                  pl.BlockSpec((CHUNK, qw), lambda u: (u, 0)),
                  zspec(kb, 0), zspec(kb, 1), zspec(kb, 2),
                  zspec(vb, 0), zspec(vb, 1), zspec(vb, 2),
                  cspec(1), cspec(0), cspec(1), cspec(0)],
        args=[sinks, z, z, z, z, z, z, z, cache_k, cache_k, cache_v, cache_v],
        out_specs=[pl.BlockSpec((CHUNK, qw), lambda u: (u, 0))],
        out_shape=[jax.ShapeDtypeStruct((T, qw), BF16)],
        scratch=[pltpu.VMEM((3, 2 * n_kv, R, NK), F32)],
        nbytes=2 * (CHUNK * qw * 4 + 10 * CHUNK * kw * 4 + CHUNK * qw * 2) + bias_bytes + 16 * R * NK * 4)


def _gla_kernel(q_ref, k_ref, v_ref, r_ref, glr_ref, wg_ref, bg_ref, gh_ref, s0_ref, o_ref, sp_ref, ss_ref, s_s, *,
                units):
    u = pl.program_id(1)
    is_sample = units.is_sample(u)
    is_prompt = jnp.logical_not(is_sample)
    L = CHUNK
    hb, dv, dk = s_s.shape

    @pl.when(jnp.logical_and(units.first(u), is_prompt))
    def _():
        s_s[...] = jnp.zeros_like(s_s)

    @pl.when(is_sample)
    def _():
        for hh in range(hb):
            s_s[hh] = s0_ref[hh].T

    glr = glr_ref[...].astype(BF16)
    row = lax.broadcasted_iota(jnp.int32, (L, L), 0)
    col = lax.broadcasted_iota(jnp.int32, (L, L), 1)
    tril = col <= row
    ones_tril = jnp.where(tril, 1.0, 0.0).astype(BF16)
    rr = lax.broadcasted_iota(jnp.int32, (L, dk), 0)
    n_sub = L // GLA_SUB

    heads = range(hb)
    ks = [slice(hh * dk, (hh + 1) * dk) for hh in heads]
    vs = [slice(hh * dv, (hh + 1) * dv) for hh in heads]
    lg = [_log_sigmoid(jnp.dot(glr, wg_ref[:, ks[hh]], preferred_element_type=F32) + bg_ref[:, ks[hh]])
          * (1.0 / GLA_TAU) for hh in heads]
    bc = [sum(jnp.dot(ones_tril, part, preferred_element_type=F32) for part in _split3(lg[hh])) for hh in heads]

    refs = [[bc[hh][i * GLA_SUB:i * GLA_SUB + 1, :] for i in range(n_sub)] for hh in heads]
    q_t = []
    for hh in heads:
        r_q = refs[hh][n_sub - 1]
        for i in range(n_sub - 2, -1, -1):
            r_q = jnp.where(rr < (i + 1) * GLA_SUB, refs[hh][i], r_q)
        q_t.append((q_ref[:, ks[hh]].astype(F32) * (dk ** -0.5) * jnp.exp(bc[hh] - r_q)).astype(BF16))
    blocks = [[] for _ in heads]
    for i in range(n_sub):
        for hh in heads:
            e_i = jnp.where(rr < (i + 1) * GLA_SUB, refs[hh][i] - bc[hh], NEG_INF)
            k_i = (k_ref[:, ks[hh]].astype(F32) * jnp.exp(e_i)).astype(BF16)
            blocks[hh].append(lax.dot_general(q_t[hh][i * GLA_SUB:(i + 1) * GLA_SUB], k_i, NT,
                                              preferred_element_type=F32))
    a = [jnp.where(tril, jnp.concatenate(blocks[hh], axis=0), 0.0).astype(BF16) for hh in heads]

    o = []
    for hh in heads:
        q_dec = (q_ref[:, ks[hh]].astype(F32) * (dk ** -0.5) * jnp.exp(bc[hh])).astype(BF16)
        o.append(jnp.dot(a[hh], v_ref[:, vs[hh]].astype(BF16), preferred_element_type=F32)
                 + lax.dot_general(q_dec, s_s[hh].astype(BF16), NT, preferred_element_type=F32))

    for hh in heads:
        b_last = bc[hh][L - 1:L, :]
        k_dec = (k_ref[:, ks[hh]].astype(F32) * jnp.exp(b_last - bc[hh])).astype(BF16)
        s_s[hh] = jnp.exp(b_last) * s_s[hh] + lax.dot_general(v_ref[:, vs[hh]].astype(BF16), k_dec, TN,
                                                               preferred_element_type=F32)

    for hh in heads:
        y = o[hh] * lax.rsqrt(jnp.mean(o[hh] * o[hh], axis=1, keepdims=True) + EPS)
        r = r_ref[:, vs[hh]].astype(F32)
        o_ref[:, vs[hh]] = (y * gh_ref[:, vs[hh]] * (r * _sigmoid(r))).astype(o_ref.dtype)

    @pl.when(jnp.logical_and(units.last(u), is_prompt))
    def _():
        for hh in range(hb):
            sp_ref[hh] = s_s[hh].T

    @pl.when(is_sample)
    def _():
        for hh in range(hb):
            ss_ref[hh] = s_s[hh].T


def gla_mixer(z, glr, w_g2, b_g, g_head, s0, layer, units):
    T = z.shape[0]
    _, DB, H, dk, dv = s0.shape
    B = units.n_prompt
    hb = GLA_HEADS_PER_STEP if H % GLA_HEADS_PER_STEP == 0 else 1
    ng = H // hb
    kw = hb * dk
    vw = hb * dv
    assert (2 * H * dk) % vw == 0
    v_blk0 = (2 * H * dk) // vw
    sidx = units.sample_index

    def pidx(u):
        return jnp.minimum(units.stream(u), B - 1)

    nbytes = (2 * (2 * CHUNK * kw * 2 + 2 * CHUNK * vw * 2 + CHUNK * LANES * 4 + LANES * kw * 2 + CHUNK * vw * 2
                   + 3 * kw * dv * 4) + kw * dv * 4 * 4)
    return pl.pallas_call(
        functools.partial(_gla_kernel, units=units),
        grid=(ng, units.total),
        in_specs=[pl.BlockSpec((CHUNK, kw), lambda g, u: (u, g)),
                  pl.BlockSpec((CHUNK, kw), lambda g, u: (u, ng + g)),
                  pl.BlockSpec((CHUNK, vw), lambda g, u: (u, v_blk0 + g)),
                  pl.BlockSpec((CHUNK, vw), lambda g, u: (u, v_blk0 + ng + g)),
                  pl.BlockSpec((CHUNK, LANES), lambda g, u: (u, 0)),
                  pl.BlockSpec((LANES, kw), lambda g, u: (0, g)),
                  pl.BlockSpec((None, 1, kw), lambda g, u: (layer, 0, g)),
                  pl.BlockSpec((None, 1, vw), lambda g, u: (layer, 0, g)),
                  pl.BlockSpec((None, None, hb, dk, dv), lambda g, u: (layer, sidx(u), g, 0, 0))],
        out_specs=[pl.BlockSpec((CHUNK, vw), lambda g, u: (u, g)),
                   pl.BlockSpec((None, hb, dk, dv), lambda g, u: (pidx(u), g, 0, 0)),
                   pl.BlockSpec((None, hb, dk, dv), lambda g, u: (sidx(u), g, 0, 0))],
        out_shape=[jax.ShapeDtypeStruct((T, H * dv), BF16),
                   jax.ShapeDtypeStruct((B, H, dk, dv), F32),
                   jax.ShapeDtypeStruct((DB, H, dk, dv), F32)],
        scratch_shapes=[pltpu.VMEM((hb, dv, dk), F32)],
        compiler_params=_params(("parallel", "arbitrary"), nbytes),
        name="gla",
    )(z, z, z, z, glr, w_g2, b_g[:, None, :], g_head[:, None, :], s0)


def _pad_rows(w, height):
    return jnp.pad(w, ((0, height - w.shape[0]), (0, 0)))


def kernel(x_prompt, x_sample, cache_swa_k, cache_swa_v, state_mlstm_C, state_mlstm_n, state_mlstm_m, state_gla_S,
           norm_mix, norm_ffn, norm_final, w_even_in, b_mlstm_i, b_mlstm_f, mlstm_head_norm, swa_sinks, w_even_out,
           w_odd_in, w_gla_gate, b_gla_gate, gla_head_norm, w_odd_out, w_ffn_in, w_ffn_out):
    B, S, D = x_prompt.shape
    DB, DS, _ = x_sample.shape
    depth = norm_mix.shape[0]
    assert DS == CHUNK and S % CHUNK == 0
    MH, dqk, dv = state_mlstm_C.shape[2:]
    n_kv = cache_swa_k.shape[3]
    w_buf = cache_swa_k.shape[2]
    assert w_buf == 2 * CHUNK and cache_swa_k.shape[4] == SWA_HEAD_DIM
    GH, gdk, gdv = state_gla_S.shape[2:]
    rank = w_gla_gate.shape[1]
    m_w = 2 * MH * dqk + 2 * MH * dv
    sq_w = n_kv * SWA_GROUP * SWA_HEAD_DIM
    kvw = n_kv * SWA_HEAD_DIM
    s_w = sq_w + 2 * kvw
    g_w = 2 * GH * gdk + 2 * GH * gdv
    units = Units(B, S // CHUNK, DB)
    TP = B * S

    assert depth >= 1
    w_even_in_t = jnp.swapaxes(w_even_in, 1, 2)
    w_odd_in_t = jnp.swapaxes(w_odd_in, 1, 2)

    p_k, p_v, p_C, p_n, p_m, p_S = [], [], [], [], [], []
    s_k, s_v, s_C, s_n, s_m, s_S = [], [], [], [], [], []
    for l in range(depth):
        e = l // 2
        if l % 2 == 0:
            if l == 0:
                x, h = stack_and_norm(x_prompt.reshape(TP, D), x_sample.reshape(DB * DS, D), norm_mix, 0)
            else:
                h = rmsnorm(x, norm_mix, l, BF16)
            zm, w_out_b = project([h], w_even_in_t, e, m_w, BF16, w_nk=True, cast=(w_ffn_out, l),
                                  name="even_in_mlstm")
            zs = project([h], w_even_in_t, e, s_w, w_nk=True, w_row0=m_w + 2 * MH, name="even_in_attn")
            gates = project([h], w_even_in_t, e, LANES, w_nk=True, w_row0=m_w, name="even_gates")
            ck = cache_swa_k[e].reshape(DB, w_buf, kvw)
            cv = cache_swa_v[e].reshape(DB, w_buf, kvw)
            (hm, pc, pn, pm, sc, sn, sm), (hs,) = run_unit_kernels(
                [mlstm_plan(zm, gates, b_mlstm_i, b_mlstm_f, mlstm_head_norm, state_mlstm_C, state_mlstm_n,
                            state_mlstm_m, e, units),
                 swa_plan(zs, swa_sinks, ck, cv, e, units, n_kv)], units, "even_mixers")
            pn, pm, sn, sm = pn[:, :, 0, :], pm[:, :, 0, 0], sn[:, :, 0, :], sm[:, :, 0, 0]
            x = project([hm, hs], w_even_out, e, D, residual=x, name="even_out")
            zk = zs[:, sq_w:sq_w + kvw]
            zv = zs[:, sq_w + kvw:]
            kv_shape = (w_buf, n_kv, SWA_HEAD_DIM)
            p_k.append(zk[:TP].reshape(B, S, kvw)[:, -w_buf:].reshape((B,) + kv_shape))
            p_v.append(zv[:TP].reshape(B, S, kvw)[:, -w_buf:].reshape((B,) + kv_shape))
            s_k.append(jnp.concatenate([ck[:, CHUNK:], zk[TP:].reshape(DB, DS, kvw)], axis=1).reshape((DB,) + kv_shape))
            s_v.append(jnp.concatenate([cv[:, CHUNK:], zv[TP:].reshape(DB, DS, kvw)], axis=1).reshape((DB,) + kv_shape))
            p_C.append(pc); p_n.append(pn); p_m.append(pm)
            s_C.append(sc); s_n.append(sn); s_m.append(sm)
        else:
            z, h, w_out_b = normed_project(x, norm_mix, l, w_odd_in_t, e, g_w, BF16, w_nk=True, emit_h=True,
                                           cast=(w_ffn_out, l), name="odd_in")
            w_rank = _pad_rows(w_odd_in_t[e, g_w:], LANES)[None]
            glr = project([h], w_rank, 0, LANES, w_nk=True, name="odd_gate_rank")
            w_g2 = jnp.pad(w_gla_gate[e], ((0, LANES - rank), (0, 0))).astype(BF16)
            mix, ps, ss = gla_mixer(z, glr, w_g2, b_gla_gate, gla_head_norm, state_gla_S, e, units)
            x = project([mix], w_odd_out, e, D, residual=x, name="odd_out")
            p_S.append(ps); s_S.append(ss)
        act = normed_project(x, norm_ffn, l, w_ffn_in, l, w_ffn_in.shape[2] // 2, BF16, swiglu=True, name="ffn_in")
        x = project([act], w_out_b[None], 0, D, residual=x, half_rows=True, name="ffn_out")
    y_prompt = rmsnorm(x, norm_final[None], 0, F32, 0, TP)
    y_sample = rmsnorm(x, norm_final[None], 0, F32, TP, DB * DS)
    return (y_prompt.reshape(B, S, D), y_sample.reshape(DB, DS, D),
            jnp.stack(p_k), jnp.stack(p_v), jnp.stack(p_C), jnp.stack(p_n), jnp.stack(p_m), jnp.stack(p_S),
            jnp.stack(s_k), jnp.stack(s_v), jnp.stack(s_C), jnp.stack(s_n), jnp.stack(s_m), jnp.stack(s_S))
```

```python
import functools
import itertools

import jax
import jax.numpy as jnp
from jax import lax
from jax.experimental import pallas as pl
from jax.experimental.pallas import tpu as pltpu

F32 = jnp.float32
BF16 = jnp.bfloat16

CHUNK = 64
SWA_HEAD_DIM = 64
SWA_GROUP = 8
LANES = 128
GLA_SUB = 16
GLA_HEADS_PER_STEP = 8
GLA_TAU = 16.0
LOG2E = 1.4426950408889634
EPS = 1e-6
NEG_INF = float("-inf")

V7X_VMEM_BYTES = 64 * 1024 * 1024
VMEM_CAP_BYTES = V7X_VMEM_BYTES - 4 * 1024 * 1024

NT = (((1,), (1,)), ((), ()))
TN = (((0,), (0,)), ((), ()))


def _pick(n, cands):
    for c in cands:
        if n % c == 0:
            return c
    raise ValueError(f"no tile in {cands} divides {n}")


def _params(semantics, block_bytes):
    limit = min(VMEM_CAP_BYTES, block_bytes + 16 * 1024 * 1024)
    return pltpu.CompilerParams(dimension_semantics=semantics, vmem_limit_bytes=int(limit))


def _log_sigmoid(x):
    return jnp.minimum(x, 0.0) - jnp.log(1.0 + jnp.exp(-jnp.abs(x)))


def _sigmoid(x):
    return 1.0 / (1.0 + jnp.exp(-x))


def _split3(x):
    hi = x.astype(BF16)
    r1 = x - hi.astype(F32)
    mid = r1.astype(BF16)
    lo = (r1 - mid.astype(F32)).astype(BF16)
    return hi, mid, lo


def _rmsnorm_kernel(x_ref, g_ref, o_ref):
    x = x_ref[...]
    y = x * lax.rsqrt(jnp.mean(x * x, axis=-1, keepdims=True) + EPS)
    o_ref[...] = (y * g_ref[...]).astype(o_ref.dtype)


def rmsnorm(x, g, layer, out_dtype, row0=0, rows=None):
    D = x.shape[1]
    T = x.shape[0] if rows is None else rows
    tr = _pick(T, (512, 272, 256, 160, 128, 64))
    assert row0 % tr == 0
    blk0 = row0 // tr
    nbytes = 2 * tr * D * (4 + jnp.dtype(out_dtype).itemsize)
    return pl.pallas_call(
        _rmsnorm_kernel,
        grid=(T // tr,),
        in_specs=[pl.BlockSpec((tr, D), lambda i: (blk0 + i, 0)), pl.BlockSpec((None, 1, D), lambda i: (layer, 0, 0))],
        out_specs=pl.BlockSpec((tr, D), lambda i: (i, 0)),
        out_shape=jax.ShapeDtypeStruct((T, D), out_dtype),
        compiler_params=_params(("parallel",), nbytes),
        name="rmsnorm",
    )(x, g[:, None, :])


def _stack_norm_kernel(xp_ref, xs_ref, g_ref, x_ref, h_ref, *, n_prompt_tiles):
    x = jnp.where(pl.program_id(0) < n_prompt_tiles, xp_ref[...], xs_ref[...])
    x_ref[...] = x
    y = x * lax.rsqrt(jnp.mean(x * x, axis=-1, keepdims=True) + EPS)
    h_ref[...] = (y * g_ref[...]).astype(h_ref.dtype)


def stack_and_norm(x_prompt, x_sample, g, layer):
    TP, D = x_prompt.shape
    TS = x_sample.shape[0]
    tr = next(c for c in (256, 128, 64) if TP % c == 0 and TS % c == 0)
    n_p = TP // tr
    T = TP + TS
    nbytes = 2 * tr * D * (4 + 4 + 4 + 2)
    return pl.pallas_call(
        functools.partial(_stack_norm_kernel, n_prompt_tiles=n_p),
        grid=(T // tr,),
        in_specs=[pl.BlockSpec((tr, D), lambda i: (jnp.minimum(i, n_p - 1), 0)),
                  pl.BlockSpec((tr, D), lambda i: (jnp.maximum(i - n_p, 0), 0)),
                  pl.BlockSpec((None, 1, D), lambda i: (layer, 0, 0))],
        out_specs=[pl.BlockSpec((tr, D), lambda i: (i, 0)), pl.BlockSpec((tr, D), lambda i: (i, 0))],
        out_shape=[jax.ShapeDtypeStruct((T, D), F32), jax.ShapeDtypeStruct((T, D), BF16)],
        compiler_params=_params(("parallel",), nbytes),
        name="stack_norm",
    )(x_prompt, x_sample, g[:, None, :])


def _m_tile(T):
    return _pick(T, (1088, 1024, 640, 512, 320, 256, 128, 64))


def _proj_kernel(*refs, n_x, has_res, w_nk, has_cast):
    x_refs = refs[:n_x]
    w_ref = refs[n_x]
    r_ref = refs[n_x + 1] if has_res else None
    o_ref = refs[n_x + 1 + has_res + has_cast]
    if has_cast:
        refs[-1][...] = refs[n_x + 1 + has_res][...].astype(BF16)
    k0 = 0
    acc = None
    for x_ref in x_refs:
        kx = x_ref.shape[1]
        if w_nk:
            w_blk = w_ref[0, :, k0:k0 + kx] if len(w_ref.shape) == 3 else w_ref[:, k0:k0 + kx]
            part = lax.dot_general(x_ref[...], w_blk.astype(BF16), NT, preferred_element_type=F32)
        else:
            part = jnp.dot(x_ref[...], w_ref[k0:k0 + kx, :].astype(BF16), preferred_element_type=F32)
        acc = part if acc is None else acc + part
        k0 += kx
    if has_res:
        acc = r_ref[...] + acc
    o_ref[...] = acc.astype(o_ref.dtype)


def _cast_rows(n_rows, n_steps):
    return min(r for r in range(16, n_rows + 1, 16) if n_rows % r == 0 and n_rows // r <= n_steps)


def project(xs, w, layer, n_cols, out_dtype=F32, residual=None, w_nk=False, w_row0=0, half_rows=False, cast=None,
            name="project"):
    T = xs[0].shape[0]
    K = w.shape[2] if w_nk else w.shape[1]
    assert sum(x.shape[1] for x in xs) == K
    tm = _m_tile(T)
    if half_rows and tm % 32 == 0:
        tm //= 2
    tn = _pick(n_cols, (512, 256, 128))
    nj = n_cols // tn
    osz = jnp.dtype(out_dtype).itemsize
    wsz = jnp.dtype(w.dtype).itemsize
    nbytes = 2 * (tm * K * 2 + K * tn * wsz + tm * tn * osz) + K * tn * 2 + tm * tn * 4
    in_specs = [pl.BlockSpec((tm, x.shape[1]), lambda i, j: (i, 0)) for x in xs]
    if w_nk and w_row0:
        assert w_row0 % 8 == 0
        in_specs.append(pl.BlockSpec((pl.Element(1), pl.Element(tn), pl.Element(K)),
                                     lambda i, j: (layer, pl.multiple_of(w_row0 + j * tn, 8), 0)))
    elif w_nk:
        in_specs.append(pl.BlockSpec((None, tn, K), lambda i, j: (layer, j, 0)))
    else:
        assert w_row0 == 0
        in_specs.append(pl.BlockSpec((None, K, tn), lambda i, j: (layer, 0, j)))
    args = list(xs) + [w]
    if residual is not None:
        in_specs.append(pl.BlockSpec((tm, tn), lambda i, j: (i, j)))
        args.append(residual)
        nbytes += 2 * tm * tn * 4
    out_specs = [pl.BlockSpec((tm, tn), lambda i, j: (i, j))]
    out_shape = [jax.ShapeDtypeStruct((T, n_cols), out_dtype)]
    if cast is not None:
        src, src_layer = cast
        n_rows, width = src.shape[1:]
        crows = _cast_rows(n_rows, (T // tm) * nj)
        last = n_rows // crows - 1
        in_specs.append(pl.BlockSpec((None, crows, width), lambda i, j: (src_layer, jnp.minimum(i * nj + j, last), 0)))
        args.append(src)
        out_specs.append(pl.BlockSpec((crows, width), lambda i, j: (jnp.minimum(i * nj + j, last), 0)))
        out_shape.append(jax.ShapeDtypeStruct((n_rows, width), BF16))
        nbytes += 2 * crows * width * 6
    outs = pl.pallas_call(
        functools.partial(_proj_kernel, n_x=len(xs), has_res=residual is not None, w_nk=w_nk,
                          has_cast=cast is not None),
        grid=(T // tm, nj),
        in_specs=in_specs,
        out_specs=out_specs,
        out_shape=out_shape,
        compiler_params=_params(("parallel" if cast is None else "arbitrary", "arbitrary"), nbytes),
        name=name,
    )(*args)
    return outs if cast is not None else outs[0]


def _normed_kernel(*refs, n_slabs, slab_rows, n_w, w_nk, emit_h, swiglu, has_cast):
    x_ref, g_ref = refs[:2]
    w_refs = refs[2:2 + n_w]
    src_ref = refs[2 + n_w] if has_cast else None
    o_ref = refs[2 + n_w + has_cast]
    hout_ref = refs[3 + n_w + has_cast] if emit_h else None
    dst_ref = refs[-2] if has_cast else None
    h2_ref = refs[-1]
    t = pl.program_id(0)
    slab = jnp.where(t < pl.num_programs(0) - 1, jnp.minimum(pl.program_id(1), n_slabs - 1), n_slabs - 1)

    def norm_slab(buf):
        x = x_ref[...]
        y = x * lax.rsqrt(jnp.mean(x * x, axis=-1, keepdims=True) + EPS)
        yb = (y * g_ref[...]).astype(BF16)
        h2_ref[buf, pl.ds(pl.multiple_of(slab * slab_rows, slab_rows), slab_rows), :] = yb
        if emit_h:
            hout_ref[...] = yb

    def matmul(h, w_ref):
        if w_nk:
            return lax.dot_general(h, w_ref[...].astype(BF16), NT, preferred_element_type=F32)
        return jnp.dot(h, w_ref[...].astype(BF16), preferred_element_type=F32)

    @pl.when(t == 0)
    def _():
        norm_slab(0)

    @pl.when(t > 0)
    def _():
        h = h2_ref[(t - 1) % 2]
        if swiglu:
            g = matmul(h, w_refs[0])
            u = matmul(h, w_refs[1])
            o_ref[...] = (g * _sigmoid(g) * u).astype(o_ref.dtype)
        else:
            o_ref[...] = matmul(h, w_refs[0]).astype(o_ref.dtype)
        norm_slab(t % 2)
        if has_cast:
            dst_ref[...] = src_ref[...].astype(BF16)


def _slab_count(tm, nj):
    units16 = tm // 16
    return max(d for d in range(1, units16 + 1) if units16 % d == 0 and d <= nj)


def normed_project(x, g, g_layer, w, layer, n_cols, out_dtype, w_nk=False, swiglu=False, emit_h=False, cast=None,
                   name="normed_project"):
    T, K = x.shape
    tm = _pick(T, (2176, 1088, 640, 512, 256, 128, 64)) if swiglu else _m_tile(T)
    cands = [c for c in ((256, 128) if swiglu else (512, 256, 128)) if n_cols % c == 0]
    tn = next((c for c in cands if n_cols // c >= tm // CHUNK), cands[-1])
    nj = n_cols // tn
    n_tiles = T // tm
    ns = _slab_count(tm, nj)
    rows = tm // ns
    osz = jnp.dtype(out_dtype).itemsize
    n_w = 2 if swiglu else 1

    def row_tile(t):
        return jnp.maximum(t - 1, 0)

    def col(t, j):
        return jnp.where(t == 0, 0, j)

    def slab_idx(t, j):
        return jnp.where(t < n_tiles, t * ns + jnp.minimum(j, ns - 1), n_tiles * ns - 1)

    if w_nk:
        w_specs = [pl.BlockSpec((None, tn, K), lambda t, j, o=o: (layer, col(t, j) + o, 0)) for o in range(n_w)]
    else:
        w_specs = [pl.BlockSpec((None, K, tn), lambda t, j, o=o: (layer, 0, col(t, j) + o * nj)) for o in range(n_w)]
    out_specs = [pl.BlockSpec((tm, tn), lambda t, j: (row_tile(t), col(t, j)))]
    out_shape = [jax.ShapeDtypeStruct((T, n_cols), out_dtype)]
    if emit_h:
        out_specs.append(pl.BlockSpec((rows, K), lambda t, j: (slab_idx(t, j), 0)))
        out_shape.append(jax.ShapeDtypeStruct((T, K), BF16))
    nbytes = (2 * tm * K * 2 + 2 * rows * K * (4 + 2 * emit_h) + n_w * (2 * K * tn * 4 + K * tn * 2)
              + 2 * tm * tn * osz + (1 + n_w) * tm * tn * 4)
    args = [x, g[:, None, :]] + [w] * n_w
    in_specs = [pl.BlockSpec((rows, K), lambda t, j: (slab_idx(t, j), 0)),
                pl.BlockSpec((None, 1, K), lambda t, j: (g_layer, 0, 0))] + w_specs
    if cast is not None:
        src, src_layer = cast
        n_rows, width = src.shape[1:]
        crows = _cast_rows(n_rows, n_tiles * nj)
        last = n_rows // crows - 1

        def cast_idx(t, j):
            return jnp.clip((t - 1) * nj + j, 0, last)

        in_specs.append(pl.BlockSpec((None, crows, width), lambda t, j: (src_layer, cast_idx(t, j), 0)))
        args.append(src)
        out_specs.append(pl.BlockSpec((crows, width), lambda t, j: (cast_idx(t, j), 0)))
        out_shape.append(jax.ShapeDtypeStruct((n_rows, width), BF16))
        nbytes += 2 * crows * width * 6
    outs = pl.pallas_call(
        functools.partial(_normed_kernel, n_slabs=ns, slab_rows=rows, n_w=n_w, w_nk=w_nk, emit_h=emit_h,
                          swiglu=swiglu, has_cast=cast is not None),
        grid=(n_tiles + 1, nj),
        in_specs=in_specs,
        out_specs=out_specs,
        out_shape=out_shape,
        scratch_shapes=[pltpu.VMEM((2, tm, K), BF16)],
        compiler_params=_params(("arbitrary", "arbitrary"), nbytes),
        name=name,
    )(*args)
    return outs if len(outs) > 1 else outs[0]


class Units:
    def __init__(self, n_prompt, chunks_per_prompt, n_sample):
        self.ncp = chunks_per_prompt
        self.up = n_prompt * chunks_per_prompt
        self.n_prompt = n_prompt
        self.n_sample = n_sample
        self.total = self.up + n_sample
        self.n_streams = n_prompt + n_sample

    def is_sample(self, u):
        return u >= self.up

    def stream(self, u):
        return jnp.where(u < self.up, u // self.ncp, self.n_prompt + u - self.up)

    def sample_index(self, u):
        return jnp.maximum(u - self.up, 0)

    def first(self, u):
        return jnp.logical_or(u >= self.up, u % self.ncp == 0)

    def last(self, u):
        return jnp.logical_or(u >= self.up, u % self.ncp == self.ncp - 1)


def _mlstm_steps(bi_ref, bf_ref, m0_ref, q_ref, k_ref, v_ref, og_ref, gate_ref, gh_ref, c0_ref, n0_ref,
                 hm_ref, cp_ref, np_ref, mp_ref, cs_ref, ns_ref, ms_ref, c_s, n_s, m_s, *, units, n_heads, layer):
    u = pl.program_id(0)
    is_sample = units.is_sample(u)
    is_prompt = jnp.logical_not(is_sample)
    L = CHUNK
    H = n_heads
    dqk = c_s.shape[1]
    dv = c_s.shape[2]

    @pl.when(jnp.logical_and(units.first(u), is_prompt))
    def _():
        c_s[...] = jnp.zeros_like(c_s)
        n_s[...] = jnp.zeros_like(n_s)
        m_s[...] = jnp.zeros_like(m_s)

    @pl.when(is_sample)
    def _():
        c_s[...] = c0_ref[...]
        n_s[...] = n0_ref[...]
        for hh in range(H):
            m_s[hh] = jnp.full(m_s.shape[1:], m0_ref[layer * units.n_sample + units.sample_index(u), hh], F32)

    yield
    gates = gate_ref[...]
    lane = lax.broadcasted_iota(jnp.int32, gates.shape, 1)
    row = lax.broadcasted_iota(jnp.int32, (L, L), 0)
    col = lax.broadcasted_iota(jnp.int32, (L, L), 1)
    eye = row == col
    tril = col <= row
    triu = row <= col

    heads = range(H)
    qs = [slice(hh * dqk, (hh + 1) * dqk) for hh in heads]
    vs = [slice(hh * dv, (hh + 1) * dv) for hh in heads]

    ig_col = [jnp.sum(jnp.where(lane == hh, gates, 0.0), axis=1, keepdims=True) + bi_ref[layer, hh] for hh in heads]
    lf_col = [_log_sigmoid(jnp.sum(jnp.where(lane == H + hh, gates, 0.0), axis=1, keepdims=True) + bf_ref[layer, hh])
              for hh in heads]
    ig_row = [jnp.sum(jnp.where(eye, ig_col[hh], 0.0), axis=0, keepdims=True) for hh in heads]
    lf_row = [jnp.sum(jnp.where(eye, lf_col[hh], 0.0), axis=0, keepdims=True) for hh in heads]
    b_col = [jnp.sum(jnp.where(tril, lf_row[hh], 0.0), axis=1, keepdims=True) for hh in heads]
    b_row = [jnp.sum(jnp.where(triu, lf_col[hh], 0.0), axis=0, keepdims=True) for hh in heads]
    m_prev = [m_s[hh][:, :1] for hh in heads]
    m_t, w_intra, w_inter = [], [], []
    for hh in heads:
        d = jnp.where(tril, b_col[hh] - b_row[hh] + ig_row[hh], NEG_INF)
        inter = b_col[hh] + m_prev[hh]
        m_c = jnp.maximum(inter, jnp.max(d, axis=1, keepdims=True))
        m_t.append(m_c)
        w_intra.append(jnp.exp(d - m_c))
        w_inter.append(jnp.exp(inter - m_c))

    yield
    a = [lax.dot_general(q_ref[:, qs[hh]].astype(BF16),
                         (k_ref[:, qs[hh]].astype(F32) * (dqk ** -0.5)).astype(BF16), NT,
                         preferred_element_type=F32) * w_intra[hh] for hh in heads]

    yield
    hval = []
    for hh in heads:
        q = q_ref[:, qs[hh]].astype(F32)
        num = (jnp.dot(a[hh].astype(BF16), v_ref[:, vs[hh]].astype(BF16), preferred_element_type=F32)
               + jnp.dot(q_ref[:, qs[hh]].astype(BF16), c_s[hh].astype(BF16), preferred_element_type=F32)
               * w_inter[hh])
        den = (jnp.sum(a[hh], axis=1, keepdims=True)
               + jnp.sum(q * n_s[hh], axis=1, keepdims=True) * w_inter[hh])
        den = jnp.maximum(jnp.abs(den), jnp.exp(-m_t[hh]))
        hval.append(num / den)

    yield
    for hh in heads:
        y = hval[hh] * lax.rsqrt(jnp.mean(hval[hh] * hval[hh], axis=1, keepdims=True) + EPS)
        gate_out = _sigmoid(og_ref[:, vs[hh]].astype(F32))
        hm_ref[:, vs[hh]] = (y * gh_ref[:, vs[hh]] * gate_out).astype(hm_ref.dtype)

    yield
    for hh in heads:
        k = k_ref[:, qs[hh]].astype(F32) * (dqk ** -0.5)
        m_new = m_t[hh][L - 1:L, :]
        b_last = b_col[hh][L - 1:L, :]
        w_last = jnp.exp(b_last - b_col[hh] + ig_col[hh] - m_new)
        decay = jnp.exp(b_last + m_prev[hh] - m_new)
        k_w = k * w_last
        c_s[hh] = decay * c_s[hh] + lax.dot_general(k_w.astype(BF16), v_ref[:, vs[hh]].astype(BF16), TN,
                                                    preferred_element_type=F32)
        n_s[hh] = decay * n_s[hh] + jnp.sum(k_w, axis=0, keepdims=True)
        m_s[hh] = jnp.broadcast_to(m_new, m_s.shape[1:])

    yield
    @pl.when(jnp.logical_and(units.last(u), is_prompt))
    def _():
        cp_ref[...] = c_s[...]
        np_ref[...] = n_s[...]
        mp_ref[...] = m_s[...]

    @pl.when(is_sample)
    def _():
        cs_ref[...] = c_s[...]
        ns_ref[...] = n_s[...]
        ms_ref[...] = m_s[...]


def mlstm_plan(z, gates, b_i, b_f, g_head, c0, n0, m0, layer, units, states=None):
    T = z.shape[0]
    n_layers, DB, H, dqk, dv = c0.shape
    B = units.n_prompt
    qw = H * dqk
    vw = H * dv
    assert (2 * qw) % vw == 0
    v_blk = (2 * qw) // vw
    sidx = units.sample_index

    def pidx(u):
        return jnp.minimum(units.stream(u), B - 1)

    smem = pl.BlockSpec(memory_space=pltpu.SMEM)

    def state_specs(idx):
        return [pl.BlockSpec((None, None, H, dqk, dv), lambda u: (layer, idx(u), 0, 0, 0)),
                pl.BlockSpec((None, None, H, 1, dqk), lambda u: (layer, idx(u), 0, 0, 0)),
                pl.BlockSpec((None, None, H, 1, LANES), lambda u: (layer, idx(u), 0, 0, 0))]

    def state_shapes(n):
        return [jax.ShapeDtypeStruct((n_layers, n, H, dqk, dv), F32),
                jax.ShapeDtypeStruct((n_layers, n, H, 1, dqk), F32),
                jax.ShapeDtypeStruct((n_layers, n, H, 1, LANES), F32)]

    earlier = [] if states is None else list(states)

    return dict(
        steps=functools.partial(_mlstm_steps, units=units, n_heads=H, layer=layer),
        in_specs=[smem, smem, smem,
                  pl.BlockSpec((CHUNK, qw), lambda u: (u, 0)),
                  pl.BlockSpec((CHUNK, qw), lambda u: (u, 1)),
                  pl.BlockSpec((CHUNK, vw), lambda u: (u, v_blk)),
                  pl.BlockSpec((CHUNK, vw), lambda u: (u, v_blk + 1)),
                  pl.BlockSpec((CHUNK, LANES), lambda u: (u, 0)),
                  pl.BlockSpec((None, 1, vw), lambda u: (layer, 0, 0)),
                  pl.BlockSpec((None, None, H, dqk, dv), lambda u: (layer, sidx(u), 0, 0, 0)),
                  pl.BlockSpec((None, None, H, 1, dqk), lambda u: (layer, sidx(u), 0, 0, 0))]
        + [pl.BlockSpec(memory_space=pl.ANY)] * len(earlier),
        args=[b_i, b_f, m0.reshape(n_layers * DB, H), z, z, z, z, gates, g_head[:, None, :], c0,
              n0.reshape(n_layers, DB, H, 1, dqk)] + earlier,
        n_alias=len(earlier),
        out_specs=[pl.BlockSpec((CHUNK, vw), lambda u: (u, 0))] + state_specs(pidx) + state_specs(sidx),
        out_shape=[jax.ShapeDtypeStruct((T, vw), BF16)] + state_shapes(B) + state_shapes(DB),
        scratch=[pltpu.VMEM((H, dqk, dv), F32), pltpu.VMEM((H, 1, dqk), F32), pltpu.VMEM((H, 1, LANES), F32)],
        nbytes=(2 * (2 * CHUNK * qw * 2 + 2 * CHUNK * vw * 2 + CHUNK * LANES * 4 + CHUNK * vw * 2 + 3 * qw * dv * 4)
                + qw * dv * 4 * 4))


def _unit_kernels(*refs, plans):
    n_in = sum(len(p["in_specs"]) for p in plans)
    n_out = sum(len(p["out_specs"]) for p in plans)
    i0, o0, s0 = 0, n_in, n_in + n_out
    gens = []
    for p in plans:
        ni, no, nscr = len(p["in_specs"]), len(p["out_specs"]), len(p["scratch"])
        gens.append(p["steps"](*refs[i0:i0 + ni - p.get("n_alias", 0)], *refs[o0:o0 + no], *refs[s0:s0 + nscr]))
        i0, o0, s0 = i0 + ni, o0 + no, s0 + nscr
    for _ in itertools.zip_longest(*gens):
        pass


def run_unit_kernels(plans, units, name):
    aliases, i0, o0 = {}, 0, 0
    for p in plans:
        ni, na = len(p["in_specs"]), p.get("n_alias", 0)
        aliases.update({i0 + ni - na + a: o0 + 1 + a for a in range(na)})
        i0, o0 = i0 + ni, o0 + len(p["out_specs"])
    outs = pl.pallas_call(
        functools.partial(_unit_kernels, plans=plans),
        grid=(units.total,),
        in_specs=[spec for p in plans for spec in p["in_specs"]],
        out_specs=[spec for p in plans for spec in p["out_specs"]],
        out_shape=[shape for p in plans for shape in p["out_shape"]],
        scratch_shapes=[scr for p in plans for scr in p["scratch"]],
        input_output_aliases=aliases,
        compiler_params=_params(("arbitrary",), sum(p["nbytes"] for p in plans)),
        name=name,
    )(*[arg for p in plans for arg in p["args"]])
    split, k = [], 0
    for p in plans:
        split.append(outs[k:k + len(p["out_specs"])])
        k += len(p["out_specs"])
    return split


def _swa_steps(sink_ref, q_ref, k0_ref, k1_ref, k2_ref, v0_ref, v1_ref, v2_ref, ck1_ref, ck2_ref, cv1_ref, cv2_ref,
               o_ref, bias_s, *, units, n_kv, layer):
    u = pl.program_id(0)
    is_sample = units.is_sample(u)
    L = CHUNK
    W = 2 * L
    NK = W + L
    hd = SWA_HEAD_DIM
    pairs = SWA_GROUP // 2
    R = pairs * L
    n_heads = n_kv * SWA_GROUP
    pair_of_row = lax.broadcasted_iota(jnp.int32, (R, 1), 0) // L

    @pl.when(u == 0)
    def _():
        jj = lax.broadcasted_iota(jnp.int32, (R, NK), 1)
        tt = lax.broadcasted_iota(jnp.int32, (R, NK), 0) % L
        dist = jnp.abs(tt + W - jj).astype(F32)
        for g in range(n_kv):
            for odd in range(2):
                head = g * SWA_GROUP + 2 * pair_of_row + odd
                slope = jnp.exp2(-8.0 * (head + 1).astype(F32) / n_heads) * LOG2E
                pen = slope * dist
                for v in range(3):
                    bias_s[v, 2 * g + odd] = jnp.where(jj >= W - v * L, pen, -NEG_INF)

    yield
    def pick(c_ref, z_ref):
        return jnp.where(is_sample, c_ref[...], z_ref[...])

    k_all = jnp.concatenate([pick(ck2_ref, k2_ref), pick(ck1_ref, k1_ref), k0_ref[...]], axis=0)
    v_all = jnp.concatenate([pick(cv2_ref, v2_ref), pick(cv1_ref, v1_ref), v0_ref[...]], axis=0)
    variant = jnp.where(is_sample, 2, jnp.minimum(u % units.ncp, 2))
    lane = lax.broadcasted_iota(jnp.int32, (NK, LANES), 1)
    lo_lanes = lane < hd

    k_half, v_half, qg = [], [], []
    for g in range(n_kv):
        tile = (g * hd) // LANES
        kt = k_all[:, tile * LANES:(tile + 1) * LANES]
        vt = v_all[:, tile * LANES:(tile + 1) * LANES]
        kr = pltpu.roll(kt, hd, axis=1)
        vr = pltpu.roll(vt, hd, axis=1)
        if (g * hd) % LANES == 0:
            k_lo, k_hi, v_lo, v_hi = kt, kr, vt, vr
        else:
            k_lo, k_hi, v_lo, v_hi = kr, kt, vr, vt
        k_half += [jnp.where(lo_lanes, k_lo, 0.0).astype(BF16), jnp.where(lo_lanes, 0.0, k_hi).astype(BF16)]
        v_half += [jnp.where(lo_lanes, v_lo, 0.0).astype(BF16), jnp.where(lo_lanes, 0.0, v_hi).astype(BF16)]
        q = jnp.concatenate(
            [q_ref[:, (g * pairs + p) * LANES:(g * pairs + p + 1) * LANES] for p in range(pairs)], axis=0)
        qg.append((q * (hd ** -0.5 * LOG2E)).astype(BF16))

    yield
    chains = [(g, odd) for g in range(n_kv) for odd in range(2)]
    s = [lax.dot_general(qg[g], k_half[2 * g + odd], NT, preferred_element_type=F32) - bias_s[variant, 2 * g + odd]
         for g, odd in chains]
    sink = []
    for g, odd in chains:
        col = jnp.zeros((R, 1), F32)
        for p in range(pairs):
            col = jnp.where(pair_of_row == p, sink_ref[layer, g * SWA_GROUP + 2 * p + odd] * LOG2E, col)
        sink.append(col)
    yield
    mx = [jnp.maximum(jnp.max(s[c], axis=1, keepdims=True), sink[c]) for c in range(len(chains))]
    e = [jnp.exp2(s[c] - mx[c]) for c in range(len(chains))]
    inv_den = [1.0 / (jnp.sum(e[c], axis=1, keepdims=True) + jnp.exp2(sink[c] - mx[c])) for c in range(len(chains))]
    yield
    pv = [jnp.dot(e[c].astype(BF16), v_half[c], preferred_element_type=F32) * inv_den[c] for c in range(len(chains))]
    yield
    for g in range(n_kv):
        o = pv[2 * g] + pv[2 * g + 1]
        for p in range(pairs):
            o_ref[:, (g * pairs + p) * LANES:(g * pairs + p + 1) * LANES] = o[p * L:(p + 1) * L].astype(o_ref.dtype)


def swa_plan(z, sinks, cache_k, cache_v, layer, units, n_kv):
    T = z.shape[0]
    qw = n_kv * SWA_GROUP * SWA_HEAD_DIM
    kw = n_kv * SWA_HEAD_DIM
    assert kw % LANES == 0 and qw % kw == 0
    kb = qw // kw
    vb = kb + 1
    sidx = units.sample_index
    R = (SWA_GROUP // 2) * CHUNK
    NK = 3 * CHUNK

    def zspec(blk, back):
        return pl.BlockSpec((CHUNK, kw), lambda u: (jnp.maximum(u - back, 0), blk))

    def cspec(part):
        return pl.BlockSpec((None, CHUNK, kw), lambda u: (sidx(u), part, 0))

    bias_bytes = 3 * 2 * n_kv * R * 2 * LANES * 4
    return dict(
        steps=functools.partial(_swa_steps, units=units, n_kv=n_kv, layer=layer),
        in_specs=[pl.BlockSpec(memory_space=pltpu.SMEM),
                  pl.BlockSpec((CHUNK, qw), lambda u: (u, 0)),
                  zspec(kb, 0), zspec(kb, 1), zspec(kb, 2),
                  zspec(vb, 0), zspec(vb, 1), zspec(vb, 2),
                  cspec(1), cspec(0), cspec(1), cspec(0)],
        args=[sinks, z, z, z, z, z, z, z, cache_k, cache_k, cache_v, cache_v],
        out_specs=[pl.BlockSpec((CHUNK, qw), lambda u: (u, 0))],
        out_shape=[jax.ShapeDtypeStruct((T, qw), BF16)],
        scratch=[pltpu.VMEM((3, 2 * n_kv, R, NK), F32)],
        nbytes=2 * (CHUNK * qw * 4 + 10 * CHUNK * kw * 4 + CHUNK * qw * 2) + bias_bytes + 16 * R * NK * 4)


def _gla_kernel(q_ref, k_ref, v_ref, r_ref, glr_ref, wg_ref, bg_ref, gh_ref, s0_ref, o_ref, sp_ref, ss_ref, s_s, *,
                units):
    u = pl.program_id(1)
    is_sample = units.is_sample(u)
    is_prompt = jnp.logical_not(is_sample)
    L = CHUNK
    hb, dv, dk = s_s.shape

    @pl.when(jnp.logical_and(units.first(u), is_prompt))
    def _():
        s_s[...] = jnp.zeros_like(s_s)

    @pl.when(is_sample)
    def _():
        for hh in range(hb):
            s_s[hh] = s0_ref[hh].T

    glr = glr_ref[...].astype(BF16)
    row = lax.broadcasted_iota(jnp.int32, (L, L), 0)
    col = lax.broadcasted_iota(jnp.int32, (L, L), 1)
    tril = col <= row
    ones_tril = jnp.where(tril, 1.0, 0.0).astype(BF16)
    rr = lax.broadcasted_iota(jnp.int32, (L, dk), 0)
    n_sub = L // GLA_SUB

    heads = range(hb)
    ks = [slice(hh * dk, (hh + 1) * dk) for hh in heads]
    vs = [slice(hh * dv, (hh + 1) * dv) for hh in heads]
    lg = [_log_sigmoid(jnp.dot(glr, wg_ref[:, ks[hh]], preferred_element_type=F32) + bg_ref[:, ks[hh]])
          * (1.0 / GLA_TAU) for hh in heads]
    bc = [sum(jnp.dot(ones_tril, part, preferred_element_type=F32) for part in _split3(lg[hh])) for hh in heads]

    refs = [[bc[hh][i * GLA_SUB:i * GLA_SUB + 1, :] for i in range(n_sub)] for hh in heads]
    q_t = []
    for hh in heads:
        r_q = refs[hh][n_sub - 1]
        for i in range(n_sub - 2, -1, -1):
            r_q = jnp.where(rr < (i + 1) * GLA_SUB, refs[hh][i], r_q)
        q_t.append((q_ref[:, ks[hh]].astype(F32) * (dk ** -0.5) * jnp.exp(bc[hh] - r_q)).astype(BF16))
    blocks = [[] for _ in heads]
    for i in range(n_sub):
        for hh in heads:
            e_i = jnp.where(rr < (i + 1) * GLA_SUB, refs[hh][i] - bc[hh], NEG_INF)
            k_i = (k_ref[:, ks[hh]].astype(F32) * jnp.exp(e_i)).astype(BF16)
            blocks[hh].append(lax.dot_general(q_t[hh][i * GLA_SUB:(i + 1) * GLA_SUB], k_i, NT,
                                              preferred_element_type=F32))
    a = [jnp.where(tril, jnp.concatenate(blocks[hh], axis=0), 0.0).astype(BF16) for hh in heads]

    o = []
    for hh in heads:
        q_dec = (q_ref[:, ks[hh]].astype(F32) * (dk ** -0.5) * jnp.exp(bc[hh])).astype(BF16)
        o.append(jnp.dot(a[hh], v_ref[:, vs[hh]].astype(BF16), preferred_element_type=F32)
                 + lax.dot_general(q_dec, s_s[hh].astype(BF16), NT, preferred_element_type=F32))

    for hh in heads:
        b_last = bc[hh][L - 1:L, :]
        k_dec = (k_ref[:, ks[hh]].astype(F32) * jnp.exp(b_last - bc[hh])).astype(BF16)
        s_s[hh] = jnp.exp(b_last) * s_s[hh] + lax.dot_general(v_ref[:, vs[hh]].astype(BF16), k_dec, TN,
                                                               preferred_element_type=F32)

    for hh in heads:
        y = o[hh] * lax.rsqrt(jnp.mean(o[hh] * o[hh], axis=1, keepdims=True) + EPS)
        r = r_ref[:, vs[hh]].astype(F32)
        o_ref[:, vs[hh]] = (y * gh_ref[:, vs[hh]] * (r * _sigmoid(r))).astype(o_ref.dtype)

    @pl.when(jnp.logical_and(units.last(u), is_prompt))
    def _():
        for hh in range(hb):
            sp_ref[hh] = s_s[hh].T

    @pl.when(is_sample)
    def _():
        for hh in range(hb):
            ss_ref[hh] = s_s[hh].T


def gla_mixer(z, glr, w_g2, b_g, g_head, s0, layer, units, states=None):
    T = z.shape[0]
    n_layers, DB, H, dk, dv = s0.shape
    earlier = [] if states is None else list(states)
    n_in = 9
    B = units.n_prompt
    hb = GLA_HEADS_PER_STEP if H % GLA_HEADS_PER_STEP == 0 else 1
    ng = H // hb
    kw = hb * dk
    vw = hb * dv
    assert (2 * H * dk) % vw == 0
    v_blk0 = (2 * H * dk) // vw
    sidx = units.sample_index

    def pidx(u):
        return jnp.minimum(units.stream(u), B - 1)

    nbytes = (2 * (2 * CHUNK * kw * 2 + 2 * CHUNK * vw * 2 + CHUNK * LANES * 4 + LANES * kw * 2 + CHUNK * vw * 2
                   + 3 * kw * dv * 4) + kw * dv * 4 * 4)
    def body(*refs):
        _gla_kernel(*refs[:n_in], *refs[n_in + len(earlier):], units=units)

    return pl.pallas_call(
        body,
        grid=(ng, units.total),
        in_specs=[pl.BlockSpec((CHUNK, kw), lambda g, u: (u, g)),
                  pl.BlockSpec((CHUNK, kw), lambda g, u: (u, ng + g)),
                  pl.BlockSpec((CHUNK, vw), lambda g, u: (u, v_blk0 + g)),
                  pl.BlockSpec((CHUNK, vw), lambda g, u: (u, v_blk0 + ng + g)),
                  pl.BlockSpec((CHUNK, LANES), lambda g, u: (u, 0)),
                  pl.BlockSpec((LANES, kw), lambda g, u: (0, g)),
                  pl.BlockSpec((None, 1, kw), lambda g, u: (layer, 0, g)),
                  pl.BlockSpec((None, 1, vw), lambda g, u: (layer, 0, g)),
                  pl.BlockSpec((None, None, hb, dk, dv), lambda g, u: (layer, sidx(u), g, 0, 0))]
        + [pl.BlockSpec(memory_space=pl.ANY)] * len(earlier),
        out_specs=[pl.BlockSpec((CHUNK, vw), lambda g, u: (u, g)),
                   pl.BlockSpec((None, None, hb, dk, dv), lambda g, u: (layer, pidx(u), g, 0, 0)),
                   pl.BlockSpec((None, None, hb, dk, dv), lambda g, u: (layer, sidx(u), g, 0, 0))],
        out_shape=[jax.ShapeDtypeStruct((T, H * dv), BF16),
                   jax.ShapeDtypeStruct((n_layers, B, H, dk, dv), F32),
                   jax.ShapeDtypeStruct((n_layers, DB, H, dk, dv), F32)],
        scratch_shapes=[pltpu.VMEM((hb, dv, dk), F32)],
        input_output_aliases={n_in + a: 1 + a for a in range(len(earlier))},
        compiler_params=_params(("parallel", "arbitrary"), nbytes),
        name="gla",
    )(z, z, z, z, glr, w_g2, b_g[:, None, :], g_head[:, None, :], s0, *earlier)


def _pad_rows(w, height):
    return jnp.pad(w, ((0, height - w.shape[0]), (0, 0)))


def kernel(x_prompt, x_sample, cache_swa_k, cache_swa_v, state_mlstm_C, state_mlstm_n, state_mlstm_m, state_gla_S,
           norm_mix, norm_ffn, norm_final, w_even_in, b_mlstm_i, b_mlstm_f, mlstm_head_norm, swa_sinks, w_even_out,
           w_odd_in, w_gla_gate, b_gla_gate, gla_head_norm, w_odd_out, w_ffn_in, w_ffn_out):
    B, S, D = x_prompt.shape
    DB, DS, _ = x_sample.shape
    depth = norm_mix.shape[0]
    assert DS == CHUNK and S % CHUNK == 0
    MH, dqk, dv = state_mlstm_C.shape[2:]
    n_kv = cache_swa_k.shape[3]
    w_buf = cache_swa_k.shape[2]
    assert w_buf == 2 * CHUNK and cache_swa_k.shape[4] == SWA_HEAD_DIM
    GH, gdk, gdv = state_gla_S.shape[2:]
    rank = w_gla_gate.shape[1]
    m_w = 2 * MH * dqk + 2 * MH * dv
    sq_w = n_kv * SWA_GROUP * SWA_HEAD_DIM
    kvw = n_kv * SWA_HEAD_DIM
    s_w = sq_w + 2 * kvw
    g_w = 2 * GH * gdk + 2 * GH * gdv
    units = Units(B, S // CHUNK, DB)
    TP = B * S

    assert depth >= 1
    w_even_in_t = jnp.swapaxes(w_even_in, 1, 2)
    w_odd_in_t = jnp.swapaxes(w_odd_in, 1, 2)

    p_k, p_v, s_k, s_v = [], [], [], []
    mlstm_states = gla_states = None
    for l in range(depth):
        e = l // 2
        if l % 2 == 0:
            if l == 0:
                x, h = stack_and_norm(x_prompt.reshape(TP, D), x_sample.reshape(DB * DS, D), norm_mix, 0)
            else:
                h = rmsnorm(x, norm_mix, l, BF16)
            zm, w_out_b = project([h], w_even_in_t, e, m_w, BF16, w_nk=True, cast=(w_ffn_out, l),
                                  name="even_in_mlstm")
            zs = project([h], w_even_in_t, e, s_w, w_nk=True, w_row0=m_w + 2 * MH, name="even_in_attn")
            gates = project([h], w_even_in_t, e, LANES, w_nk=True, w_row0=m_w, name="even_gates")
            ck = cache_swa_k[e].reshape(DB, w_buf, kvw)
            cv = cache_swa_v[e].reshape(DB, w_buf, kvw)
            (hm, *mlstm_states), (hs,) = run_unit_kernels(
                [mlstm_plan(zm, gates, b_mlstm_i, b_mlstm_f, mlstm_head_norm, state_mlstm_C, state_mlstm_n,
                            state_mlstm_m, e, units, mlstm_states),
                 swa_plan(zs, swa_sinks, ck, cv, e, units, n_kv)], units, "even_mixers")
            x = project([hm, hs], w_even_out, e, D, residual=x, name="even_out")
            zk = zs[:, sq_w:sq_w + kvw]
            zv = zs[:, sq_w + kvw:]
            kv_shape = (w_buf, n_kv, SWA_HEAD_DIM)
            p_k.append(zk[:TP].reshape(B, S, kvw)[:, -w_buf:].reshape((B,) + kv_shape))
            p_v.append(zv[:TP].reshape(B, S, kvw)[:, -w_buf:].reshape((B,) + kv_shape))
            s_k.append(jnp.concatenate([ck[:, CHUNK:], zk[TP:].reshape(DB, DS, kvw)], axis=1).reshape((DB,) + kv_shape))
            s_v.append(jnp.concatenate([cv[:, CHUNK:], zv[TP:].reshape(DB, DS, kvw)], axis=1).reshape((DB,) + kv_shape))
        else:
            z, h, w_out_b = normed_project(x, norm_mix, l, w_odd_in_t, e, g_w, BF16, w_nk=True, emit_h=True,
                                           cast=(w_ffn_out, l), name="odd_in")
            w_rank = _pad_rows(w_odd_in_t[e, g_w:], LANES)[None]
            glr = project([h], w_rank, 0, LANES, w_nk=True, name="odd_gate_rank")
            w_g2 = jnp.pad(w_gla_gate[e], ((0, LANES - rank), (0, 0))).astype(BF16)
            mix, *gla_states = gla_mixer(z, glr, w_g2, b_gla_gate, gla_head_norm, state_gla_S, e, units, gla_states)
            x = project([mix], w_odd_out, e, D, residual=x, name="odd_out")
        act = normed_project(x, norm_ffn, l, w_ffn_in, l, w_ffn_in.shape[2] // 2, BF16, swiglu=True, name="ffn_in")
        x = project([act], w_out_b[None], 0, D, residual=x, half_rows=True, name="ffn_out")
    y_prompt = rmsnorm(x, norm_final[None], 0, F32, 0, TP)
    y_sample = rmsnorm(x, norm_final[None], 0, F32, TP, DB * DS)
    p_C, p_n, p_m, s_C, s_n, s_m = mlstm_states
    p_S, s_S = gla_states
    return (y_prompt.reshape(B, S, D), y_sample.reshape(DB, DS, D),
            jnp.stack(p_k), jnp.stack(p_v), p_C, p_n[:, :, :, 0, :], p_m[:, :, :, 0, 0], p_S,
            jnp.stack(s_k), jnp.stack(s_v), s_C, s_n[:, :, :, 0, :], s_m[:, :, :, 0, 0], s_S)
```

```python
import functools
import itertools

import jax
import jax.numpy as jnp
from jax import lax
from jax.experimental import pallas as pl
from jax.experimental.pallas import tpu as pltpu

F32 = jnp.float32
BF16 = jnp.bfloat16

CHUNK = 64
SWA_HEAD_DIM = 64
SWA_GROUP = 8
LANES = 128
GLA_SUB = 16
GLA_HEADS_PER_STEP = 8
GLA_TAU = 16.0
LOG2E = 1.4426950408889634
EPS = 1e-6
NEG_INF = float("-inf")

V7X_VMEM_BYTES = 64 * 1024 * 1024
VMEM_CAP_BYTES = V7X_VMEM_BYTES - 4 * 1024 * 1024

NT = (((1,), (1,)), ((), ()))
TN = (((0,), (0,)), ((), ()))


def _pick(n, cands):
    for c in cands:
        if n % c == 0:
            return c
    raise ValueError(f"no tile in {cands} divides {n}")


def _params(semantics, block_bytes):
    limit = min(VMEM_CAP_BYTES, block_bytes + 16 * 1024 * 1024)
    return pltpu.CompilerParams(dimension_semantics=semantics, vmem_limit_bytes=int(limit))


def _log_sigmoid(x):
    return jnp.minimum(x, 0.0) - jnp.log(1.0 + jnp.exp(-jnp.abs(x)))


def _sigmoid(x):
    return 1.0 / (1.0 + jnp.exp(-x))


def _split3(x):
    hi = x.astype(BF16)
    r1 = x - hi.astype(F32)
    mid = r1.astype(BF16)
    lo = (r1 - mid.astype(F32)).astype(BF16)
    return hi, mid, lo


def _rmsnorm_kernel(x_ref, g_ref, o_ref):
    x = x_ref[...]
    y = x * lax.rsqrt(jnp.mean(x * x, axis=-1, keepdims=True) + EPS)
    o_ref[...] = (y * g_ref[...]).astype(o_ref.dtype)


def rmsnorm(x, g, layer, out_dtype, row0=0, rows=None):
    D = x.shape[1]
    T = x.shape[0] if rows is None else rows
    tr = _pick(T, (512, 272, 256, 160, 128, 64))
    assert row0 % tr == 0
    blk0 = row0 // tr
    nbytes = 2 * tr * D * (4 + jnp.dtype(out_dtype).itemsize)
    return pl.pallas_call(
        _rmsnorm_kernel,
        grid=(T // tr,),
        in_specs=[pl.BlockSpec((tr, D), lambda i: (blk0 + i, 0)), pl.BlockSpec((None, 1, D), lambda i: (layer, 0, 0))],
        out_specs=pl.BlockSpec((tr, D), lambda i: (i, 0)),
        out_shape=jax.ShapeDtypeStruct((T, D), out_dtype),
        compiler_params=_params(("parallel",), nbytes),
        name="rmsnorm",
    )(x, g[:, None, :])


def _stack_norm_kernel(xp_ref, xs_ref, g_ref, x_ref, h_ref, *, n_prompt_tiles):
    x = jnp.where(pl.program_id(0) < n_prompt_tiles, xp_ref[...], xs_ref[...])
    x_ref[...] = x
    y = x * lax.rsqrt(jnp.mean(x * x, axis=-1, keepdims=True) + EPS)
    h_ref[...] = (y * g_ref[...]).astype(h_ref.dtype)


def stack_and_norm(x_prompt, x_sample, g, layer):
    TP, D = x_prompt.shape
    TS = x_sample.shape[0]
    tr = next(c for c in (256, 128, 64) if TP % c == 0 and TS % c == 0)
    n_p = TP // tr
    T = TP + TS
    nbytes = 2 * tr * D * (4 + 4 + 4 + 2)
    return pl.pallas_call(
        functools.partial(_stack_norm_kernel, n_prompt_tiles=n_p),
        grid=(T // tr,),
        in_specs=[pl.BlockSpec((tr, D), lambda i: (jnp.minimum(i, n_p - 1), 0)),
                  pl.BlockSpec((tr, D), lambda i: (jnp.maximum(i - n_p, 0), 0)),
                  pl.BlockSpec((None, 1, D), lambda i: (layer, 0, 0))],
        out_specs=[pl.BlockSpec((tr, D), lambda i: (i, 0)), pl.BlockSpec((tr, D), lambda i: (i, 0))],
        out_shape=[jax.ShapeDtypeStruct((T, D), F32), jax.ShapeDtypeStruct((T, D), BF16)],
        compiler_params=_params(("parallel",), nbytes),
        name="stack_norm",
    )(x_prompt, x_sample, g[:, None, :])


def _m_tile(T):
    return _pick(T, (1088, 1024, 640, 512, 320, 256, 128, 64))


def _proj_kernel(*refs, n_x, has_res, w_nk, has_cast):
    x_refs = refs[:n_x]
    w_ref = refs[n_x]
    r_ref = refs[n_x + 1] if has_res else None
    o_ref = refs[n_x + 1 + has_res + has_cast]
    if has_cast:
        refs[-1][...] = refs[n_x + 1 + has_res][...].astype(BF16)
    k0 = 0
    acc = None
    for x_ref in x_refs:
        kx = x_ref.shape[1]
        if w_nk:
            w_blk = w_ref[0, :, k0:k0 + kx] if len(w_ref.shape) == 3 else w_ref[:, k0:k0 + kx]
            part = lax.dot_general(x_ref[...], w_blk.astype(BF16), NT, preferred_element_type=F32)
        else:
            part = jnp.dot(x_ref[...], w_ref[k0:k0 + kx, :].astype(BF16), preferred_element_type=F32)
        acc = part if acc is None else acc + part
        k0 += kx
    if has_res:
        acc = r_ref[...] + acc
    o_ref[...] = acc.astype(o_ref.dtype)


def _cast_rows(n_rows, n_steps):
    return min(r for r in range(16, n_rows + 1, 16) if n_rows % r == 0 and n_rows // r <= n_steps)


def project(xs, w, layer, n_cols, out_dtype=F32, residual=None, w_nk=False, w_row0=0, half_rows=False, cast=None,
            name="project"):
    T = xs[0].shape[0]
    K = w.shape[2] if w_nk else w.shape[1]
    assert sum(x.shape[1] for x in xs) == K
    tm = _m_tile(T)
    if half_rows and tm % 32 == 0:
        tm //= 2
    tn = _pick(n_cols, (512, 256, 128))
    nj = n_cols // tn
    osz = jnp.dtype(out_dtype).itemsize
    wsz = jnp.dtype(w.dtype).itemsize
    nbytes = 2 * (tm * K * 2 + K * tn * wsz + tm * tn * osz) + K * tn * 2 + tm * tn * 4
    in_specs = [pl.BlockSpec((tm, x.shape[1]), lambda i, j: (i, 0)) for x in xs]
    if w_nk and w_row0:
        assert w_row0 % 8 == 0
        in_specs.append(pl.BlockSpec((pl.Element(1), pl.Element(tn), pl.Element(K)),
                                     lambda i, j: (layer, pl.multiple_of(w_row0 + j * tn, 8), 0)))
    elif w_nk:
        in_specs.append(pl.BlockSpec((None, tn, K), lambda i, j: (layer, j, 0)))
    else:
        assert w_row0 == 0
        in_specs.append(pl.BlockSpec((None, K, tn), lambda i, j: (layer, 0, j)))
    args = list(xs) + [w]
    if residual is not None:
        in_specs.append(pl.BlockSpec((tm, tn), lambda i, j: (i, j)))
        args.append(residual)
        nbytes += 2 * tm * tn * 4
    out_specs = [pl.BlockSpec((tm, tn), lambda i, j: (i, j))]
    out_shape = [jax.ShapeDtypeStruct((T, n_cols), out_dtype)]
    if cast is not None:
        src, src_layer = cast
        n_rows, width = src.shape[1:]
        crows = _cast_rows(n_rows, (T // tm) * nj)
        last = n_rows // crows - 1
        in_specs.append(pl.BlockSpec((None, crows, width), lambda i, j: (src_layer, jnp.minimum(i * nj + j, last), 0)))
        args.append(src)
        out_specs.append(pl.BlockSpec((crows, width), lambda i, j: (jnp.minimum(i * nj + j, last), 0)))
        out_shape.append(jax.ShapeDtypeStruct((n_rows, width), BF16))
        nbytes += 2 * crows * width * 6
    outs = pl.pallas_call(
        functools.partial(_proj_kernel, n_x=len(xs), has_res=residual is not None, w_nk=w_nk,
                          has_cast=cast is not None),
        grid=(T // tm, nj),
        in_specs=in_specs,
        out_specs=out_specs,
        out_shape=out_shape,
        compiler_params=_params(("parallel" if cast is None else "arbitrary", "arbitrary"), nbytes),
        name=name,
    )(*args)
    return outs if cast is not None else outs[0]


def _normed_kernel(*refs, n_slabs, slab_rows, n_w, w_nk, emit_h, swiglu, has_cast):
    x_ref, g_ref = refs[:2]
    w_refs = refs[2:2 + n_w]
    src_ref = refs[2 + n_w] if has_cast else None
    o_ref = refs[2 + n_w + has_cast]
    hout_ref = refs[3 + n_w + has_cast] if emit_h else None
    dst_ref = refs[-2] if has_cast else None
    h2_ref = refs[-1]
    t = pl.program_id(0)
    slab = jnp.where(t < pl.num_programs(0) - 1, jnp.minimum(pl.program_id(1), n_slabs - 1), n_slabs - 1)

    def norm_slab(buf):
        x = x_ref[...]
        y = x * lax.rsqrt(jnp.mean(x * x, axis=-1, keepdims=True) + EPS)
        yb = (y * g_ref[...]).astype(BF16)
        h2_ref[buf, pl.ds(pl.multiple_of(slab * slab_rows, slab_rows), slab_rows), :] = yb
        if emit_h:
            hout_ref[...] = yb

    def matmul(h, w_ref):
        if w_nk:
            return lax.dot_general(h, w_ref[...].astype(BF16), NT, preferred_element_type=F32)
        return jnp.dot(h, w_ref[...].astype(BF16), preferred_element_type=F32)

    @pl.when(t == 0)
    def _():
        norm_slab(0)

    @pl.when(t > 0)
    def _():
        h = h2_ref[(t - 1) % 2]
        if swiglu:
            half = h.shape[0] // 2
            for r in range(2):
                rows = slice(r * half, (r + 1) * half)
                g = matmul(h[rows], w_refs[0])
                u = matmul(h[rows], w_refs[1])
                o_ref[rows, :] = (g * _sigmoid(g) * u).astype(o_ref.dtype)
        else:
            o_ref[...] = matmul(h, w_refs[0]).astype(o_ref.dtype)
        norm_slab(t % 2)
        if has_cast:
            dst_ref[...] = src_ref[...].astype(BF16)


def _slab_count(tm, nj):
    units16 = tm // 16
    return max(d for d in range(1, units16 + 1) if units16 % d == 0 and d <= nj)


def normed_project(x, g, g_layer, w, layer, n_cols, out_dtype, w_nk=False, swiglu=False, emit_h=False, cast=None,
                   name="normed_project"):
    T, K = x.shape
    tm = _pick(T, (2176, 1088, 640, 512, 256, 128, 64)) if swiglu else _m_tile(T)
    cands = [c for c in ((256, 128) if swiglu else (512, 256, 128)) if n_cols % c == 0]
    tn = next((c for c in cands if n_cols // c >= tm // CHUNK), cands[-1])
    nj = n_cols // tn
    n_tiles = T // tm
    ns = _slab_count(tm, nj)
    rows = tm // ns
    osz = jnp.dtype(out_dtype).itemsize
    n_w = 2 if swiglu else 1

    def row_tile(t):
        return jnp.maximum(t - 1, 0)

    def col(t, j):
        return jnp.where(t == 0, 0, j)

    def slab_idx(t, j):
        return jnp.where(t < n_tiles, t * ns + jnp.minimum(j, ns - 1), n_tiles * ns - 1)

    if w_nk:
        w_specs = [pl.BlockSpec((None, tn, K), lambda t, j, o=o: (layer, col(t, j) + o, 0)) for o in range(n_w)]
    else:
        w_specs = [pl.BlockSpec((None, K, tn), lambda t, j, o=o: (layer, 0, col(t, j) + o * nj)) for o in range(n_w)]
    out_specs = [pl.BlockSpec((tm, tn), lambda t, j: (row_tile(t), col(t, j)))]
    out_shape = [jax.ShapeDtypeStruct((T, n_cols), out_dtype)]
    if emit_h:
        out_specs.append(pl.BlockSpec((rows, K), lambda t, j: (slab_idx(t, j), 0)))
        out_shape.append(jax.ShapeDtypeStruct((T, K), BF16))
    nbytes = (2 * tm * K * 2 + 2 * rows * K * (4 + 2 * emit_h) + n_w * (2 * K * tn * 4 + K * tn * 2)
              + 2 * tm * tn * osz + (1 + n_w) * tm * tn * 4)
    args = [x, g[:, None, :]] + [w] * n_w
    in_specs = [pl.BlockSpec((rows, K), lambda t, j: (slab_idx(t, j), 0)),
                pl.BlockSpec((None, 1, K), lambda t, j: (g_layer, 0, 0))] + w_specs
    if cast is not None:
        src, src_layer = cast
        n_rows, width = src.shape[1:]
        crows = _cast_rows(n_rows, n_tiles * nj)
        last = n_rows // crows - 1

        def cast_idx(t, j):
            return jnp.clip((t - 1) * nj + j, 0, last)

        in_specs.append(pl.BlockSpec((None, crows, width), lambda t, j: (src_layer, cast_idx(t, j), 0)))
        args.append(src)
        out_specs.append(pl.BlockSpec((crows, width), lambda t, j: (cast_idx(t, j), 0)))
        out_shape.append(jax.ShapeDtypeStruct((n_rows, width), BF16))
        nbytes += 2 * crows * width * 6
    outs = pl.pallas_call(
        functools.partial(_normed_kernel, n_slabs=ns, slab_rows=rows, n_w=n_w, w_nk=w_nk, emit_h=emit_h,
                          swiglu=swiglu, has_cast=cast is not None),
        grid=(n_tiles + 1, nj),
        in_specs=in_specs,
        out_specs=out_specs,
        out_shape=out_shape,
        scratch_shapes=[pltpu.VMEM((2, tm, K), BF16)],
        compiler_params=_params(("arbitrary", "arbitrary"), nbytes),
        name=name,
    )(*args)
    return outs if len(outs) > 1 else outs[0]


class Units:
    def __init__(self, n_prompt, chunks_per_prompt, n_sample):
        self.ncp = chunks_per_prompt
        self.up = n_prompt * chunks_per_prompt
        self.n_prompt = n_prompt
        self.n_sample = n_sample
        self.total = self.up + n_sample
        self.n_streams = n_prompt + n_sample

    def is_sample(self, u):
        return u >= self.up

    def stream(self, u):
        return jnp.where(u < self.up, u // self.ncp, self.n_prompt + u - self.up)

    def sample_index(self, u):
        return jnp.maximum(u - self.up, 0)

    def first(self, u):
        return jnp.logical_or(u >= self.up, u % self.ncp == 0)

    def last(self, u):
        return jnp.logical_or(u >= self.up, u % self.ncp == self.ncp - 1)


def _mlstm_steps(bi_ref, bf_ref, m0_ref, q_ref, k_ref, v_ref, og_ref, gate_ref, gh_ref, c0_ref, n0_ref,
                 hm_ref, cp_ref, np_ref, mp_ref, cs_ref, ns_ref, ms_ref, c_s, n_s, m_s, *, units, n_heads, layer):
    u = pl.program_id(0)
    is_sample = units.is_sample(u)
    is_prompt = jnp.logical_not(is_sample)
    L = CHUNK
    H = n_heads
    dqk = c_s.shape[1]
    dv = c_s.shape[2]

    @pl.when(jnp.logical_and(units.first(u), is_prompt))
    def _():
        c_s[...] = jnp.zeros_like(c_s)
        n_s[...] = jnp.zeros_like(n_s)
        m_s[...] = jnp.zeros_like(m_s)

    @pl.when(is_sample)
    def _():
        c_s[...] = c0_ref[...]
        n_s[...] = n0_ref[...]
        for hh in range(H):
            m_s[hh] = jnp.full(m_s.shape[1:], m0_ref[layer * units.n_sample + units.sample_index(u), hh], F32)

    yield
    gates = gate_ref[...]
    lane = lax.broadcasted_iota(jnp.int32, gates.shape, 1)
    row = lax.broadcasted_iota(jnp.int32, (L, L), 0)
    col = lax.broadcasted_iota(jnp.int32, (L, L), 1)
    eye = row == col
    tril = col <= row
    triu = row <= col

    heads = range(H)
    qs = [slice(hh * dqk, (hh + 1) * dqk) for hh in heads]
    vs = [slice(hh * dv, (hh + 1) * dv) for hh in heads]

    ig_col = [jnp.sum(jnp.where(lane == hh, gates, 0.0), axis=1, keepdims=True) + bi_ref[layer, hh] for hh in heads]
    lf_col = [_log_sigmoid(jnp.sum(jnp.where(lane == H + hh, gates, 0.0), axis=1, keepdims=True) + bf_ref[layer, hh])
              for hh in heads]
    ig_row = [jnp.sum(jnp.where(eye, ig_col[hh], 0.0), axis=0, keepdims=True) for hh in heads]
    lf_row = [jnp.sum(jnp.where(eye, lf_col[hh], 0.0), axis=0, keepdims=True) for hh in heads]
    b_col = [jnp.sum(jnp.where(tril, lf_row[hh], 0.0), axis=1, keepdims=True) for hh in heads]
    b_row = [jnp.sum(jnp.where(triu, lf_col[hh], 0.0), axis=0, keepdims=True) for hh in heads]
    m_prev = [m_s[hh][:, :1] for hh in heads]
    m_t, w_intra, w_inter = [], [], []
    for hh in heads:
        d = jnp.where(tril, b_col[hh] - b_row[hh] + ig_row[hh], NEG_INF)
        inter = b_col[hh] + m_prev[hh]
        m_c = jnp.maximum(inter, jnp.max(d, axis=1, keepdims=True))
        m_t.append(m_c)
        w_intra.append(jnp.exp(d - m_c))
        w_inter.append(jnp.exp(inter - m_c))

    yield
    a = [lax.dot_general(q_ref[:, qs[hh]].astype(BF16),
                         (k_ref[:, qs[hh]].astype(F32) * (dqk ** -0.5)).astype(BF16), NT,
                         preferred_element_type=F32) * w_intra[hh] for hh in heads]

    yield
    hval = []
    for hh in heads:
        q = q_ref[:, qs[hh]].astype(F32)
        num = (jnp.dot(a[hh].astype(BF16), v_ref[:, vs[hh]].astype(BF16), preferred_element_type=F32)
               + jnp.dot(q_ref[:, qs[hh]].astype(BF16), c_s[hh].astype(BF16), preferred_element_type=F32)
               * w_inter[hh])
        den = (jnp.sum(a[hh], axis=1, keepdims=True)
               + jnp.sum(q * n_s[hh], axis=1, keepdims=True) * w_inter[hh])
        den = jnp.maximum(jnp.abs(den), jnp.exp(-m_t[hh]))
        hval.append(num / den)

    yield
    for hh in heads:
        y = hval[hh] * lax.rsqrt(jnp.mean(hval[hh] * hval[hh], axis=1, keepdims=True) + EPS)
        gate_out = _sigmoid(og_ref[:, vs[hh]].astype(F32))
        hm_ref[:, vs[hh]] = (y * gh_ref[:, vs[hh]] * gate_out).astype(hm_ref.dtype)

    yield
    for hh in heads:
        k = k_ref[:, qs[hh]].astype(F32) * (dqk ** -0.5)
        m_new = m_t[hh][L - 1:L, :]
        b_last = b_col[hh][L - 1:L, :]
        w_last = jnp.exp(b_last - b_col[hh] + ig_col[hh] - m_new)
        decay = jnp.exp(b_last + m_prev[hh] - m_new)
        k_w = k * w_last
        c_s[hh] = decay * c_s[hh] + lax.dot_general(k_w.astype(BF16), v_ref[:, vs[hh]].astype(BF16), TN,
                                                    preferred_element_type=F32)
        n_s[hh] = decay * n_s[hh] + jnp.sum(k_w, axis=0, keepdims=True)
        m_s[hh] = jnp.broadcast_to(m_new, m_s.shape[1:])

    yield
    @pl.when(jnp.logical_and(units.last(u), is_prompt))
    def _():
        cp_ref[...] = c_s[...]
        np_ref[...] = n_s[...]
        mp_ref[...] = m_s[...]

    @pl.when(is_sample)
    def _():
        cs_ref[...] = c_s[...]
        ns_ref[...] = n_s[...]
        ms_ref[...] = m_s[...]


def mlstm_plan(z, gates, b_i, b_f, g_head, c0, n0, m0, layer, units):
    T = z.shape[0]
    n_layers, DB, H, dqk, dv = c0.shape
    B = units.n_prompt
    qw = H * dqk
    vw = H * dv
    assert (2 * qw) % vw == 0
    v_blk = (2 * qw) // vw
    sidx = units.sample_index

    def pidx(u):
        return jnp.minimum(units.stream(u), B - 1)

    smem = pl.BlockSpec(memory_space=pltpu.SMEM)

    def state_specs(idx):
        return [pl.BlockSpec((None, H, dqk, dv), lambda u: (idx(u), 0, 0, 0)),
                pl.BlockSpec((None, H, 1, dqk), lambda u: (idx(u), 0, 0, 0)),
                pl.BlockSpec((None, H, 1, LANES), lambda u: (idx(u), 0, 0, 0))]

    def state_shapes(n):
        return [jax.ShapeDtypeStruct((n, H, dqk, dv), F32), jax.ShapeDtypeStruct((n, H, 1, dqk), F32),
                jax.ShapeDtypeStruct((n, H, 1, LANES), F32)]

    return dict(
        steps=functools.partial(_mlstm_steps, units=units, n_heads=H, layer=layer),
        in_specs=[smem, smem, smem,
                  pl.BlockSpec((CHUNK, qw), lambda u: (u, 0)),
                  pl.BlockSpec((CHUNK, qw), lambda u: (u, 1)),
                  pl.BlockSpec((CHUNK, vw), lambda u: (u, v_blk)),
                  pl.BlockSpec((CHUNK, vw), lambda u: (u, v_blk + 1)),
                  pl.BlockSpec((CHUNK, LANES), lambda u: (u, 0)),
                  pl.BlockSpec((None, 1, vw), lambda u: (layer, 0, 0)),
                  pl.BlockSpec((None, None, H, dqk, dv), lambda u: (layer, sidx(u), 0, 0, 0)),
                  pl.BlockSpec((None, None, H, 1, dqk), lambda u: (layer, sidx(u), 0, 0, 0))],
        args=[b_i, b_f, m0.reshape(n_layers * DB, H), z, z, z, z, gates, g_head[:, None, :], c0,
              n0.reshape(n_layers, DB, H, 1, dqk)],
        out_specs=[pl.BlockSpec((CHUNK, vw), lambda u: (u, 0))] + state_specs(pidx) + state_specs(sidx),
        out_shape=[jax.ShapeDtypeStruct((T, vw), BF16)] + state_shapes(B) + state_shapes(DB),
        scratch=[pltpu.VMEM((H, dqk, dv), F32), pltpu.VMEM((H, 1, dqk), F32), pltpu.VMEM((H, 1, LANES), F32)],
        nbytes=(2 * (2 * CHUNK * qw * 2 + 2 * CHUNK * vw * 2 + CHUNK * LANES * 4 + CHUNK * vw * 2 + 3 * qw * dv * 4)
                + qw * dv * 4 * 4))


def _unit_kernels(*refs, plans):
    n_in = sum(len(p["in_specs"]) for p in plans)
    n_out = sum(len(p["out_specs"]) for p in plans)
    i0, o0, s0 = 0, n_in, n_in + n_out
    gens = []
    for p in plans:
        ni, no, nscr = len(p["in_specs"]), len(p["out_specs"]), len(p["scratch"])
        gens.append(p["steps"](*refs[i0:i0 + ni], *refs[o0:o0 + no], *refs[s0:s0 + nscr]))
        i0, o0, s0 = i0 + ni, o0 + no, s0 + nscr
    for _ in itertools.zip_longest(*gens):
        pass


def run_unit_kernels(plans, units, name):
    outs = pl.pallas_call(
        functools.partial(_unit_kernels, plans=plans),
        grid=(units.total,),
        in_specs=[spec for p in plans for spec in p["in_specs"]],
        out_specs=[spec for p in plans for spec in p["out_specs"]],
        out_shape=[shape for p in plans for shape in p["out_shape"]],
        scratch_shapes=[scr for p in plans for scr in p["scratch"]],
        compiler_params=_params(("arbitrary",), sum(p["nbytes"] for p in plans)),
        name=name,
    )(*[arg for p in plans for arg in p["args"]])
    split, k = [], 0
    for p in plans:
        split.append(outs[k:k + len(p["out_specs"])])
        k += len(p["out_specs"])
    return split


def _swa_steps(sink_ref, q_ref, k0_ref, k1_ref, k2_ref, v0_ref, v1_ref, v2_ref, ck1_ref, ck2_ref, cv1_ref, cv2_ref,
               o_ref, bias_s, *, units, n_kv, layer):
    u = pl.program_id(0)
    is_sample = units.is_sample(u)
    L = CHUNK
    W = 2 * L
    NK = W + L
    hd = SWA_HEAD_DIM
    pairs = SWA_GROUP // 2
    R = pairs * L
    n_heads = n_kv * SWA_GROUP
    pair_of_row = lax.broadcasted_iota(jnp.int32, (R, 1), 0) // L

    @pl.when(u == 0)
    def _():
        jj = lax.broadcasted_iota(jnp.int32, (R, NK), 1)
        tt = lax.broadcasted_iota(jnp.int32, (R, NK), 0) % L
        dist = jnp.abs(tt + W - jj).astype(F32)
        for g in range(n_kv):
            for odd in range(2):
                head = g * SWA_GROUP + 2 * pair_of_row + odd
                slope = jnp.exp2(-8.0 * (head + 1).astype(F32) / n_heads) * LOG2E
                pen = slope * dist
                for v in range(3):
                    bias_s[v, 2 * g + odd] = jnp.where(jj >= W - v * L, pen, -NEG_INF)

    yield
    def pick(c_ref, z_ref):
        return jnp.where(is_sample, c_ref[...], z_ref[...])

    k_all = jnp.concatenate([pick(ck2_ref, k2_ref), pick(ck1_ref, k1_ref), k0_ref[...]], axis=0)
    v_all = jnp.concatenate([pick(cv2_ref, v2_ref), pick(cv1_ref, v1_ref), v0_ref[...]], axis=0)
    variant = jnp.where(is_sample, 2, jnp.minimum(u % units.ncp, 2))
    lane = lax.broadcasted_iota(jnp.int32, (NK, LANES), 1)
    lo_lanes = lane < hd

    k_half, v_half, qg = [], [], []
    for g in range(n_kv):
        tile = (g * hd) // LANES
        kt = k_all[:, tile * LANES:(tile + 1) * LANES]
        vt = v_all[:, tile * LANES:(tile + 1) * LANES]
        kr = pltpu.roll(kt, hd, axis=1)
        vr = pltpu.roll(vt, hd, axis=1)
        if (g * hd) % LANES == 0:
            k_lo, k_hi, v_lo, v_hi = kt, kr, vt, vr
        else:
            k_lo, k_hi, v_lo, v_hi = kr, kt, vr, vt
        k_half += [jnp.where(lo_lanes, k_lo, 0.0).astype(BF16), jnp.where(lo_lanes, 0.0, k_hi).astype(BF16)]
        v_half += [jnp.where(lo_lanes, v_lo, 0.0).astype(BF16), jnp.where(lo_lanes, 0.0, v_hi).astype(BF16)]
        q = jnp.concatenate(
            [q_ref[:, (g * pairs + p) * LANES:(g * pairs + p + 1) * LANES] for p in range(pairs)], axis=0)
        qg.append((q * (hd ** -0.5 * LOG2E)).astype(BF16))

    yield
    chains = [(g, odd) for g in range(n_kv) for odd in range(2)]
    s = [lax.dot_general(qg[g], k_half[2 * g + odd], NT, preferred_element_type=F32) - bias_s[variant, 2 * g + odd]
         for g, odd in chains]
    sink = []
    for g, odd in chains:
        col = jnp.zeros((R, 1), F32)
        for p in range(pairs):
            col = jnp.where(pair_of_row == p, sink_ref[layer, g * SWA_GROUP + 2 * p + odd] * LOG2E, col)
        sink.append(col)
    yield
    mx = [jnp.maximum(jnp.max(s[c], axis=1, keepdims=True), sink[c]) for c in range(len(chains))]
    e = [jnp.exp2(s[c] - mx[c]) for c in range(len(chains))]
    inv_den = [1.0 / (jnp.sum(e[c], axis=1, keepdims=True) + jnp.exp2(sink[c] - mx[c])) for c in range(len(chains))]
    yield
    pv = [jnp.dot(e[c].astype(BF16), v_half[c], preferred_element_type=F32) * inv_den[c] for c in range(len(chains))]
    yield
    for g in range(n_kv):
        o = pv[2 * g] + pv[2 * g + 1]
        for p in range(pairs):
            o_ref[:, (g * pairs + p) * LANES:(g * pairs + p + 1) * LANES] = o[p * L:(p + 1) * L].astype(o_ref.dtype)


def swa_plan(z, sinks, cache_k, cache_v, layer, units, n_kv):
    T = z.shape[0]
    qw = n_kv * SWA_GROUP * SWA_HEAD_DIM
    kw = n_kv * SWA_HEAD_DIM
    assert kw % LANES == 0 and qw % kw == 0
    kb = qw // kw
    vb = kb + 1
    sidx = units.sample_index
    R = (SWA_GROUP // 2) * CHUNK
    NK = 3 * CHUNK

    def zspec(blk, back):
        return pl.BlockSpec((CHUNK, kw), lambda u: (jnp.maximum(u - back, 0), blk))

    def cspec(part):
        return pl.BlockSpec((None, CHUNK, kw), lambda u: (sidx(u), part, 0))

    bias_bytes = 3 * 2 * n_kv * R * 2 * LANES * 4
    return dict(
        steps=functools.partial(_swa_steps, units=units, n_kv=n_kv, layer=layer),
        in_specs=[pl.BlockSpec(memory_space=pltpu.SMEM),
                  pl.BlockSpec((CHUNK, qw), lambda u: (u, 0)),
                  zspec(kb, 0), zspec(kb, 1), zspec(kb, 2),
                  zspec(vb, 0), zspec(vb, 1), zspec(vb, 2),
                  cspec(1), cspec(0), cspec(1), cspec(0)],
        args=[sinks, z, z, z, z, z, z, z, cache_k, cache_k, cache_v, cache_v],
        out_specs=[pl.BlockSpec((CHUNK, qw), lambda u: (u, 0))],
        out_shape=[jax.ShapeDtypeStruct((T, qw), BF16)],
        scratch=[pltpu.VMEM((3, 2 * n_kv, R, NK), F32)],
        nbytes=2 * (CHUNK * qw * 4 + 10 * CHUNK * kw * 4 + CHUNK * qw * 2) + bias_bytes + 16 * R * NK * 4)


def _gla_kernel(q_ref, k_ref, v_ref, r_ref, glr_ref, wg_ref, bg_ref, gh_ref, s0_ref, o_ref, sp_ref, ss_ref, s_s, *,
                units):
    u = pl.program_id(1)
    is_sample = units.is_sample(u)
    is_prompt = jnp.logical_not(is_sample)
    L = CHUNK
    hb, dv, dk = s_s.shape

    @pl.when(jnp.logical_and(units.first(u), is_prompt))
    def _():
        s_s[...] = jnp.zeros_like(s_s)

    @pl.when(is_sample)
    def _():
        for hh in range(hb):
            s_s[hh] = s0_ref[hh].T

    glr = glr_ref[...].astype(BF16)
    row = lax.broadcasted_iota(jnp.int32, (L, L), 0)
    col = lax.broadcasted_iota(jnp.int32, (L, L), 1)
    tril = col <= row
    ones_tril = jnp.where(tril, 1.0, 0.0).astype(BF16)
    rr = lax.broadcasted_iota(jnp.int32, (L, dk), 0)
    n_sub = L // GLA_SUB

    heads = range(hb)
    ks = [slice(hh * dk, (hh + 1) * dk) for hh in heads]
    vs = [slice(hh * dv, (hh + 1) * dv) for hh in heads]
    lg = [_log_sigmoid(jnp.dot(glr, wg_ref[:, ks[hh]], preferred_element_type=F32) + bg_ref[:, ks[hh]])
          * (1.0 / GLA_TAU) for hh in heads]
    bc = [sum(jnp.dot(ones_tril, part, preferred_element_type=F32) for part in _split3(lg[hh])) for hh in heads]

    refs = [[bc[hh][i * GLA_SUB:i * GLA_SUB + 1, :] for i in range(n_sub)] for hh in heads]
    q_t = []
    for hh in heads:
        r_q = refs[hh][n_sub - 1]
        for i in range(n_sub - 2, -1, -1):
            r_q = jnp.where(rr < (i + 1) * GLA_SUB, refs[hh][i], r_q)
        q_t.append((q_ref[:, ks[hh]].astype(F32) * (dk ** -0.5) * jnp.exp(bc[hh] - r_q)).astype(BF16))
    blocks = [[] for _ in heads]
    for i in range(n_sub):
        for hh in heads:
            e_i = jnp.where(rr < (i + 1) * GLA_SUB, refs[hh][i] - bc[hh], NEG_INF)
            k_i = (k_ref[:, ks[hh]].astype(F32) * jnp.exp(e_i)).astype(BF16)
            blocks[hh].append(lax.dot_general(q_t[hh][i * GLA_SUB:(i + 1) * GLA_SUB], k_i, NT,
                                              preferred_element_type=F32))
    a = [jnp.where(tril, jnp.concatenate(blocks[hh], axis=0), 0.0).astype(BF16) for hh in heads]

    o = []
    for hh in heads:
        q_dec = (q_ref[:, ks[hh]].astype(F32) * (dk ** -0.5) * jnp.exp(bc[hh])).astype(BF16)
        o.append(jnp.dot(a[hh], v_ref[:, vs[hh]].astype(BF16), preferred_element_type=F32)
                 + lax.dot_general(q_dec, s_s[hh].astype(BF16), NT, preferred_element_type=F32))

    for hh in heads:
        b_last = bc[hh][L - 1:L, :]
        k_dec = (k_ref[:, ks[hh]].astype(F32) * jnp.exp(b_last - bc[hh])).astype(BF16)
        s_s[hh] = jnp.exp(b_last) * s_s[hh] + lax.dot_general(v_ref[:, vs[hh]].astype(BF16), k_dec, TN,
                                                               preferred_element_type=F32)

    for hh in heads:
        y = o[hh] * lax.rsqrt(jnp.mean(o[hh] * o[hh], axis=1, keepdims=True) + EPS)
        r = r_ref[:, vs[hh]].astype(F32)
        o_ref[:, vs[hh]] = (y * gh_ref[:, vs[hh]] * (r * _sigmoid(r))).astype(o_ref.dtype)

    @pl.when(jnp.logical_and(units.last(u), is_prompt))
    def _():
        for hh in range(hb):
            sp_ref[hh] = s_s[hh].T

    @pl.when(is_sample)
    def _():
        for hh in range(hb):
            ss_ref[hh] = s_s[hh].T


def gla_mixer(z, glr, w_g2, b_g, g_head, s0, layer, units):
    T = z.shape[0]
    _, DB, H, dk, dv = s0.shape
    B = units.n_prompt
    hb = GLA_HEADS_PER_STEP if H % GLA_HEADS_PER_STEP == 0 else 1
    ng = H // hb
    kw = hb * dk
    vw = hb * dv
    assert (2 * H * dk) % vw == 0
    v_blk0 = (2 * H * dk) // vw
    sidx = units.sample_index

    def pidx(u):
        return jnp.minimum(units.stream(u), B - 1)

    nbytes = (2 * (2 * CHUNK * kw * 2 + 2 * CHUNK * vw * 2 + CHUNK * LANES * 4 + LANES * kw * 2 + CHUNK * vw * 2
                   + 3 * kw * dv * 4) + kw * dv * 4 * 4)
    return pl.pallas_call(
        functools.partial(_gla_kernel, units=units),
        grid=(ng, units.total),
        in_specs=[pl.BlockSpec((CHUNK, kw), lambda g, u: (u, g)),
                  pl.BlockSpec((CHUNK, kw), lambda g, u: (u, ng + g)),
                  pl.BlockSpec((CHUNK, vw), lambda g, u: (u, v_blk0 + g)),
                  pl.BlockSpec((CHUNK, vw), lambda g, u: (u, v_blk0 + ng + g)),
                  pl.BlockSpec((CHUNK, LANES), lambda g, u: (u, 0)),
                  pl.BlockSpec((LANES, kw), lambda g, u: (0, g)),
                  pl.BlockSpec((None, 1, kw), lambda g, u: (layer, 0, g)),
                  pl.BlockSpec((None, 1, vw), lambda g, u: (layer, 0, g)),
                  pl.BlockSpec((None, None, hb, dk, dv), lambda g, u: (layer, sidx(u), g, 0, 0))],
        out_specs=[pl.BlockSpec((CHUNK, vw), lambda g, u: (u, g)),
                   pl.BlockSpec((None, hb, dk, dv), lambda g, u: (pidx(u), g, 0, 0)),
                   pl.BlockSpec((None, hb, dk, dv), lambda g, u: (sidx(u), g, 0, 0))],
        out_shape=[jax.ShapeDtypeStruct((T, H * dv), BF16),
                   jax.ShapeDtypeStruct((B, H, dk, dv), F32),
                   jax.ShapeDtypeStruct((DB, H, dk, dv), F32)],
        scratch_shapes=[pltpu.VMEM((hb, dv, dk), F32)],
        compiler_params=_params(("parallel", "arbitrary"), nbytes),
        name="gla",
    )(z, z, z, z, glr, w_g2, b_g[:, None, :], g_head[:, None, :], s0)


def _pad_rows(w, height):
    return jnp.pad(w, ((0, height - w.shape[0]), (0, 0)))


def kernel(x_prompt, x_sample, cache_swa_k, cache_swa_v, state_mlstm_C, state_mlstm_n, state_mlstm_m, state_gla_S,
           norm_mix, norm_ffn, norm_final, w_even_in, b_mlstm_i, b_mlstm_f, mlstm_head_norm, swa_sinks, w_even_out,
           w_odd_in, w_gla_gate, b_gla_gate, gla_head_norm, w_odd_out, w_ffn_in, w_ffn_out):
    B, S, D = x_prompt.shape
    DB, DS, _ = x_sample.shape
    depth = norm_mix.shape[0]
    assert DS == CHUNK and S % CHUNK == 0
    MH, dqk, dv = state_mlstm_C.shape[2:]
    n_kv = cache_swa_k.shape[3]
    w_buf = cache_swa_k.shape[2]
    assert w_buf == 2 * CHUNK and cache_swa_k.shape[4] == SWA_HEAD_DIM
    GH, gdk, gdv = state_gla_S.shape[2:]
    rank = w_gla_gate.shape[1]
    m_w = 2 * MH * dqk + 2 * MH * dv
    sq_w = n_kv * SWA_GROUP * SWA_HEAD_DIM
    kvw = n_kv * SWA_HEAD_DIM
    s_w = sq_w + 2 * kvw
    g_w = 2 * GH * gdk + 2 * GH * gdv
    units = Units(B, S // CHUNK, DB)
    TP = B * S

    assert depth >= 1
    w_even_in_t = jnp.swapaxes(w_even_in, 1, 2)
    w_odd_in_t = jnp.swapaxes(w_odd_in, 1, 2)

    p_k, p_v, p_C, p_n, p_m, p_S = [], [], [], [], [], []
    s_k, s_v, s_C, s_n, s_m, s_S = [], [], [], [], [], []
    for l in range(depth):
        e = l // 2
        if l % 2 == 0:
            if l == 0:
                x, h = stack_and_norm(x_prompt.reshape(TP, D), x_sample.reshape(DB * DS, D), norm_mix, 0)
            else:
                h = rmsnorm(x, norm_mix, l, BF16)
            zm, w_out_b = project([h], w_even_in_t, e, m_w, BF16, w_nk=True, cast=(w_ffn_out, l),
                                  name="even_in_mlstm")
            zs = project([h], w_even_in_t, e, s_w, w_nk=True, w_row0=m_w + 2 * MH, name="even_in_attn")
            gates = project([h], w_even_in_t, e, LANES, w_nk=True, w_row0=m_w, name="even_gates")
            ck = cache_swa_k[e].reshape(DB, w_buf, kvw)
            cv = cache_swa_v[e].reshape(DB, w_buf, kvw)
            (hm, pc, pn, pm, sc, sn, sm), (hs,) = run_unit_kernels(
                [mlstm_plan(zm, gates, b_mlstm_i, b_mlstm_f, mlstm_head_norm, state_mlstm_C, state_mlstm_n,
                            state_mlstm_m, e, units),
                 swa_plan(zs, swa_sinks, ck, cv, e, units, n_kv)], units, "even_mixers")
            pn, pm, sn, sm = pn[:, :, 0, :], pm[:, :, 0, 0], sn[:, :, 0, :], sm[:, :, 0, 0]
            x = project([hm, hs], w_even_out, e, D, residual=x, name="even_out")
            zk = zs[:, sq_w:sq_w + kvw]
            zv = zs[:, sq_w + kvw:]
            kv_shape = (w_buf, n_kv, SWA_HEAD_DIM)
            p_k.append(zk[:TP].reshape(B, S, kvw)[:, -w_buf:].reshape((B,) + kv_shape))
            p_v.append(zv[:TP].reshape(B, S, kvw)[:, -w_buf:].reshape((B,) + kv_shape))
            s_k.append(jnp.concatenate([ck[:, CHUNK:], zk[TP:].reshape(DB, DS, kvw)], axis=1).reshape((DB,) + kv_shape))
            s_v.append(jnp.concatenate([cv[:, CHUNK:], zv[TP:].reshape(DB, DS, kvw)], axis=1).reshape((DB,) + kv_shape))
            p_C.append(pc); p_n.append(pn); p_m.append(pm)
            s_C.append(sc); s_n.append(sn); s_m.append(sm)
        else:
            z, h, w_out_b = normed_project(x, norm_mix, l, w_odd_in_t, e, g_w, BF16, w_nk=True, emit_h=True,
                                           cast=(w_ffn_out, l), name="odd_in")
            w_rank = _pad_rows(w_odd_in_t[e, g_w:], LANES)[None]
            glr = project([h], w_rank, 0, LANES, w_nk=True, name="odd_gate_rank")
            w_g2 = jnp.pad(w_gla_gate[e], ((0, LANES - rank), (0, 0))).astype(BF16)
            mix, ps, ss = gla_mixer(z, glr, w_g2, b_gla_gate, gla_head_norm, state_gla_S, e, units)
            x = project([mix], w_odd_out, e, D, residual=x, name="odd_out")
            p_S.append(ps); s_S.append(ss)
        act = normed_project(x, norm_ffn, l, w_ffn_in, l, w_ffn_in.shape[2] // 2, BF16, swiglu=True, name="ffn_in")
        x = project([act], w_out_b[None], 0, D, residual=x, half_rows=True, name="ffn_out")
    y_prompt = rmsnorm(x, norm_final[None], 0, F32, 0, TP)
    y_sample = rmsnorm(x, norm_final[None], 0, F32, TP, DB * DS)
    return (y_prompt.reshape(B, S, D), y_sample.reshape(DB, DS, D),
            jnp.stack(p_k), jnp.stack(p_v), jnp.stack(p_C), jnp.stack(p_n), jnp.stack(p_m), jnp.stack(p_S),
            jnp.stack(s_k), jnp.stack(s_v), jnp.stack(s_C), jnp.stack(s_n), jnp.stack(s_m), jnp.stack(s_S))
```
